```python
import jax, jax.numpy as jnp
from jax import lax
import numpy as np

D_MODEL = 2048
BATCH = 4
SEQ = 4096
DEPTH = 1

RWKV_HEADS = 16
RWKV_HEAD_DIM = 64
RWKV_WIDTH = RWKV_HEADS * RWKV_HEAD_DIM
DECAY_LORA = 64
ICLR_LORA = 64
GATE_LORA = 160
RWKV_GN_EPS = 64e-5

ATTN_GROUPS = ((128, 1), (512, 4), (2048, 16))
N_GROUPS = 3
HEADS_PER_GROUP = 4
ATTN_HEAD_DIM = 128
ATTN_WIDTH = N_GROUPS * HEADS_PER_GROUP * ATTN_HEAD_DIM
ATTN_OUT_WIDTH = HEADS_PER_GROUP * ATTN_HEAD_DIM
ROPE_DIM = ATTN_HEAD_DIM // 4
ROPE_THETA = 500000.0

N_EXPERTS = 32
TOP_K = 4
EXPERT_FF = 2048
SWIGLU_LIMIT = 7.0
SWIGLU_ALPHA = 1.702

NORM_EPS = 1e-5

SHIFT_WIDTH = 3 * RWKV_WIDTH + DECAY_LORA + ICLR_LORA + GATE_LORA
IN_WIDTH = SHIFT_WIDTH + 3 * ATTN_WIDTH + 2 * D_MODEL

kernel_name = "hybrid_rwkv7_dilated_attn_moe_block"


def rms_norm(x, g):
    xf = x.astype(jnp.float32)
    y = xf * lax.rsqrt(jnp.mean(xf * xf, axis=-1, keepdims=True) + NORM_EPS)
    return (y * g.astype(jnp.float32)).astype(x.dtype)


def split_cols(t, widths):
    outs, off = [], 0
    for wd in widths:
        outs.append(t[..., off:off + wd])
        off += wd
    return outs


def token_shift(z):
    return jnp.pad(z, ((0, 0), (1, 0), (0, 0)))[:, :-1]


def rwkv7_scan(r, w, k, v, kk, a):
    B, S, H, N = r.shape

    def step(state, inp):
        r_t, w_t, k_t, v_t, kk_t, a_t = inp
        sa = jnp.einsum('bhvk,bhk->bhv', state, -kk_t)
        state = (state * w_t[:, :, None, :]
                 + sa[..., None] * (kk_t * a_t)[:, :, None, :]
                 + v_t[..., None] * k_t[:, :, None, :])
        y_t = jnp.einsum('bhvk,bhk->bhv', state, r_t)
        return state, y_t

    xs = tuple(jnp.swapaxes(t, 0, 1) for t in (r, w, k, v, kk, a))
    state0 = jnp.zeros((B, H, N, N), jnp.float32)
    _, ys = lax.scan(step, state0, xs)
    return jnp.swapaxes(ys, 0, 1)


def rwkv7_branch(zr, zk, zv, zw, za, zg, w0, w2_decay, a0, a2_iclr, g2_gate,
                 k_k, k_a, r_k, ln_x_w, ln_x_b, w_o_rwkv):
    B, S, _ = zr.shape
    H, N = RWKV_HEADS, RWKV_HEAD_DIM
    f32 = jnp.float32
    r = zr.astype(f32)
    k = zk.astype(f32)
    v = zv.astype(f32)
    w_raw = w0 + jnp.tanh(zw.astype(f32)) @ w2_decay
    log_w = -jax.nn.softplus(-w_raw) - 0.5
    decay = jnp.exp(-jnp.exp(log_w))
    a = jax.nn.sigmoid(a0 + za.astype(f32) @ a2_iclr)
    g = jax.nn.sigmoid(zg.astype(f32)) @ g2_gate

    heads = lambda t: t.reshape(B, S, H, N)
    kk = heads(k * k_k)
    kk = kk / jnp.maximum(jnp.sqrt(jnp.sum(kk * kk, axis=-1, keepdims=True)), 1e-12)
    k = k * (1.0 + (a - 1.0) * k_a)
    r, k, v, a, decay = heads(r), heads(k), heads(v), heads(a), heads(decay)

    y = rwkv7_scan(r, decay, k, v, kk, a)
    mean = jnp.mean(y, axis=-1, keepdims=True)
    var = jnp.var(y, axis=-1, keepdims=True)
    y = ((y - mean) * lax.rsqrt(var + RWKV_GN_EPS)).reshape(B, S, RWKV_WIDTH)
    y = y * ln_x_w + ln_x_b
    bonus = jnp.sum(r * k * r_k, axis=-1, keepdims=True) * v
    y = y + bonus.reshape(B, S, RWKV_WIDTH)
    return (y * g) @ w_o_rwkv


def partial_rotary(t, positions):
    half = ROPE_DIM // 2
    inv_freq = ROPE_THETA ** (-jnp.arange(half, dtype=jnp.float32) / half)
    ang = positions.astype(jnp.float32)[..., None] * inv_freq
    cos = jnp.cos(ang)[:, :, None, None, :]
    sin = jnp.sin(ang)[:, :, None, None, :]
    t = t.astype(jnp.float32)
    x1 = t[..., :half]
    x2 = t[..., half:ROPE_DIM]
    return jnp.concatenate([x1 * cos - x2 * sin, x2 * cos + x1 * sin, t[..., ROPE_DIM:]], axis=-1)


def dilated_window_attention(q, k, v, window, dilation):
    B, S, H, E = q.shape
    blk = window // dilation
    L = S // dilation
    nb = -(-L // blk)
    Lp = nb * blk

    def to_blocks(t):
        t = t.reshape(B, L, dilation, H, E).transpose(0, 2, 3, 1, 4)
        t = jnp.pad(t, ((0, 0), (0, 0), (0, 0), (0, Lp - L), (0, 0)))
        return t.reshape(B, dilation, H, nb, blk, E)

    def with_prev(t):
        prev = jnp.pad(t, ((0, 0), (0, 0), (0, 0), (1, 0), (0, 0), (0, 0)))[:, :, :, :-1]
        return jnp.concatenate([prev, t], axis=4)

    qb, kb, vb = to_blocks(q), to_blocks(k), to_blocks(v)
    kc, vc = with_prev(kb), with_prev(vb)
    s = jnp.einsum('bdhnqe,bdhnke->bdhnqk', qb, kc) * (E ** -0.5)
    qi = jnp.arange(blk)[:, None]
    kj = jnp.arange(2 * blk)[None, :]
    dist = blk + qi - kj
    band = (dist >= 0) & (dist <= blk)
    key_sub = jnp.arange(nb)[:, None, None] * blk + kj[None] - blk
    mask = band[None] & (key_sub >= 0)
    s = jnp.where(mask, s.astype(jnp.float32), -jnp.inf)
    m = jnp.max(s, axis=-1, keepdims=True)
    p = jnp.exp(s - m)
    l = jnp.sum(p, axis=-1)
    o = jnp.einsum('bdhnqk,bdhnke->bdhnqe', p, vc) / l[..., None]

    def from_blocks(t):
        t = t.reshape((B, dilation, H, Lp) + t.shape[5:])[:, :, :, :L]
        t = jnp.moveaxis(t, 3, 1)
        return t.reshape((B, S, H) + t.shape[4:])

    return from_blocks(o), from_blocks(m[..., 0]), from_blocks(l)


def hybrid_mixer(h, positions, w_in, shift_mu, w0, w2_decay, a0, a2_iclr, g2_gate,
                 k_k, k_a, r_k, ln_x_w, ln_x_b, w_o_rwkv, w_o_attn, w_out):
    B, S, _ = h.shape
    proj = h @ w_in

    z = proj[..., :SHIFT_WIDTH]
    z = z + (token_shift(z) - z) * shift_mu
    zr, zk, zv, zw, za, zg = split_cols(
        z, (RWKV_WIDTH, RWKV_WIDTH, RWKV_WIDTH, DECAY_LORA, ICLR_LORA, GATE_LORA))
    y_a = rwkv7_branch(zr, zk, zv, zw, za, zg, w0, w2_decay, a0, a2_iclr, g2_gate,
                       k_k, k_a, r_k, ln_x_w, ln_x_b, w_o_rwkv)

    q, k, v, gate_a, gate_b = split_cols(
        proj[..., SHIFT_WIDTH:], (ATTN_WIDTH, ATTN_WIDTH, ATTN_WIDTH, D_MODEL, D_MODEL))
    shape5 = (B, S, N_GROUPS, HEADS_PER_GROUP, ATTN_HEAD_DIM)
    q = partial_rotary(q.reshape(shape5), positions)
    k = partial_rotary(k.reshape(shape5), positions)
    v = v.reshape(shape5).astype(jnp.float32)
    outs, ms, ls = [], [], []
    for gi, (window, dilation) in enumerate(ATTN_GROUPS):
        o_g, m_g, l_g = dilated_window_attention(q[:, :, gi], k[:, :, gi], v[:, :, gi], window, dilation)
        outs.append(o_g)
        ms.append(m_g)
        ls.append(l_g)
    m_all = jnp.stack(ms)
    wts = jnp.stack(ls) * jnp.exp(m_all - jnp.max(m_all, axis=0, keepdims=True))
    o = jnp.einsum('gbsh,gbshe->bshe', wts, jnp.stack(outs)) / jnp.sum(wts, axis=0)[..., None]
    y_b = o.reshape(B, S, ATTN_OUT_WIDTH) @ w_o_attn

    merged = (jax.nn.sigmoid(gate_a.astype(jnp.float32)) * y_a
              + jax.nn.sigmoid(gate_b.astype(jnp.float32)) * y_b)
    return merged @ w_out


def moe_ffn(h, w_router, b_router, w1, b1, w2, b2):
    B, S, D = h.shape
    t = h.reshape(-1, D)
    T = t.shape[0]
    logits = (t @ w_router).astype(jnp.float32) + b_router.astype(jnp.float32)
    top_val, top_idx = lax.top_k(logits, TOP_K)
    gates = jax.nn.softmax(top_val, axis=-1)
    flat_e = top_idx.reshape(-1)
    order = jnp.argsort(flat_e)
    e_sorted = flat_e[order]
    tok = order // TOP_K
    group_sizes = jnp.bincount(flat_e, length=N_EXPERTS).astype(jnp.int32)
    xs = t[tok]
    hid = lax.ragged_dot(xs, w1, group_sizes) + b1[e_sorted]
    x_glu = jnp.minimum(hid[:, :EXPERT_FF], SWIGLU_LIMIT)
    x_lin = jnp.clip(hid[:, EXPERT_FF:], -SWIGLU_LIMIT, SWIGLU_LIMIT)
    act = x_glu * jax.nn.sigmoid(SWIGLU_ALPHA * x_glu) * (x_lin + 1.0)
    out = lax.ragged_dot(act.astype(w2.dtype), w2, group_sizes) + b2[e_sorted]
    out = out.astype(jnp.float32) * gates.reshape(-1)[order][:, None]
    y = jnp.zeros((T, D), jnp.float32).at[tok].add(out)
    return y.reshape(B, S, D)


def setup_inputs(seed: int = 0) -> dict:
    key = jax.random.key(seed)
    ks = jax.random.split(key, 32)
    f32 = jnp.float32
    nrm = lambda k, shape, scale: jax.random.normal(k, shape, f32) * scale
    x = nrm(ks[0], (BATCH, SEQ, D_MODEL), 1.0)
    offsets = jax.random.randint(ks[1], (BATCH, 1), 0, 4096, dtype=jnp.int32)
    positions = (offsets + jnp.arange(SEQ, dtype=jnp.int32)[None, :]).astype(jnp.int32)
    return {
        "x": x,
        "positions": positions,
        "norm_mix": 1.0 + nrm(ks[2], (DEPTH, D_MODEL), 0.02),
        "w_in": nrm(ks[3], (DEPTH, D_MODEL, IN_WIDTH), D_MODEL ** -0.5),
        "shift_mu": jax.random.uniform(ks[4], (DEPTH, SHIFT_WIDTH), f32),
        "w0": jax.random.uniform(ks[5], (DEPTH, RWKV_WIDTH), f32, -6.0, 1.0),
        "w2_decay": nrm(ks[6], (DEPTH, DECAY_LORA, RWKV_WIDTH), 0.5 * DECAY_LORA ** -0.5),
        "a0": nrm(ks[7], (DEPTH, RWKV_WIDTH), 0.5),
        "a2_iclr": nrm(ks[8], (DEPTH, ICLR_LORA, RWKV_WIDTH), 0.5 * ICLR_LORA ** -0.5),
        "g2_gate": nrm(ks[9], (DEPTH, GATE_LORA, RWKV_WIDTH), GATE_LORA ** -0.5),
        "k_k": 0.85 + nrm(ks[10], (DEPTH, RWKV_WIDTH), 0.05),
        "k_a": 1.0 + nrm(ks[11], (DEPTH, RWKV_WIDTH), 0.05),
        "r_k": nrm(ks[12], (DEPTH, RWKV_HEADS, RWKV_HEAD_DIM), 0.1),
        "ln_x_w": 1.0 + nrm(ks[13], (DEPTH, RWKV_WIDTH), 0.02),
        "ln_x_b": nrm(ks[14], (DEPTH, RWKV_WIDTH), 0.01),
        "w_o_rwkv": nrm(ks[15], (DEPTH, RWKV_WIDTH, D_MODEL), RWKV_WIDTH ** -0.5),
        "w_o_attn": nrm(ks[16], (DEPTH, ATTN_OUT_WIDTH, D_MODEL), ATTN_OUT_WIDTH ** -0.5),
        "w_out": nrm(ks[17], (DEPTH, D_MODEL, D_MODEL), D_MODEL ** -0.5),
        "norm_ffn": 1.0 + nrm(ks[18], (DEPTH, D_MODEL), 0.02),
        "w_router": nrm(ks[19], (DEPTH, D_MODEL, N_EXPERTS), D_MODEL ** -0.5),
        "b_router": nrm(ks[20], (DEPTH, N_EXPERTS), 0.01),
        "w1": nrm(ks[21], (DEPTH, N_EXPERTS, D_MODEL, 2 * EXPERT_FF), D_MODEL ** -0.5),
        "b1": nrm(ks[22], (DEPTH, N_EXPERTS, 2 * EXPERT_FF), 0.01),
        "w2": nrm(ks[23], (DEPTH, N_EXPERTS, EXPERT_FF, D_MODEL), EXPERT_FF ** -0.5),
        "b2": nrm(ks[24], (DEPTH, N_EXPERTS, D_MODEL), 0.01),
        "norm_final": 1.0 + nrm(ks[25], (D_MODEL,), 0.02),
    }


def reference(x, positions, norm_mix, w_in, shift_mu, w0, w2_decay, a0, a2_iclr, g2_gate,
              k_k, k_a, r_k, ln_x_w, ln_x_b, w_o_rwkv, w_o_attn, w_out, norm_ffn,
              w_router, b_router, w1, b1, w2, b2, norm_final):
    for layer in range(DEPTH):
        h = rms_norm(x, norm_mix[layer])
        mix = hybrid_mixer(h, positions, w_in[layer], shift_mu[layer], w0[layer], w2_decay[layer],
                           a0[layer], a2_iclr[layer], g2_gate[layer], k_k[layer], k_a[layer],
                           r_k[layer], ln_x_w[layer], ln_x_b[layer], w_o_rwkv[layer],
                           w_o_attn[layer], w_out[layer])
        x = x + mix.astype(x.dtype)
        h = rms_norm(x, norm_ffn[layer])
        x = x + moe_ffn(h, w_router[layer], b_router[layer], w1[layer], b1[layer],
                        w2[layer], b2[layer]).astype(x.dtype)
    return rms_norm(x, norm_final)
```

```python
import functools

import jax
import jax.numpy as jnp
from jax import lax
from jax.experimental import pallas as pl
from jax.experimental.pallas import tpu as pltpu

f32 = jnp.float32
bf16 = jnp.bfloat16
HIGHEST = lax.Precision.HIGHEST

NORM_EPS = 1e-5
RWKV_GN_EPS = 64e-5
ROPE_THETA = 500000.0
SWIGLU_LIMIT = 7.0
SWIGLU_ALPHA = 1.702

RWKV_HEADS = 16
HEAD_N = 64
RWKV_WIDTH = RWKV_HEADS * HEAD_N
DECAY_LORA = 64
ICLR_LORA = 64
GATE_LORA = 160
ATTN_GROUPS = ((128, 1), (512, 4), (2048, 16))
HEADS_PER_GROUP = 4
ATTN_E = 128
ATTN_GW = HEADS_PER_GROUP * ATTN_E
ATTN_WIDTH = len(ATTN_GROUPS) * ATTN_GW
ROPE_DIM = ATTN_E // 4
TOP_K = 4

LORA_PAD = 512
ZW_OFF, ZA_OFF, ZG_OFF = 3 * RWKV_WIDTH, 3 * RWKV_WIDTH + 128, 3 * RWKV_WIDTH + 256
SHIFT_PAD = 3 * RWKV_WIDTH + LORA_PAD
ATTN_OFF = SHIFT_PAD
GATE_OFF = ATTN_OFF + 3 * ATTN_WIDTH

RWKV_CHUNK = 64
_RWKV_NWIDE = 13
ATTN_BLK = 128
VMEM_LIMIT = 56 * 1024 * 1024


def _dot(a, b, dims=(((1,), (0,)), ((), ())), precision=None):
    return lax.dot_general(a, b, dims, precision=precision, preferred_element_type=f32)


_NT = (((1,), (1,)), ((), ()))
_TN = (((0,), (0,)), ((), ()))


def _proj_kernel(x_ref, g_ref, w_ref, o_ref, h_ref):
    @pl.when(pl.program_id(1) == 0)
    def _():
        x = x_ref[...]
        ms = jnp.mean(x * x, axis=-1, keepdims=True)
        h_ref[...] = (x * lax.rsqrt(ms + NORM_EPS) * g_ref[...]).astype(bf16)

    o_ref[...] = _dot(h_ref[...], w_ref[...])


def _proj(x2, g, wp, tm=512, tn=1024):
    T, D = x2.shape
    NP = wp.shape[1]
    return pl.pallas_call(
        _proj_kernel,
        out_shape=jax.ShapeDtypeStruct((T, NP), f32),
        grid=(T // tm, NP // tn),
        in_specs=[pl.BlockSpec((tm, D), lambda i, j: (i, 0)),
                  pl.BlockSpec((1, D), lambda i, j: (0, 0)),
                  pl.BlockSpec((D, tn), lambda i, j: (0, j))],
        out_specs=pl.BlockSpec((tm, tn), lambda i, j: (i, j)),
        scratch_shapes=[pltpu.VMEM((tm, D), bf16)],
        compiler_params=pltpu.CompilerParams(
            dimension_semantics=("arbitrary", "arbitrary"), vmem_limit_bytes=VMEM_LIMIT),
        name="proj",
    )(x2, g, wp)


def _rwkv_kernel(p_ref, mu_ref, w0_ref, w2_ref, a0_ref, a2_ref, g2_ref, kk_ref, ka_ref,
                 rk_ref, lnw_ref, lnb_ref, o_ref, state_ref, prev_ref, dec_ref, wide_ref):
    (at0_ref, at_ref, rt0_ref, rt_ref, bt_ref, kt_ref, bh_ref, kh_ref, v_ref, rkb_ref, g_ref,
     kkr_ref, y_ref) = [wide_ref.at[i] for i in range(_RWKV_NWIDE)]
    L = RWKV_CHUNK
    W = RWKV_WIDTH
    c_idx = pl.program_id(1)

    @pl.when(c_idx == 0)
    def _():
        state_ref[...] = jnp.zeros_like(state_ref)
        prev_ref[...] = jnp.zeros_like(prev_ref)

    p = p_ref[0]
    row = lax.broadcasted_iota(jnp.int32, (L, 1), 0)
    shifted = jnp.where(row == 0, prev_ref[...], pltpu.roll(p, 1, axis=0))
    prev_ref[...] = p[L - 1:L, :]
    z = p + (shifted - p) * mu_ref[...]

    r = z[:, 0:W]
    k = z[:, W:2 * W]
    v = z[:, 2 * W:3 * W]
    zw = z[:, ZW_OFF:ZW_OFF + 128]
    za = z[:, ZA_OFF:ZA_OFF + 128]
    zg = z[:, ZG_OFF:ZG_OFF + 256]

    w_raw = w0_ref[...] + _dot(jnp.tanh(zw), w2_ref[...], precision=HIGHEST)
    sp = jnp.maximum(-w_raw, 0.0) + jnp.log(1.0 + jnp.exp(-jnp.abs(w_raw)))
    lw = -jnp.exp(-sp - 0.5)
    a = jax.nn.sigmoid(a0_ref[...] + _dot(za, a2_ref[...], precision=HIGHEST))
    g_ref[...] = _dot(jax.nn.sigmoid(zg), g2_ref[...])

    ti = lax.broadcasted_iota(jnp.int32, (L, L), 0)
    si = lax.broadcasted_iota(jnp.int32, (L, L), 1)
    tril_incl = ti >= si
    tril_strict = ti > si
    c = _dot(tril_incl.astype(f32), lw, precision=HIGHEST)
    cex = c - lw
    cm = c[L // 2 - 1:L // 2, :]
    cl = c[L - 1:L, :]
    e_m = jnp.exp(-cm)
    e_c = jnp.exp(c)
    e_ex = jnp.exp(cex)
    e_inv = jnp.exp(cm - c)
    e_tail = jnp.exp(cl - c)

    kkv = k * kk_ref[...]
    kmod = k * (1.0 + (a - 1.0) * ka_ref[...])
    at0_ref[...] = e_ex
    rt0 = r * e_c
    rt0_ref[...] = rt0
    rt_ref[...] = rt0 * e_m
    kt_ref[...] = kmod * e_inv
    kh_ref[...] = kmod * e_tail
    bt_ref[...] = a * e_inv
    bh_ref[...] = a * e_tail
    at_ref[...] = e_ex * e_m
    v_ref[...] = v
    dec_ref[...] = jnp.exp(cl)
    rkb_ref[...] = r * kmod * rk_ref[...]
    kkr_ref[...] = kkv

    eye = (ti == si).astype(f32)
    for h in range(RWKV_HEADS):
        sl = slice(h * HEAD_N, (h + 1) * HEAD_N)
        kk_h = kkr_ref[:, sl]
        nrm = jnp.sqrt(jnp.sum(kk_h * kk_h, axis=-1, keepdims=True))
        kk_h = kk_h / jnp.maximum(nrm, 1e-12)
        at0 = -kk_h * at0_ref[:, sl]
        at = -kk_h * at_ref[:, sl]
        bt = kk_h * bt_ref[:, sl]
        bh = kk_h * bh_ref[:, sl]
        rt0_h = rt0_ref[:, sl]
        rt = rt_ref[:, sl]
        kt = kt_ref[:, sl]
        kh = kh_ref[:, sl]
        v_h = v_ref[:, sl]

        a_ab = jnp.where(tril_strict, _dot(at, bt, _NT), 0.0)
        a_ak = jnp.where(tril_strict, _dot(at, kt, _NT), 0.0)
        a_rb = jnp.where(tril_incl, _dot(rt, bt, _NT), 0.0)
        a_rk = jnp.where(tril_incl, _dot(rt, kt, _NT), 0.0)

        tinv = eye + a_ab
        xp = a_ab
        n = 2
        while n < L:
            xp = _dot(xp, xp, precision=HIGHEST)
            tinv = tinv + _dot(tinv, xp, precision=HIGHEST)
            n *= 2

        w1 = _dot(tinv, at0, precision=HIGHEST)
        w2 = _dot(tinv, _dot(a_ak, v_h), precision=HIGHEST)

        s0 = state_ref[h]
        u = _dot(w1, s0, _NT) + w2
        y = _dot(rt0_h, s0, _NT) + _dot(a_rb, u) + _dot(a_rk, v_h)
        state_ref[h] = s0 * dec_ref[:, sl] + _dot(u, bh, _TN) + _dot(v_h, kh, _TN)

        mean = jnp.mean(y, axis=-1, keepdims=True)
        yc = y - mean
        var = jnp.mean(yc * yc, axis=-1, keepdims=True)
        yn = yc * lax.rsqrt(var + RWKV_GN_EPS) * lnw_ref[:, sl] + lnb_ref[:, sl]
        bonus = jnp.sum(rkb_ref[:, sl], axis=-1, keepdims=True) * v_h
        y_ref[:, sl] = (yn + bonus) * g_ref[:, sl]
    o_ref[0] = y_ref[...].astype(o_ref.dtype)


def _rwkv(proj3, mu, w0, w2d, a0, a2, g2, k_k, k_a, r_k, ln_w, ln_b):
    B, S, NP = proj3.shape
    L, W = RWKV_CHUNK, RWKV_WIDTH
    row = lambda n: pl.BlockSpec((1, n), lambda b, c: (0, 0))
    full = lambda a: pl.BlockSpec(a.shape, lambda b, c: (0, 0))
    return pl.pallas_call(
        _rwkv_kernel,
        out_shape=jax.ShapeDtypeStruct((B, S, W), bf16),
        grid=(B, S // L),
        in_specs=[pl.BlockSpec((1, L, SHIFT_PAD), lambda b, c: (b, c, 0)),
                  row(SHIFT_PAD), row(W), full(w2d), row(W), full(a2), full(g2),
                  row(W), row(W), row(W), row(W), row(W)],
        out_specs=pl.BlockSpec((1, L, W), lambda b, c: (b, c, 0)),
        scratch_shapes=[pltpu.VMEM((RWKV_HEADS, HEAD_N, HEAD_N), f32),
                        pltpu.VMEM((1, SHIFT_PAD), f32), pltpu.VMEM((1, W), f32),
                        pltpu.VMEM((_RWKV_NWIDE, L, W), f32)],
        compiler_params=pltpu.CompilerParams(
            dimension_semantics=("arbitrary", "arbitrary"), vmem_limit_bytes=VMEM_LIMIT),
        name="rwkv",
    )(proj3, mu, w0, w2d, a0, a2, g2, k_k, k_a, r_k, ln_w, ln_b)


def _rope(x, cosf, sinf, lane):
    rot = jnp.where(lane < ROPE_DIM // 2, pltpu.roll(x, ATTN_E - ROPE_DIM // 2, axis=1),
                    pltpu.roll(x, ROPE_DIM // 2, axis=1))
    return x * cosf + rot * sinf


def _attn_kernel(q_ref, kc_ref, kp_ref, vc_ref, vp_ref, cc_ref, sc_ref, cp_ref, sp_ref,
                 o_ref, l_ref):
    n = pl.program_id(2)
    blk = ATTN_BLK
    qi = lax.broadcasted_iota(jnp.int32, (blk, blk), 0)
    kj = lax.broadcasted_iota(jnp.int32, (blk, blk), 1)
    lane = lax.broadcasted_iota(jnp.int32, (blk, ATTN_E), 1)
    cur_ok = kj <= qi
    prev_ok = jnp.logical_and(kj >= qi, n > 0)
    cc, sc, cp, sp = cc_ref[0], sc_ref[0], cp_ref[0], sp_ref[0]
    scale = ATTN_E ** -0.5
    for hh in range(HEADS_PER_GROUP):
        sl = slice(hh * ATTN_E, (hh + 1) * ATTN_E)
        q = _rope(q_ref[0, :, sl], cc, sc, lane).astype(bf16)
        kc = _rope(kc_ref[0, :, sl], cc, sc, lane).astype(bf16)
        kp = _rope(kp_ref[0, :, sl], cp, sp, lane).astype(bf16)
        s_c = jnp.where(cur_ok, _dot(q, kc, _NT) * scale, -jnp.inf)
        s_p = jnp.where(prev_ok, _dot(q, kp, _NT) * scale, -jnp.inf)
        m = jnp.maximum(jnp.max(s_c, axis=-1, keepdims=True), jnp.max(s_p, axis=-1, keepdims=True))
        p_c = jnp.exp(s_c - m)
        p_p = jnp.exp(s_p - m)
        l = jnp.sum(p_c, axis=-1, keepdims=True) + jnp.sum(p_p, axis=-1, keepdims=True)
        acc = (_dot(p_c.astype(bf16), vc_ref[0, :, sl].astype(bf16))
               + _dot(p_p.astype(bf16), vp_ref[0, :, sl].astype(bf16)))
        o_ref[0, :, sl] = acc / l
        l_ref[0, :, sl] = jnp.broadcast_to(m + jnp.log(l), (blk, ATTN_E))


def _attn_group(proj3, cosf, sinf, gi, dil):
    B, S, NP = proj3.shape
    Lr = S // dil
    nb = Lr // ATTN_BLK
    pv = proj3.reshape(B, Lr, dil * NP)
    cv = cosf.reshape(B, Lr, dil * ATTN_E)
    sv = sinf.reshape(B, Lr, dil * ATTN_E)
    nblk = NP // ATTN_GW
    qo = ATTN_OFF // ATTN_GW + gi
    ko = qo + ATTN_WIDTH // ATTN_GW
    vo = ko + ATTN_WIDTH // ATTN_GW

    def cur(off):
        return pl.BlockSpec((1, ATTN_BLK, ATTN_GW), lambda b, r, n: (b, n, r * nblk + off))

    def prev(off):
        return pl.BlockSpec((1, ATTN_BLK, ATTN_GW),
                            lambda b, r, n: (b, jnp.maximum(n - 1, 0), r * nblk + off))

    tc = pl.BlockSpec((1, ATTN_BLK, ATTN_E), lambda b, r, n: (b, n, r))
    tp = pl.BlockSpec((1, ATTN_BLK, ATTN_E), lambda b, r, n: (b, jnp.maximum(n - 1, 0), r))
    out = pl.BlockSpec((1, ATTN_BLK, ATTN_GW), lambda b, r, n: (b, n, r))
    o, l = pl.pallas_call(
        _attn_kernel,
        out_shape=[jax.ShapeDtypeStruct((B, Lr, dil * ATTN_GW), f32)] * 2,
        grid=(B, dil, nb),
        in_specs=[cur(qo), cur(ko), prev(ko), cur(vo), prev(vo), tc, tc, tp, tp],
        out_specs=[out, out],
        compiler_params=pltpu.CompilerParams(
            dimension_semantics=("arbitrary", "arbitrary", "arbitrary"),
            vmem_limit_bytes=VMEM_LIMIT),
        name=f"attn_g{gi}",
    )(pv, pv, pv, pv, pv, cv, sv, cv, sv)
    return o.reshape(B * S, ATTN_GW), l.reshape(B * S, ATTN_GW)


def _post_kernel(x_ref, rw_ref, o0_ref, o1_ref, o2_ref, l0_ref, l1_ref, l2_ref, ga_ref, gb_ref,
                 wor_ref, woa_ref, wout_ref, gn_ref, wr_ref, br_ref,
                 x1_ref, h2_ref, lg_ref):
    l0, l1, l2 = l0_ref[...], l1_ref[...], l2_ref[...]
    m = jnp.maximum(jnp.maximum(l0, l1), l2)
    e0, e1, e2 = jnp.exp(l0 - m), jnp.exp(l1 - m), jnp.exp(l2 - m)
    o = (e0 * o0_ref[...] + e1 * o1_ref[...] + e2 * o2_ref[...]) / (e0 + e1 + e2)
    y_b = _dot(o.astype(bf16), woa_ref[...])
    y_a = _dot(rw_ref[...], wor_ref[...])
    merged = jax.nn.sigmoid(ga_ref[...]) * y_a + jax.nn.sigmoid(gb_ref[...]) * y_b
    x1 = x_ref[...] + _dot(merged.astype(bf16), wout_ref[...])
    x1_ref[...] = x1
    ms = jnp.mean(x1 * x1, axis=-1, keepdims=True)
    h2 = x1 * lax.rsqrt(ms + NORM_EPS) * gn_ref[...]
    h2_ref[...] = h2.astype(bf16)
    lg_ref[...] = _dot(h2, wr_ref[...], precision=HIGHEST) + br_ref[...]


def _post(x2, rw, outs, lses, proj, wor, woa, wout, gn, wr, br, tm=256):
    T, D = x2.shape
    NP = proj.shape[1]
    row = lambda w: pl.BlockSpec((tm, w), lambda i: (i, 0))
    const = lambda a: pl.BlockSpec(a.shape, lambda i: (0, 0), pipeline_mode=pl.Buffered(1))
    ga_blk = GATE_OFF // D
    return pl.pallas_call(
        _post_kernel,
        out_shape=[jax.ShapeDtypeStruct((T, D), f32), jax.ShapeDtypeStruct((T, D), bf16),
                   jax.ShapeDtypeStruct((T, 128), f32)],
        grid=(T // tm,),
        in_specs=[row(D), row(RWKV_WIDTH)] + [row(ATTN_GW)] * 6
                 + [pl.BlockSpec((tm, D), lambda i: (i, ga_blk)),
                    pl.BlockSpec((tm, D), lambda i: (i, ga_blk + 1)),
                    const(wor), const(woa), const(wout), const(gn), const(wr), const(br)],
        out_specs=[row(D), row(D), row(128)],
        compiler_params=pltpu.CompilerParams(
            dimension_semantics=("arbitrary",), vmem_limit_bytes=VMEM_LIMIT),
        name="post",
    )(x2, rw, *outs, *lses, proj, proj, wor, woa, wout, gn, wr, br)


def _moe_kernel(te_ref, nv_ref, xs_ref, gate_ref, w1g_ref, w1l_ref, b1g_ref, b1l_ref,
                w2_ref, b2_ref, o_ref, acc_ref):
    t = pl.program_id(0)
    j = pl.program_id(1)
    nj = pl.num_programs(1)

    @pl.when(t < nv_ref[0])
    def _():
        @pl.when(j == 0)
        def _():
            acc_ref[...] = jnp.zeros_like(acc_ref)

        xs = xs_ref[...]
        hg = _dot(xs, w1g_ref[...].astype(bf16)) + b1g_ref[...]
        hl = _dot(xs, w1l_ref[...].astype(bf16)) + b1l_ref[...]
        x_glu = jnp.minimum(hg, SWIGLU_LIMIT)
        x_lin = jnp.clip(hl, -SWIGLU_LIMIT, SWIGLU_LIMIT)
        act = x_glu * jax.nn.sigmoid(SWIGLU_ALPHA * x_glu) * (x_lin + 1.0)
        acc_ref[...] += _dot(act.astype(bf16), w2_ref[...].astype(bf16))

        @pl.when(j == nj - 1)
        def _():
            o_ref[...] = (acc_ref[...] + b2_ref[...]) * gate_ref[...]


def _moe(tile_e, n_valid, xs, gate, w1, b1, w2, b2, tm, tf=512):
    P, D = xs.shape
    E, _, F2 = w1.shape
    F = F2 // 2
    nj = F // tf
    nt = P // tm

    def tcl(t, nv):
        return jnp.minimum(t, nv[0] - 1)

    def jcl(t, j, nv):
        return jnp.where(t < nv[0], j, nj - 1)

    grid_spec = pltpu.PrefetchScalarGridSpec(
        num_scalar_prefetch=2,
        grid=(nt, nj),
        in_specs=[
            pl.BlockSpec((tm, D), lambda t, j, te, nv: (tcl(t, nv), 0)),
            pl.BlockSpec((tm, 1), lambda t, j, te, nv: (tcl(t, nv), 0)),
            pl.BlockSpec((None, D, tf), lambda t, j, te, nv: (te[tcl(t, nv)], 0, jcl(t, j, nv))),
            pl.BlockSpec((None, D, tf), lambda t, j, te, nv: (te[tcl(t, nv)], 0, nj + jcl(t, j, nv))),
            pl.BlockSpec((None, 1, tf), lambda t, j, te, nv: (te[tcl(t, nv)], 0, jcl(t, j, nv))),
            pl.BlockSpec((None, 1, tf), lambda t, j, te, nv: (te[tcl(t, nv)], 0, nj + jcl(t, j, nv))),
            pl.BlockSpec((None, tf, D), lambda t, j, te, nv: (te[tcl(t, nv)], jcl(t, j, nv), 0)),
            pl.BlockSpec((None, 1, D), lambda t, j, te, nv: (te[tcl(t, nv)], 0, 0)),
        ],
        out_specs=pl.BlockSpec((tm, D), lambda t, j, te, nv: (tcl(t, nv), 0)),
        scratch_shapes=[pltpu.VMEM((tm, D), f32)],
    )
    return pl.pallas_call(
        _moe_kernel,
        out_shape=jax.ShapeDtypeStruct((P, D), f32),
        grid_spec=grid_spec,
        compiler_params=pltpu.CompilerParams(
            dimension_semantics=("arbitrary", "arbitrary"), vmem_limit_bytes=VMEM_LIMIT),
        name="moe",
    )(tile_e, n_valid, xs, gate, w1, w1, b1, b1, w2, b2)


def _final_kernel(x1_ref, y_ref, g_ref, o_ref, *, apply_norm):
    D = x1_ref.shape[1]
    y = y_ref[:, 0:D]
    for kk in range(1, TOP_K):
        y = y + y_ref[:, kk * D:(kk + 1) * D]
    x2 = x1_ref[...] + y
    if apply_norm:
        ms = jnp.mean(x2 * x2, axis=-1, keepdims=True)
        x2 = x2 * lax.rsqrt(ms + NORM_EPS) * g_ref[...]
    o_ref[...] = x2


def _final(x1, y4, g, apply_norm, tm=256):
    T, D = x1.shape
    return pl.pallas_call(
        functools.partial(_final_kernel, apply_norm=apply_norm),
        out_shape=jax.ShapeDtypeStruct((T, D), f32),
        grid=(T // tm,),
        in_specs=[pl.BlockSpec((tm, D), lambda i: (i, 0)),
                  pl.BlockSpec((tm, TOP_K * D), lambda i: (i, 0)),
                  pl.BlockSpec((1, D), lambda i: (0, 0))],
        out_specs=pl.BlockSpec((tm, D), lambda i: (i, 0)),
        compiler_params=pltpu.CompilerParams(
            dimension_semantics=("arbitrary",), vmem_limit_bytes=VMEM_LIMIT),
        name="final",
    )(x1, y4, g)


def _pad_cols(a, n):
    return jnp.pad(a, ((0, 0), (0, n - a.shape[1])))


def _pad_rows(a, n):
    return jnp.pad(a, ((0, n - a.shape[0]), (0, 0)))


def _regroup_cols(a):
    W = RWKV_WIDTH
    o = 3 * W
    return jnp.concatenate([
        a[:, :o],
        _pad_cols(a[:, o:o + DECAY_LORA], 128),
        _pad_cols(a[:, o + DECAY_LORA:o + DECAY_LORA + ICLR_LORA], 128),
        _pad_cols(a[:, o + DECAY_LORA + ICLR_LORA:o + DECAY_LORA + ICLR_LORA + GATE_LORA], 256),
        a[:, o + DECAY_LORA + ICLR_LORA + GATE_LORA:],
    ], axis=1)


def _rope_tables(positions):
    half = ROPE_DIM // 2
    inv_freq = ROPE_THETA ** (-jnp.arange(half, dtype=f32) / half)
    ang = positions.astype(f32)[..., None] * inv_freq
    cos, sin = jnp.cos(ang), jnp.sin(ang)
    B, S = positions.shape
    cosf = jnp.concatenate([cos, cos, jnp.ones((B, S, ATTN_E - ROPE_DIM), f32)], axis=-1)
    sinf = jnp.concatenate([-sin, sin, jnp.zeros((B, S, ATTN_E - ROPE_DIM), f32)], axis=-1)
    return cosf, sinf


def _route(logits, tm, n_tiles):
    T, E = logits.shape
    top_val, top_idx = lax.top_k(logits, TOP_K)
    gates = jax.nn.softmax(top_val, axis=-1)
    flat_e = top_idx.reshape(-1)
    order = jnp.argsort(flat_e)
    e_sorted = flat_e[order]
    sizes = jnp.bincount(flat_e, length=E).astype(jnp.int32)
    padded = ((sizes + tm - 1) // tm) * tm
    pad_end = jnp.cumsum(padded)
    pad_start = pad_end - padded
    start = jnp.cumsum(sizes) - sizes
    dest = pad_start[e_sorted] + (jnp.arange(T * TOP_K, dtype=jnp.int32) - start[e_sorted])
    P = n_tiles * tm
    tok = jnp.zeros((P,), jnp.int32).at[dest].set((order // TOP_K).astype(jnp.int32))
    gate = jnp.zeros((P,), f32).at[dest].set(gates.reshape(-1)[order])
    pos = jnp.zeros((T * TOP_K,), jnp.int32).at[order].set(dest)
    tile_start = jnp.arange(n_tiles, dtype=jnp.int32) * tm
    tile_e = jnp.minimum(jnp.searchsorted(pad_end, tile_start, side="right"), E - 1).astype(jnp.int32)
    n_valid = (pad_end[-1] // tm).astype(jnp.int32).reshape(1)
    return tok, gate.reshape(P, 1), pos, tile_e, n_valid


def kernel(x, positions, norm_mix, w_in, shift_mu, w0, w2_decay, a0, a2_iclr, g2_gate, k_k, k_a,
           r_k, ln_x_w, ln_x_b, w_o_rwkv, w_o_attn, w_out, norm_ffn, w_router, b_router,
           w1, b1, w2, b2, norm_final):
    B, S, D = x.shape
    T = B * S
    E = w_router.shape[-1]
    depth = norm_mix.shape[0]
    xt = x.reshape(T, D)
    cosf, sinf = _rope_tables(positions)
    moe_tm = 512
    n_tiles = (T * TOP_K) // moe_tm + E
    for layer in range(depth):
        wp = _regroup_cols(w_in[layer]).astype(bf16)
        mu = _regroup_cols(shift_mu[layer][None, :])
        proj = _proj(xt, norm_mix[layer][None, :], wp)
        proj3 = proj.reshape(B, S, -1)
        rw = _rwkv(proj3, mu, w0[layer][None, :], _pad_rows(w2_decay[layer], 128),
                   a0[layer][None, :], _pad_rows(a2_iclr[layer], 128),
                   _pad_rows(g2_gate[layer], 256), k_k[layer][None, :], k_a[layer][None, :],
                   r_k[layer].reshape(1, -1), ln_x_w[layer][None, :], ln_x_b[layer][None, :])
        outs, lses = [], []
        for gi, (window, dil) in enumerate(ATTN_GROUPS):
            o_g, l_g = _attn_group(proj3, cosf, sinf, gi, dil)
            outs.append(o_g)
            lses.append(l_g)
        x1, h2, logits = _post(
            xt, rw.reshape(T, -1), outs, lses, proj,
            w_o_rwkv[layer].astype(bf16), w_o_attn[layer].astype(bf16), w_out[layer].astype(bf16),
            norm_ffn[layer][None, :], _pad_cols(w_router[layer], 128),
            _pad_cols(b_router[layer][None, :], 128))
        tok, gate, pos, tile_e, n_valid = _route(logits[:, :E], moe_tm, n_tiles)
        xs = h2[tok]
        ys = _moe(tile_e, n_valid, xs, gate, w1[layer], b1[layer][:, None, :], w2[layer],
                  b2[layer][:, None, :], moe_tm)
        y4 = ys[pos].reshape(T, TOP_K * D)
        xt = _final(x1, y4, norm_final[None, :], apply_norm=layer + 1 == depth)
    return xt.reshape(B, S, D)
```

```python
import functools

import jax
import jax.numpy as jnp
from jax import lax
from jax.experimental import pallas as pl
from jax.experimental.pallas import tpu as pltpu

f32 = jnp.float32
bf16 = jnp.bfloat16
HIGHEST = lax.Precision.HIGHEST

NORM_EPS = 1e-5
RWKV_GN_EPS = 64e-5
ROPE_THETA = 500000.0
SWIGLU_LIMIT = 7.0
SWIGLU_ALPHA = 1.702

RWKV_HEADS = 16
HEAD_N = 64
RWKV_WIDTH = RWKV_HEADS * HEAD_N
DECAY_LORA = 64
ICLR_LORA = 64
GATE_LORA = 160
ATTN_GROUPS = ((128, 1), (512, 4), (2048, 16))
HEADS_PER_GROUP = 4
ATTN_E = 128
ATTN_GW = HEADS_PER_GROUP * ATTN_E
ATTN_WIDTH = len(ATTN_GROUPS) * ATTN_GW
ROPE_DIM = ATTN_E // 4
TOP_K = 4

LORA_PAD = 512
ZW_OFF, ZA_OFF, ZG_OFF = 3 * RWKV_WIDTH, 3 * RWKV_WIDTH + 128, 3 * RWKV_WIDTH + 256
SHIFT_PAD = 3 * RWKV_WIDTH + LORA_PAD
ATTN_OFF = SHIFT_PAD
GATE_OFF = ATTN_OFF + 3 * ATTN_WIDTH

RWKV_CHUNK = 64
_RWKV_NWIDE = 12
ATTN_BLK = 128
VMEM_LIMIT = 56 * 1024 * 1024


def _dot(a, b, dims=(((1,), (0,)), ((), ())), precision=None):
    return lax.dot_general(a, b, dims, precision=precision, preferred_element_type=f32)


_NT = (((1,), (1,)), ((), ()))
_TN = (((0,), (0,)), ((), ()))


def _split2(a):
    hi = a.astype(bf16)
    return hi, (a - hi.astype(f32)).astype(bf16)


def _split3(a):
    hi = a.astype(bf16)
    rem = a - hi.astype(f32)
    mid = rem.astype(bf16)
    return hi, mid, (rem - mid.astype(f32)).astype(bf16)


def _stack_lhs(a):
    hi, lo = _split2(a)
    return jnp.concatenate([hi, lo, hi], axis=1)


def _stack_rhs(w):
    hi, lo = _split2(w)
    return jnp.concatenate([hi, hi, lo], axis=0)


def _proj_kernel(x_ref, g_ref, w_ref, o_ref, h_ref):
    @pl.when(pl.program_id(1) == 0)
    def _():
        x = x_ref[...]
        ms = jnp.mean(x * x, axis=-1, keepdims=True)
        h_ref[...] = (x * lax.rsqrt(ms + NORM_EPS) * g_ref[...]).astype(bf16)

    o_ref[...] = _dot(h_ref[...], w_ref[...])


def _proj(x2, g, wp, tm=512, tn=1024):
    T, D = x2.shape
    NP = wp.shape[1]
    return pl.pallas_call(
        _proj_kernel,
        out_shape=jax.ShapeDtypeStruct((T, NP), f32),
        grid=(T // tm, NP // tn),
        in_specs=[pl.BlockSpec((tm, D), lambda i, j: (i, 0)),
                  pl.BlockSpec((1, D), lambda i, j: (0, 0)),
                  pl.BlockSpec((D, tn), lambda i, j: (0, j))],
        out_specs=pl.BlockSpec((tm, tn), lambda i, j: (i, j)),
        scratch_shapes=[pltpu.VMEM((tm, D), bf16)],
        compiler_params=pltpu.CompilerParams(
            dimension_semantics=("arbitrary", "arbitrary"), vmem_limit_bytes=VMEM_LIMIT),
        name="proj",
    )(x2, g, wp)


def _rwkv_kernel(p_ref, mu_ref, w0_ref, w2_ref, a0_ref, a2_ref, g2_ref, kk_ref, ka_ref,
                 rk_ref, lnw_ref, lnb_ref, o_ref, state_ref, prev_ref, dec_ref, wide_ref):
    (at0_ref, at_ref, rt0_ref, rt_ref, bt_ref, kt_ref, bh_ref, kh_ref, v_ref, rkb_ref, g_ref,
     kkr_ref) = [wide_ref.at[i] for i in range(_RWKV_NWIDE)]
    L = RWKV_CHUNK
    W = RWKV_WIDTH
    c_idx = pl.program_id(1)

    @pl.when(c_idx == 0)
    def _():
        state_ref[...] = jnp.zeros_like(state_ref)
        prev_ref[...] = jnp.zeros_like(prev_ref)

    p = p_ref[0]
    row = lax.broadcasted_iota(jnp.int32, (L, 1), 0)
    shifted = jnp.where(row == 0, prev_ref[...], pltpu.roll(p, 1, axis=0))
    prev_ref[...] = p[L - 1:L, :]
    z = p + (shifted - p) * mu_ref[...]

    r = z[:, 0:W]
    k = z[:, W:2 * W]
    v = z[:, 2 * W:3 * W]
    zw = z[:, ZW_OFF:ZW_OFF + 128]
    za = z[:, ZA_OFF:ZA_OFF + 128]
    zg = z[:, ZG_OFF:ZG_OFF + 256]

    w_raw = w0_ref[...] + _dot(_stack_lhs(jnp.tanh(zw)), w2_ref[...])
    sp = jnp.maximum(-w_raw, 0.0) + jnp.log(1.0 + jnp.exp(-jnp.abs(w_raw)))
    lw = -jnp.exp(-sp - 0.5)
    a = jax.nn.sigmoid(a0_ref[...] + _dot(_stack_lhs(za), a2_ref[...]))
    g_ref[...] = _dot(jax.nn.sigmoid(zg).astype(bf16), g2_ref[...])

    ti = lax.broadcasted_iota(jnp.int32, (L, L), 0)
    si = lax.broadcasted_iota(jnp.int32, (L, L), 1)
    tril_incl = ti >= si
    tril_strict = ti > si
    tri = tril_incl.astype(bf16)
    lw_h, lw_m, lw_l = _split3(lw)
    c = _dot(jnp.concatenate([tri, tri, tri], axis=1), jnp.concatenate([lw_h, lw_m, lw_l], axis=0))
    cex = c - lw
    cm = c[L // 2 - 1:L // 2, :]
    cl = c[L - 1:L, :]
    e_m = jnp.exp(-cm)
    e_c = jnp.exp(c)
    e_ex = jnp.exp(cex)
    e_inv = jnp.exp(cm - c)
    e_tail = jnp.exp(cl - c)

    kkv = k * kk_ref[...]
    kmod = k * (1.0 + (a - 1.0) * ka_ref[...])
    at0_ref[...] = e_ex
    rt0 = r * e_c
    rt0_ref[...] = rt0
    rt_ref[...] = rt0 * e_m
    kt_ref[...] = kmod * e_inv
    kh_ref[...] = kmod * e_tail
    bt_ref[...] = a * e_inv
    bh_ref[...] = a * e_tail
    at_ref[...] = e_ex * e_m
    v_ref[...] = v
    dec_ref[...] = jnp.exp(cl)
    rkb_ref[...] = r * kmod * rk_ref[...]
    kkr_ref[...] = kkv

    PW = 2 * HEAD_N
    pairs = range(RWKV_HEADS // 2)
    lane = lax.broadcasted_iota(jnp.int32, (L, PW), 1)
    trow = lax.broadcasted_iota(jnp.int32, (L, PW), 0)
    first = lane < HEAD_N
    scol = jnp.where(first, lane, lane - HEAD_N)
    strict_p = trow > scol
    incl_p = trow >= scol
    eye_p = (trow == scol).astype(f32)
    r2 = lax.broadcasted_iota(jnp.int32, (2 * PW, PW), 0)
    l2 = lax.broadcasted_iota(jnp.int32, (2 * PW, PW), 1)
    seg_ones = (((r2 % PW) < HEAD_N) == (l2 < HEAD_N)).astype(bf16)
    rb = lax.broadcasted_iota(jnp.int32, (PW, PW), 0)
    lb = lax.broadcasted_iota(jnp.int32, (PW, PW), 1)
    bd_mask = (rb < HEAD_N) == (lb < HEAD_N)

    def bd(x):
        zero = jnp.zeros_like(x)
        return jnp.concatenate([jnp.where(first, x, zero), jnp.where(first, zero, x)], axis=0)

    def segsum(x):
        hi, lo = _split2(x)
        return _dot(jnp.concatenate([hi, lo], axis=1), seg_ones)

    ps = [slice(p * PW, (p + 1) * PW) for p in pairs]
    kkr = [kkr_ref[:, s] for s in ps]
    ssq = [segsum(x * x) for x in kkr]
    kkh = [x / jnp.maximum(jnp.sqrt(q), 1e-12) for x, q in zip(kkr, ssq)]
    at0 = [(-kh_ * at0_ref[:, s]).astype(bf16) for kh_, s in zip(kkh, ps)]
    at = [(-kh_ * at_ref[:, s]).astype(bf16) for kh_, s in zip(kkh, ps)]
    bt = [(kh_ * bt_ref[:, s]).astype(bf16) for kh_, s in zip(kkh, ps)]
    bh = [(kh_ * bh_ref[:, s]).astype(bf16) for kh_, s in zip(kkh, ps)]
    vb = [v_ref[:, s].astype(bf16) for s in ps]

    lhs = [jnp.concatenate([a_, rt_ref[:, s].astype(bf16)], axis=0) for a_, s in zip(at, ps)]
    sb = [_dot(l_, bd(b_), _NT) for l_, b_ in zip(lhs, bt)]
    sk = [_dot(l_, bd(kt_ref[:, s].astype(bf16)), _NT) for l_, s in zip(lhs, ps)]
    a_ab = [jnp.where(strict_p, x[:L], 0.0) for x in sb]
    a_rb = [jnp.where(incl_p, x[L:], 0.0).astype(bf16) for x in sb]
    a_ak = [jnp.where(strict_p, x[:L], 0.0).astype(bf16) for x in sk]
    a_rk = [jnp.where(incl_p, x[L:], 0.0).astype(bf16) for x in sk]

    tinv = [eye_p + x for x in a_ab]
    xp = a_ab
    n = 2
    while n < L:
        xb = [x.astype(bf16) for x in xp]
        xp = [_dot(x, bd(x)) for x in xb]
        tinv = [t + _dot(t.astype(bf16), bd(x.astype(bf16))) for t, x in zip(tinv, xp)]
        n *= 2
    tb = [t.astype(bf16) for t in tinv]

    akv = [_dot(a_, bd(v_)).astype(bf16) for a_, v_ in zip(a_ak, vb)]
    w12 = [_dot(t, jnp.concatenate([bd(a_), bd(k_)], axis=1)) for t, a_, k_ in zip(tb, at0, akv)]

    s0 = [state_ref[p] for p in pairs]
    s0b = [s.astype(bf16) for s in s0]
    uy = [_dot(jnp.concatenate([w[:, :PW].astype(bf16), rt0_ref[:, s].astype(bf16)], axis=0), sb_, _NT)
          for w, s, sb_ in zip(w12, ps, s0b)]
    u = [x[:L] + w[:, PW:] for x, w in zip(uy, w12)]
    ub = [x.astype(bf16) for x in u]
    y = [x[L:] + _dot(jnp.concatenate([rb_, rk_], axis=1), jnp.concatenate([bd(u_), bd(v_)], axis=0))
         for x, rb_, rk_, u_, v_ in zip(uy, a_rb, a_rk, ub, vb)]
    for p in pairs:
        upd = _dot(jnp.concatenate([ub[p], vb[p]], axis=0),
                   jnp.concatenate([bh[p], kh_ref[:, ps[p]].astype(bf16)], axis=0), _TN)
        state_ref[p] = s0[p] * dec_ref[:, ps[p]] + jnp.where(bd_mask, upd, 0.0)

    inv_n = 1.0 / HEAD_N
    mean = [segsum(x) * inv_n for x in y]
    yc = [x - m for x, m in zip(y, mean)]
    var = [segsum(x * x) * inv_n for x in yc]
    bonus = [segsum(rkb_ref[:, s]) for s in ps]
    for p in pairs:
        s = ps[p]
        yn = yc[p] * lax.rsqrt(var[p] + RWKV_GN_EPS) * lnw_ref[:, s] + lnb_ref[:, s]
        o_ref[0, :, s] = ((yn + bonus[p] * v_ref[:, s]) * g_ref[:, s]).astype(o_ref.dtype)


def _rwkv(proj3, mu, w0, w2d, a0, a2, g2, k_k, k_a, r_k, ln_w, ln_b):
    B, S, NP = proj3.shape
    L, W = RWKV_CHUNK, RWKV_WIDTH
    row = lambda n: pl.BlockSpec((1, n), lambda b, c: (0, 0))
    full = lambda a: pl.BlockSpec(a.shape, lambda b, c: (0, 0))
    return pl.pallas_call(
        _rwkv_kernel,
        out_shape=jax.ShapeDtypeStruct((B, S, W), bf16),
        grid=(B, S // L),
        in_specs=[pl.BlockSpec((1, L, SHIFT_PAD), lambda b, c: (b, c, 0)),
                  row(SHIFT_PAD), row(W), full(w2d), row(W), full(a2), full(g2),
                  row(W), row(W), row(W), row(W), row(W)],
        out_specs=pl.BlockSpec((1, L, W), lambda b, c: (b, c, 0)),
        scratch_shapes=[pltpu.VMEM((RWKV_HEADS // 2, 2 * HEAD_N, 2 * HEAD_N), f32),
                        pltpu.VMEM((1, SHIFT_PAD), f32), pltpu.VMEM((1, W), f32),
                        pltpu.VMEM((_RWKV_NWIDE, L, W), f32)],
        compiler_params=pltpu.CompilerParams(
            dimension_semantics=("arbitrary", "arbitrary"), vmem_limit_bytes=VMEM_LIMIT),
        name="rwkv",
    )(proj3, mu, w0, w2d, a0, a2, g2, k_k, k_a, r_k, ln_w, ln_b)


def _rope(x, cosf, sinf, lane):
    rot = jnp.where(lane < ROPE_DIM // 2, pltpu.roll(x, ATTN_E - ROPE_DIM // 2, axis=1),
                    pltpu.roll(x, ROPE_DIM // 2, axis=1))
    return x * cosf + rot * sinf


def _attn_kernel(q_ref, kc_ref, kp_ref, vc_ref, vp_ref, cc_ref, sc_ref, cp_ref, sp_ref,
                 o_ref, l_ref):
    n = pl.program_id(2)
    blk = ATTN_BLK
    qi = lax.broadcasted_iota(jnp.int32, (blk, blk), 0)
    kj = lax.broadcasted_iota(jnp.int32, (blk, blk), 1)
    lane = lax.broadcasted_iota(jnp.int32, (blk, ATTN_E), 1)
    cur_ok = kj <= qi
    prev_ok = jnp.logical_and(kj >= qi, n > 0)
    cc, sc, cp, sp = cc_ref[0], sc_ref[0], cp_ref[0], sp_ref[0]
    scale = ATTN_E ** -0.5
    for hh in range(HEADS_PER_GROUP):
        sl = slice(hh * ATTN_E, (hh + 1) * ATTN_E)
        q = _rope(q_ref[0, :, sl], cc, sc, lane).astype(bf16)
        kc = _rope(kc_ref[0, :, sl], cc, sc, lane).astype(bf16)
        kp = _rope(kp_ref[0, :, sl], cp, sp, lane).astype(bf16)
        s_c = jnp.where(cur_ok, _dot(q, kc, _NT) * scale, -jnp.inf)
        s_p = jnp.where(prev_ok, _dot(q, kp, _NT) * scale, -jnp.inf)
        m = jnp.maximum(jnp.max(s_c, axis=-1, keepdims=True), jnp.max(s_p, axis=-1, keepdims=True))
        p_c = jnp.exp(s_c - m)
        p_p = jnp.exp(s_p - m)
        l = jnp.sum(p_c, axis=-1, keepdims=True) + jnp.sum(p_p, axis=-1, keepdims=True)
        acc = (_dot(p_c.astype(bf16), vc_ref[0, :, sl].astype(bf16))
               + _dot(p_p.astype(bf16), vp_ref[0, :, sl].astype(bf16)))
        o_ref[0, :, sl] = acc / l
        l_ref[0, :, sl] = jnp.broadcast_to(m + jnp.log(l), (blk, ATTN_E))


def _attn_group(proj3, cosf, sinf, gi, dil):
    B, S, NP = proj3.shape
    Lr = S // dil
    nb = Lr // ATTN_BLK
    pv = proj3.reshape(B, Lr, dil * NP)
    cv = cosf.reshape(B, Lr, dil * ATTN_E)
    sv = sinf.reshape(B, Lr, dil * ATTN_E)
    nblk = NP // ATTN_GW
    qo = ATTN_OFF // ATTN_GW + gi
    ko = qo + ATTN_WIDTH // ATTN_GW
    vo = ko + ATTN_WIDTH // ATTN_GW

    def cur(off):
        return pl.BlockSpec((1, ATTN_BLK, ATTN_GW), lambda b, r, n: (b, n, r * nblk + off))

    def prev(off):
        return pl.BlockSpec((1, ATTN_BLK, ATTN_GW),
                            lambda b, r, n: (b, jnp.maximum(n - 1, 0), r * nblk + off))

    tc = pl.BlockSpec((1, ATTN_BLK, ATTN_E), lambda b, r, n: (b, n, r))
    tp = pl.BlockSpec((1, ATTN_BLK, ATTN_E), lambda b, r, n: (b, jnp.maximum(n - 1, 0), r))
    out = pl.BlockSpec((1, ATTN_BLK, ATTN_GW), lambda b, r, n: (b, n, r))
    o, l = pl.pallas_call(
        _attn_kernel,
        out_shape=[jax.ShapeDtypeStruct((B, Lr, dil * ATTN_GW), f32)] * 2,
        grid=(B, dil, nb),
        in_specs=[cur(qo), cur(ko), prev(ko), cur(vo), prev(vo), tc, tc, tp, tp],
        out_specs=[out, out],
        compiler_params=pltpu.CompilerParams(
            dimension_semantics=("arbitrary", "arbitrary", "arbitrary"),
            vmem_limit_bytes=VMEM_LIMIT),
        name=f"attn_g{gi}",
    )(pv, pv, pv, pv, pv, cv, sv, cv, sv)
    return o.reshape(B * S, ATTN_GW), l.reshape(B * S, ATTN_GW)


def _post_kernel(x_ref, rw_ref, o0_ref, o1_ref, o2_ref, l0_ref, l1_ref, l2_ref, ga_ref, gb_ref,
                 wor_ref, woa_ref, wout_ref, gn_ref, wr_ref, br_ref,
                 x1_ref, h2_ref, lg_ref):
    l0, l1, l2 = l0_ref[...], l1_ref[...], l2_ref[...]
    m = jnp.maximum(jnp.maximum(l0, l1), l2)
    e0, e1, e2 = jnp.exp(l0 - m), jnp.exp(l1 - m), jnp.exp(l2 - m)
    o = (e0 * o0_ref[...] + e1 * o1_ref[...] + e2 * o2_ref[...]) / (e0 + e1 + e2)
    y_b = _dot(o.astype(bf16), woa_ref[...])
    y_a = _dot(rw_ref[...], wor_ref[...])
    merged = jax.nn.sigmoid(ga_ref[...]) * y_a + jax.nn.sigmoid(gb_ref[...]) * y_b
    x1 = x_ref[...] + _dot(merged.astype(bf16), wout_ref[...])
    x1_ref[...] = x1
    ms = jnp.mean(x1 * x1, axis=-1, keepdims=True)
    h2 = x1 * lax.rsqrt(ms + NORM_EPS) * gn_ref[...]
    h2_ref[...] = h2.astype(bf16)
    lg_ref[...] = _dot(h2, wr_ref[...], precision=HIGHEST) + br_ref[...]


def _post(x2, rw, outs, lses, proj, wor, woa, wout, gn, wr, br, tm=256):
    T, D = x2.shape
    NP = proj.shape[1]
    row = lambda w: pl.BlockSpec((tm, w), lambda i: (i, 0))
    const = lambda a: pl.BlockSpec(a.shape, lambda i: (0, 0), pipeline_mode=pl.Buffered(1))
    ga_blk = GATE_OFF // D
    return pl.pallas_call(
        _post_kernel,
        out_shape=[jax.ShapeDtypeStruct((T, D), f32), jax.ShapeDtypeStruct((T, D), bf16),
                   jax.ShapeDtypeStruct((T, 128), f32)],
        grid=(T // tm,),
        in_specs=[row(D), row(RWKV_WIDTH)] + [row(ATTN_GW)] * 6
                 + [pl.BlockSpec((tm, D), lambda i: (i, ga_blk)),
                    pl.BlockSpec((tm, D), lambda i: (i, ga_blk + 1)),
                    const(wor), const(woa), const(wout), const(gn), const(wr), const(br)],
        out_specs=[row(D), row(D), row(128)],
        compiler_params=pltpu.CompilerParams(
            dimension_semantics=("arbitrary",), vmem_limit_bytes=VMEM_LIMIT),
        name="post",
    )(x2, rw, *outs, *lses, proj, proj, wor, woa, wout, gn, wr, br)


def _moe_kernel(te_ref, nv_ref, xs_ref, gate_ref, w1g_ref, w1l_ref, b1g_ref, b1l_ref,
                w2_ref, b2_ref, o_ref, acc_ref):
    t = pl.program_id(0)
    j = pl.program_id(1)
    nj = pl.num_programs(1)

    @pl.when(t < nv_ref[0])
    def _():
        @pl.when(j == 0)
        def _():
            acc_ref[...] = jnp.zeros_like(acc_ref)

        xs = xs_ref[...]
        hg = _dot(xs, w1g_ref[...].astype(bf16)) + b1g_ref[...]
        hl = _dot(xs, w1l_ref[...].astype(bf16)) + b1l_ref[...]
        x_glu = jnp.minimum(hg, SWIGLU_LIMIT)
        x_lin = jnp.clip(hl, -SWIGLU_LIMIT, SWIGLU_LIMIT)
        act = x_glu * jax.nn.sigmoid(SWIGLU_ALPHA * x_glu) * (x_lin + 1.0)
        acc_ref[...] += _dot(act.astype(bf16), w2_ref[...].astype(bf16))

        @pl.when(j == nj - 1)
        def _():
            o_ref[...] = (acc_ref[...] + b2_ref[...]) * gate_ref[...]


def _moe(tile_e, n_valid, xs, gate, w1, b1, w2, b2, tm, tf=512):
    P, D = xs.shape
    E, _, F2 = w1.shape
    F = F2 // 2
    nj = F // tf
    nt = P // tm

    def tcl(t, nv):
        return jnp.minimum(t, nv[0] - 1)

    def jcl(t, j, nv):
        return jnp.where(t < nv[0], j, nj - 1)

    grid_spec = pltpu.PrefetchScalarGridSpec(
        num_scalar_prefetch=2,
        grid=(nt, nj),
        in_specs=[
            pl.BlockSpec((tm, D), lambda t, j, te, nv: (tcl(t, nv), 0)),
            pl.BlockSpec((tm, 1), lambda t, j, te, nv: (tcl(t, nv), 0)),
            pl.BlockSpec((None, D, tf), lambda t, j, te, nv: (te[tcl(t, nv)], 0, jcl(t, j, nv))),
            pl.BlockSpec((None, D, tf), lambda t, j, te, nv: (te[tcl(t, nv)], 0, nj + jcl(t, j, nv))),
            pl.BlockSpec((None, 1, tf), lambda t, j, te, nv: (te[tcl(t, nv)], 0, jcl(t, j, nv))),
            pl.BlockSpec((None, 1, tf), lambda t, j, te, nv: (te[tcl(t, nv)], 0, nj + jcl(t, j, nv))),
            pl.BlockSpec((None, tf, D), lambda t, j, te, nv: (te[tcl(t, nv)], jcl(t, j, nv), 0)),
            pl.BlockSpec((None, 1, D), lambda t, j, te, nv: (te[tcl(t, nv)], 0, 0)),
        ],
        out_specs=pl.BlockSpec((tm, D), lambda t, j, te, nv: (tcl(t, nv), 0)),
        scratch_shapes=[pltpu.VMEM((tm, D), f32)],
    )
    return pl.pallas_call(
        _moe_kernel,
        out_shape=jax.ShapeDtypeStruct((P, D), f32),
        grid_spec=grid_spec,
        compiler_params=pltpu.CompilerParams(
            dimension_semantics=("arbitrary", "arbitrary"), vmem_limit_bytes=VMEM_LIMIT),
        name="moe",
    )(tile_e, n_valid, xs, gate, w1, w1, b1, b1, w2, b2)


def _final_kernel(x1_ref, y_ref, g_ref, o_ref, *, apply_norm):
    D = x1_ref.shape[1]
    y = y_ref[:, 0:D]
    for kk in range(1, TOP_K):
        y = y + y_ref[:, kk * D:(kk + 1) * D]
    x2 = x1_ref[...] + y
    if apply_norm:
        ms = jnp.mean(x2 * x2, axis=-1, keepdims=True)
        x2 = x2 * lax.rsqrt(ms + NORM_EPS) * g_ref[...]
    o_ref[...] = x2


def _final(x1, y4, g, apply_norm, tm=256):
    T, D = x1.shape
    return pl.pallas_call(
        functools.partial(_final_kernel, apply_norm=apply_norm),
        out_shape=jax.ShapeDtypeStruct((T, D), f32),
        grid=(T // tm,),
        in_specs=[pl.BlockSpec((tm, D), lambda i: (i, 0)),
                  pl.BlockSpec((tm, TOP_K * D), lambda i: (i, 0)),
                  pl.BlockSpec((1, D), lambda i: (0, 0))],
        out_specs=pl.BlockSpec((tm, D), lambda i: (i, 0)),
        compiler_params=pltpu.CompilerParams(
            dimension_semantics=("arbitrary",), vmem_limit_bytes=VMEM_LIMIT),
        name="final",
    )(x1, y4, g)


def _pad_cols(a, n):
    return jnp.pad(a, ((0, 0), (0, n - a.shape[1])))


def _pad_rows(a, n):
    return jnp.pad(a, ((0, n - a.shape[0]), (0, 0)))


def _regroup_cols(a):
    W = RWKV_WIDTH
    o = 3 * W
    return jnp.concatenate([
        a[:, :o],
        _pad_cols(a[:, o:o + DECAY_LORA], 128),
        _pad_cols(a[:, o + DECAY_LORA:o + DECAY_LORA + ICLR_LORA], 128),
        _pad_cols(a[:, o + DECAY_LORA + ICLR_LORA:o + DECAY_LORA + ICLR_LORA + GATE_LORA], 256),
        a[:, o + DECAY_LORA + ICLR_LORA + GATE_LORA:],
    ], axis=1)


def _rope_tables(positions):
    half = ROPE_DIM // 2
    inv_freq = ROPE_THETA ** (-jnp.arange(half, dtype=f32) / half)
    ang = positions.astype(f32)[..., None] * inv_freq
    cos, sin = jnp.cos(ang), jnp.sin(ang)
    B, S = positions.shape
    cosf = jnp.concatenate([cos, cos, jnp.ones((B, S, ATTN_E - ROPE_DIM), f32)], axis=-1)
    sinf = jnp.concatenate([-sin, sin, jnp.zeros((B, S, ATTN_E - ROPE_DIM), f32)], axis=-1)
    return cosf, sinf


def _route(logits, tm, n_tiles):
    T, E = logits.shape
    top_val, top_idx = lax.top_k(logits, TOP_K)
    gates = jax.nn.softmax(top_val, axis=-1)
    flat_e = top_idx.reshape(-1)
    order = jnp.argsort(flat_e)
    e_sorted = flat_e[order]
    sizes = jnp.bincount(flat_e, length=E).astype(jnp.int32)
    padded = ((sizes + tm - 1) // tm) * tm
    pad_end = jnp.cumsum(padded)
    pad_start = pad_end - padded
    start = jnp.cumsum(sizes) - sizes
    dest = pad_start[e_sorted] + (jnp.arange(T * TOP_K, dtype=jnp.int32) - start[e_sorted])
    P = n_tiles * tm
    tok = jnp.zeros((P,), jnp.int32).at[dest].set((order // TOP_K).astype(jnp.int32))
    gate = jnp.zeros((P,), f32).at[dest].set(gates.reshape(-1)[order])
    pos = jnp.zeros((T * TOP_K,), jnp.int32).at[order].set(dest)
    tile_start = jnp.arange(n_tiles, dtype=jnp.int32) * tm
    tile_e = jnp.minimum(jnp.searchsorted(pad_end, tile_start, side="right"), E - 1).astype(jnp.int32)
    n_valid = (pad_end[-1] // tm).astype(jnp.int32).reshape(1)
    return tok, gate.reshape(P, 1), pos, tile_e, n_valid


def kernel(x, positions, norm_mix, w_in, shift_mu, w0, w2_decay, a0, a2_iclr, g2_gate, k_k, k_a,
           r_k, ln_x_w, ln_x_b, w_o_rwkv, w_o_attn, w_out, norm_ffn, w_router, b_router,
           w1, b1, w2, b2, norm_final):
    B, S, D = x.shape
    T = B * S
    E = w_router.shape[-1]
    depth = norm_mix.shape[0]
    xt = x.reshape(T, D)
    cosf, sinf = _rope_tables(positions)
    moe_tm = 512
    n_tiles = (T * TOP_K) // moe_tm + E
    for layer in range(depth):
        wp = _regroup_cols(w_in[layer]).astype(bf16)
        mu = _regroup_cols(shift_mu[layer][None, :])
        proj = _proj(xt, norm_mix[layer][None, :], wp)
        proj3 = proj.reshape(B, S, -1)
        rw = _rwkv(proj3, mu, w0[layer][None, :], _stack_rhs(_pad_rows(w2_decay[layer], 128)),
                   a0[layer][None, :], _stack_rhs(_pad_rows(a2_iclr[layer], 128)),
                   _pad_rows(g2_gate[layer], 256).astype(bf16), k_k[layer][None, :],
                   k_a[layer][None, :],
                   r_k[layer].reshape(1, -1), ln_x_w[layer][None, :], ln_x_b[layer][None, :])
        outs, lses = [], []
        for gi, (window, dil) in enumerate(ATTN_GROUPS):
            o_g, l_g = _attn_group(proj3, cosf, sinf, gi, dil)
            outs.append(o_g)
            lses.append(l_g)
        x1, h2, logits = _post(
            xt, rw.reshape(T, -1), outs, lses, proj,
            w_o_rwkv[layer].astype(bf16), w_o_attn[layer].astype(bf16), w_out[layer].astype(bf16),
            norm_ffn[layer][None, :], _pad_cols(w_router[layer], 128),
            _pad_cols(b_router[layer][None, :], 128))
        tok, gate, pos, tile_e, n_valid = _route(logits[:, :E], moe_tm, n_tiles)
        xs = h2[tok]
        ys = _moe(tile_e, n_valid, xs, gate, w1[layer], b1[layer][:, None, :], w2[layer],
                  b2[layer][:, None, :], moe_tm)
        y4 = ys[pos].reshape(T, TOP_K * D)
        xt = _final(x1, y4, norm_final[None, :], apply_norm=layer + 1 == depth)
    return xt.reshape(B, S, D)
```

```python
import functools

import jax
import jax.numpy as jnp
from jax import lax
from jax.experimental import pallas as pl
from jax.experimental.pallas import tpu as pltpu

f32 = jnp.float32
bf16 = jnp.bfloat16
HIGHEST = lax.Precision.HIGHEST

NORM_EPS = 1e-5
RWKV_GN_EPS = 64e-5
ROPE_THETA = 500000.0
SWIGLU_LIMIT = 7.0
SWIGLU_ALPHA = 1.702

RWKV_HEADS = 16
HEAD_N = 64
RWKV_WIDTH = RWKV_HEADS * HEAD_N
DECAY_LORA = 64
ICLR_LORA = 64
GATE_LORA = 160
ATTN_GROUPS = ((128, 1), (512, 4), (2048, 16))
HEADS_PER_GROUP = 4
ATTN_E = 128
ATTN_GW = HEADS_PER_GROUP * ATTN_E
ATTN_WIDTH = len(ATTN_GROUPS) * ATTN_GW
ROPE_DIM = ATTN_E // 4
TOP_K = 4

LORA_PAD = 512
ZW_OFF, ZA_OFF, ZG_OFF = 3 * RWKV_WIDTH, 3 * RWKV_WIDTH + 128, 3 * RWKV_WIDTH + 256
SHIFT_PAD = 3 * RWKV_WIDTH + LORA_PAD
ATTN_OFF = SHIFT_PAD
GATE_OFF = ATTN_OFF + 3 * ATTN_WIDTH

RWKV_CHUNK = 64
_RWKV_NWIDE = 12
ATTN_BLK = 128
VMEM_LIMIT = 56 * 1024 * 1024


def _dot(a, b, dims=(((1,), (0,)), ((), ())), precision=None):
    return lax.dot_general(a, b, dims, precision=precision, preferred_element_type=f32)


_NT = (((1,), (1,)), ((), ()))
_TN = (((0,), (0,)), ((), ()))


def _split2(a):
    hi = a.astype(bf16)
    return hi, (a - hi.astype(f32)).astype(bf16)


def _split3(a):
    hi = a.astype(bf16)
    rem = a - hi.astype(f32)
    mid = rem.astype(bf16)
    return hi, mid, (rem - mid.astype(f32)).astype(bf16)


def _stack_lhs(a):
    hi, lo = _split2(a)
    return jnp.concatenate([hi, lo, hi], axis=1)


def _stack_rhs(w):
    hi, lo = _split2(w)
    return jnp.concatenate([hi, hi, lo], axis=0)


def _proj_kernel(x_ref, g_ref, w_ref, o_ref, h_ref):
    @pl.when(pl.program_id(1) == 0)
    def _():
        x = x_ref[...]
        ms = jnp.mean(x * x, axis=-1, keepdims=True)
        h_ref[...] = (x * lax.rsqrt(ms + NORM_EPS) * g_ref[...]).astype(bf16)

    o_ref[...] = _dot(h_ref[...], w_ref[...])


def _proj(x2, g, wp, tm=512, tn=1024):
    T, D = x2.shape
    NP = wp.shape[1]
    return pl.pallas_call(
        _proj_kernel,
        out_shape=jax.ShapeDtypeStruct((T, NP), f32),
        grid=(T // tm, NP // tn),
        in_specs=[pl.BlockSpec((tm, D), lambda i, j: (i, 0)),
                  pl.BlockSpec((1, D), lambda i, j: (0, 0)),
                  pl.BlockSpec((D, tn), lambda i, j: (0, j))],
        out_specs=pl.BlockSpec((tm, tn), lambda i, j: (i, j)),
        scratch_shapes=[pltpu.VMEM((tm, D), bf16)],
        compiler_params=pltpu.CompilerParams(
            dimension_semantics=("arbitrary", "arbitrary"), vmem_limit_bytes=VMEM_LIMIT),
        name="proj",
    )(x2, g, wp)


def _rwkv_kernel(p_ref, mu_ref, w0_ref, w2_ref, a0_ref, a2_ref, g2_ref, kk_ref, ka_ref,
                 rk_ref, lnw_ref, lnb_ref, o_ref, state_ref, prev_ref, dec_ref, wide_ref):
    (at0_ref, at_ref, rt0_ref, rt_ref, bt_ref, kt_ref, bh_ref, kh_ref, v_ref, rkb_ref, g_ref,
     kkr_ref) = [wide_ref.at[i] for i in range(_RWKV_NWIDE)]
    L = RWKV_CHUNK
    W = RWKV_WIDTH
    c_idx = pl.program_id(1)

    @pl.when(c_idx == 0)
    def _():
        state_ref[...] = jnp.zeros_like(state_ref)
        prev_ref[...] = jnp.zeros_like(prev_ref)

    p = p_ref[0]
    row = lax.broadcasted_iota(jnp.int32, (L, 1), 0)
    shifted = jnp.where(row == 0, prev_ref[...], pltpu.roll(p, 1, axis=0))
    prev_ref[...] = p[L - 1:L, :]
    z = p + (shifted - p) * mu_ref[...]

    r = z[:, 0:W]
    k = z[:, W:2 * W]
    v = z[:, 2 * W:3 * W]
    zw = z[:, ZW_OFF:ZW_OFF + 128]
    za = z[:, ZA_OFF:ZA_OFF + 128]
    zg = z[:, ZG_OFF:ZG_OFF + 256]

    w_raw = w0_ref[...] + _dot(_stack_lhs(jnp.tanh(zw)), w2_ref[...])
    sp = jnp.maximum(-w_raw, 0.0) + jnp.log(1.0 + jnp.exp(-jnp.abs(w_raw)))
    lw = -jnp.exp(-sp - 0.5)
    a = jax.nn.sigmoid(a0_ref[...] + _dot(_stack_lhs(za), a2_ref[...]))
    g_ref[...] = _dot(jax.nn.sigmoid(zg).astype(bf16), g2_ref[...])

    ti = lax.broadcasted_iota(jnp.int32, (L, L), 0)
    si = lax.broadcasted_iota(jnp.int32, (L, L), 1)
    tril_incl = ti >= si
    tril_strict = ti > si
    tri = tril_incl.astype(bf16)
    lw_h, lw_m, lw_l = _split3(lw)
    c = _dot(jnp.concatenate([tri, tri, tri], axis=1), jnp.concatenate([lw_h, lw_m, lw_l], axis=0))
    cex = c - lw
    cm = c[L // 2 - 1:L // 2, :]
    cl = c[L - 1:L, :]
    e_m = jnp.exp(-cm)
    e_c = jnp.exp(c)
    e_ex = jnp.exp(cex)
    e_inv = jnp.exp(cm - c)
    e_tail = jnp.exp(cl - c)

    kkv = k * kk_ref[...]
    kmod = k * (1.0 + (a - 1.0) * ka_ref[...])
    at0_ref[...] = e_ex
    rt0 = r * e_c
    rt0_ref[...] = rt0
    rt_ref[...] = rt0 * e_m
    kt_ref[...] = kmod * e_inv
    kh_ref[...] = kmod * e_tail
    bt_ref[...] = a * e_inv
    bh_ref[...] = a * e_tail
    at_ref[...] = e_ex * e_m
    v_ref[...] = v
    dec_ref[...] = jnp.exp(cl)
    rkb_ref[...] = r * kmod * rk_ref[...]
    kkr_ref[...] = kkv

    PW = 2 * HEAD_N
    pairs = range(RWKV_HEADS // 2)
    lane = lax.broadcasted_iota(jnp.int32, (L, PW), 1)
    trow = lax.broadcasted_iota(jnp.int32, (L, PW), 0)
    first = lane < HEAD_N
    scol = jnp.where(first, lane, lane - HEAD_N)
    strict_p = trow > scol
    incl_p = trow >= scol
    eye_p = (trow == scol).astype(f32)
    r2 = lax.broadcasted_iota(jnp.int32, (2 * PW, PW), 0)
    l2 = lax.broadcasted_iota(jnp.int32, (2 * PW, PW), 1)
    seg_ones = (((r2 % PW) < HEAD_N) == (l2 < HEAD_N)).astype(bf16)
    rb = lax.broadcasted_iota(jnp.int32, (PW, PW), 0)
    lb = lax.broadcasted_iota(jnp.int32, (PW, PW), 1)
    bd_mask = (rb < HEAD_N) == (lb < HEAD_N)

    def bd(x):
        zero = jnp.zeros_like(x)
        return jnp.concatenate([jnp.where(first, x, zero), jnp.where(first, zero, x)], axis=0)

    def segsum(x):
        hi, lo = _split2(x)
        return _dot(jnp.concatenate([hi, lo], axis=1), seg_ones)

    ps = [slice(p * PW, (p + 1) * PW) for p in pairs]
    kkr = [kkr_ref[:, s] for s in ps]
    ssq = [segsum(x * x) for x in kkr]
    kkh = [x / jnp.maximum(jnp.sqrt(q), 1e-12) for x, q in zip(kkr, ssq)]
    at0 = [(-kh_ * at0_ref[:, s]).astype(bf16) for kh_, s in zip(kkh, ps)]
    at = [(-kh_ * at_ref[:, s]).astype(bf16) for kh_, s in zip(kkh, ps)]
    bt = [(kh_ * bt_ref[:, s]).astype(bf16) for kh_, s in zip(kkh, ps)]
    bh = [(kh_ * bh_ref[:, s]).astype(bf16) for kh_, s in zip(kkh, ps)]
    vb = [v_ref[:, s].astype(bf16) for s in ps]

    lhs = [jnp.concatenate([a_, rt_ref[:, s].astype(bf16)], axis=0) for a_, s in zip(at, ps)]
    sb = [_dot(l_, bd(b_), _NT) for l_, b_ in zip(lhs, bt)]
    sk = [_dot(l_, bd(kt_ref[:, s].astype(bf16)), _NT) for l_, s in zip(lhs, ps)]
    a_ab = [jnp.where(strict_p, x[:L], 0.0) for x in sb]
    a_rb = [jnp.where(incl_p, x[L:], 0.0).astype(bf16) for x in sb]
    a_ak = [jnp.where(strict_p, x[:L], 0.0).astype(bf16) for x in sk]
    a_rk = [jnp.where(incl_p, x[L:], 0.0).astype(bf16) for x in sk]

    tinv = [eye_p + x for x in a_ab]
    xp = a_ab
    n = 2
    while n < L:
        xb = [x.astype(bf16) for x in xp]
        xp = [_dot(x, bd(x)) for x in xb]
        tinv = [t + _dot(t.astype(bf16), bd(x.astype(bf16))) for t, x in zip(tinv, xp)]
        n *= 2
    tb = [t.astype(bf16) for t in tinv]

    akv = [_dot(a_, bd(v_)).astype(bf16) for a_, v_ in zip(a_ak, vb)]
    w12 = [_dot(t, jnp.concatenate([bd(a_), bd(k_)], axis=1)) for t, a_, k_ in zip(tb, at0, akv)]

    s0 = [state_ref[p] for p in pairs]
    s0b = [s.astype(bf16) for s in s0]
    uy = [_dot(jnp.concatenate([w[:, :PW].astype(bf16), rt0_ref[:, s].astype(bf16)], axis=0), sb_, _NT)
          for w, s, sb_ in zip(w12, ps, s0b)]
    u = [x[:L] + w[:, PW:] for x, w in zip(uy, w12)]
    ub = [x.astype(bf16) for x in u]
    y = [x[L:] + _dot(jnp.concatenate([rb_, rk_], axis=1), jnp.concatenate([bd(u_), bd(v_)], axis=0))
         for x, rb_, rk_, u_, v_ in zip(uy, a_rb, a_rk, ub, vb)]
    for p in pairs:
        upd = _dot(jnp.concatenate([ub[p], vb[p]], axis=0),
                   jnp.concatenate([bh[p], kh_ref[:, ps[p]].astype(bf16)], axis=0), _TN)
        state_ref[p] = s0[p] * dec_ref[:, ps[p]] + jnp.where(bd_mask, upd, 0.0)

    inv_n = 1.0 / HEAD_N
    mean = [segsum(x) * inv_n for x in y]
    yc = [x - m for x, m in zip(y, mean)]
    var = [segsum(x * x) * inv_n for x in yc]
    bonus = [segsum(rkb_ref[:, s]) for s in ps]
    for p in pairs:
        s = ps[p]
        yn = yc[p] * lax.rsqrt(var[p] + RWKV_GN_EPS) * lnw_ref[:, s] + lnb_ref[:, s]
        o_ref[0, :, s] = ((yn + bonus[p] * v_ref[:, s]) * g_ref[:, s]).astype(o_ref.dtype)


def _rwkv(proj3, mu, w0, w2d, a0, a2, g2, k_k, k_a, r_k, ln_w, ln_b):
    B, S, NP = proj3.shape
    L, W = RWKV_CHUNK, RWKV_WIDTH
    row = lambda n: pl.BlockSpec((1, n), lambda b, c: (0, 0))
    full = lambda a: pl.BlockSpec(a.shape, lambda b, c: (0, 0))
    return pl.pallas_call(
        _rwkv_kernel,
        out_shape=jax.ShapeDtypeStruct((B, S, W), bf16),
        grid=(B, S // L),
        in_specs=[pl.BlockSpec((1, L, SHIFT_PAD), lambda b, c: (b, c, 0)),
                  row(SHIFT_PAD), row(W), full(w2d), row(W), full(a2), full(g2),
                  row(W), row(W), row(W), row(W), row(W)],
        out_specs=pl.BlockSpec((1, L, W), lambda b, c: (b, c, 0)),
        scratch_shapes=[pltpu.VMEM((RWKV_HEADS // 2, 2 * HEAD_N, 2 * HEAD_N), f32),
                        pltpu.VMEM((1, SHIFT_PAD), f32), pltpu.VMEM((1, W), f32),
                        pltpu.VMEM((_RWKV_NWIDE, L, W), f32)],
        compiler_params=pltpu.CompilerParams(
            dimension_semantics=("arbitrary", "arbitrary"), vmem_limit_bytes=VMEM_LIMIT),
        name="rwkv",
    )(proj3, mu, w0, w2d, a0, a2, g2, k_k, k_a, r_k, ln_w, ln_b)


def _rope(x, cosf, sinf, lane):
    rot = jnp.where(lane < ROPE_DIM // 2, pltpu.roll(x, ATTN_E - ROPE_DIM // 2, axis=1),
                    pltpu.roll(x, ROPE_DIM // 2, axis=1))
    return x * cosf + rot * sinf


def _attn_kernel(q_ref, kc_ref, kp_ref, vc_ref, vp_ref, cc_ref, sc_ref, cp_ref, sp_ref,
                 o_ref, l_ref, *, dil, hps):
    n = pl.program_id(1)
    blk = ATTN_BLK
    qi = lax.broadcasted_iota(jnp.int32, (blk, blk), 0)
    kj = lax.broadcasted_iota(jnp.int32, (blk, blk), 1)
    lane = lax.broadcasted_iota(jnp.int32, (blk, ATTN_E), 1)
    cur_ok = kj <= qi
    prev_ok = jnp.logical_and(kj >= qi, n > 0)
    scale = ATTN_E ** -0.5
    for r in range(dil):
        rows = pl.ds(r, blk, stride=dil) if dil > 1 else slice(None)
        cc, sc, cp, sp = cc_ref[0, rows, :], sc_ref[0, rows, :], cp_ref[0, rows, :], sp_ref[0, rows, :]
        for hh in range(hps):
            sl = slice(hh * ATTN_E, (hh + 1) * ATTN_E)
            q = _rope(q_ref[0, rows, sl], cc, sc, lane).astype(bf16)
            kc = _rope(kc_ref[0, rows, sl], cc, sc, lane).astype(bf16)
            kp = _rope(kp_ref[0, rows, sl], cp, sp, lane).astype(bf16)
            s_c = jnp.where(cur_ok, _dot(q, kc, _NT) * scale, -jnp.inf)
            s_p = jnp.where(prev_ok, _dot(q, kp, _NT) * scale, -jnp.inf)
            m = jnp.maximum(jnp.max(s_c, axis=-1, keepdims=True),
                            jnp.max(s_p, axis=-1, keepdims=True))
            p_c = jnp.exp(s_c - m)
            p_p = jnp.exp(s_p - m)
            l = jnp.sum(p_c, axis=-1, keepdims=True) + jnp.sum(p_p, axis=-1, keepdims=True)
            acc = (_dot(p_c.astype(bf16), vc_ref[0, rows, sl].astype(bf16))
                   + _dot(p_p.astype(bf16), vp_ref[0, rows, sl].astype(bf16)))
            o_ref[0, rows, sl] = acc / l
            l_ref[0, rows, sl] = jnp.broadcast_to(m + jnp.log(l), (blk, ATTN_E))


def _attn_group(proj3, cosf, sinf, gi, dil):
    B, S, NP = proj3.shape
    R = ATTN_BLK * dil
    hps = max(1, HEADS_PER_GROUP // dil)
    cw = hps * ATTN_E
    per = ATTN_GW // cw
    qo = (ATTN_OFF + gi * ATTN_GW) // cw
    ko = qo + ATTN_WIDTH // cw
    vo = ko + ATTN_WIDTH // cw

    def cur(off):
        return pl.BlockSpec((1, R, cw), lambda b, n, h: (b, n, off + h))

    def prev(off):
        return pl.BlockSpec((1, R, cw), lambda b, n, h: (b, jnp.maximum(n - 1, 0), off + h))

    tc = pl.BlockSpec((1, R, ATTN_E), lambda b, n, h: (b, n, 0))
    tp = pl.BlockSpec((1, R, ATTN_E), lambda b, n, h: (b, jnp.maximum(n - 1, 0), 0))
    out = pl.BlockSpec((1, R, cw), lambda b, n, h: (b, n, h))
    o, l = pl.pallas_call(
        functools.partial(_attn_kernel, dil=dil, hps=hps),
        out_shape=[jax.ShapeDtypeStruct((B, S, ATTN_GW), f32)] * 2,
        grid=(B, S // R, per),
        in_specs=[cur(qo), cur(ko), prev(ko), cur(vo), prev(vo), tc, tc, tp, tp],
        out_specs=[out, out],
        compiler_params=pltpu.CompilerParams(
            dimension_semantics=("arbitrary", "arbitrary", "arbitrary"),
            vmem_limit_bytes=VMEM_LIMIT),
        name=f"attn_g{gi}",
    )(proj3, proj3, proj3, proj3, proj3, cosf, sinf, cosf, sinf)
    return o.reshape(B * S, ATTN_GW), l.reshape(B * S, ATTN_GW)


def _post_kernel(x_ref, rw_ref, o0_ref, o1_ref, o2_ref, l0_ref, l1_ref, l2_ref, ga_ref, gb_ref,
                 wor_ref, woa_ref, wout_ref, gn_ref, wr_ref, br_ref,
                 x1_ref, h2_ref, lg_ref):
    l0, l1, l2 = l0_ref[...], l1_ref[...], l2_ref[...]
    m = jnp.maximum(jnp.maximum(l0, l1), l2)
    e0, e1, e2 = jnp.exp(l0 - m), jnp.exp(l1 - m), jnp.exp(l2 - m)
    o = (e0 * o0_ref[...] + e1 * o1_ref[...] + e2 * o2_ref[...]) / (e0 + e1 + e2)
    y_b = _dot(o.astype(bf16), woa_ref[...])
    y_a = _dot(rw_ref[...], wor_ref[...])
    merged = jax.nn.sigmoid(ga_ref[...]) * y_a + jax.nn.sigmoid(gb_ref[...]) * y_b
    x1 = x_ref[...] + _dot(merged.astype(bf16), wout_ref[...])
    x1_ref[...] = x1
    ms = jnp.mean(x1 * x1, axis=-1, keepdims=True)
    h2 = x1 * lax.rsqrt(ms + NORM_EPS) * gn_ref[...]
    h2_ref[...] = h2.astype(bf16)
    lg_ref[...] = _dot(h2, wr_ref[...], precision=HIGHEST) + br_ref[...]


def _post(x2, rw, outs, lses, proj, wor, woa, wout, gn, wr, br, tm=256):
    T, D = x2.shape
    NP = proj.shape[1]
    row = lambda w: pl.BlockSpec((tm, w), lambda i: (i, 0))
    const = lambda a: pl.BlockSpec(a.shape, lambda i: (0, 0), pipeline_mode=pl.Buffered(1))
    ga_blk = GATE_OFF // D
    return pl.pallas_call(
        _post_kernel,
        out_shape=[jax.ShapeDtypeStruct((T, D), f32), jax.ShapeDtypeStruct((T, D), bf16),
                   jax.ShapeDtypeStruct((T, 128), f32)],
        grid=(T // tm,),
        in_specs=[row(D), row(RWKV_WIDTH)] + [row(ATTN_GW)] * 6
                 + [pl.BlockSpec((tm, D), lambda i: (i, ga_blk)),
                    pl.BlockSpec((tm, D), lambda i: (i, ga_blk + 1)),
                    const(wor), const(woa), const(wout), const(gn), const(wr), const(br)],
        out_specs=[row(D), row(D), row(128)],
        compiler_params=pltpu.CompilerParams(
            dimension_semantics=("arbitrary",), vmem_limit_bytes=VMEM_LIMIT),
        name="post",
    )(x2, rw, *outs, *lses, proj, proj, wor, woa, wout, gn, wr, br)


def _moe_kernel(te_ref, nv_ref, xs_ref, w1g_ref, w1l_ref, b1g_ref, b1l_ref,
                w2_ref, b2_ref, o_ref, acc_ref):
    t = pl.program_id(0)
    j = pl.program_id(1)
    nj = pl.num_programs(1)

    @pl.when(t < nv_ref[0])
    def _():
        @pl.when(j == 0)
        def _():
            acc_ref[...] = jnp.zeros_like(acc_ref)

        xs = xs_ref[...]
        hg = _dot(xs, w1g_ref[...].astype(bf16)) + b1g_ref[...]
        hl = _dot(xs, w1l_ref[...].astype(bf16)) + b1l_ref[...]
        x_glu = jnp.minimum(hg, SWIGLU_LIMIT)
        x_lin = jnp.clip(hl, -SWIGLU_LIMIT, SWIGLU_LIMIT)
        act = x_glu * jax.nn.sigmoid(SWIGLU_ALPHA * x_glu) * (x_lin + 1.0)
        acc_ref[...] += _dot(act.astype(bf16), w2_ref[...].astype(bf16))

        @pl.when(j == nj - 1)
        def _():
            o_ref[...] = (acc_ref[...] + b2_ref[...]).astype(o_ref.dtype)


def _moe(tile_e, n_valid, xs, w1, b1, w2, b2, tm, tf=512):
    P, D = xs.shape
    E, _, F2 = w1.shape
    F = F2 // 2
    nj = F // tf
    nt = P // tm

    def tcl(t, nv):
        return jnp.minimum(t, nv[0] - 1)

    def jcl(t, j, nv):
        return jnp.where(t < nv[0], j, nj - 1)

    grid_spec = pltpu.PrefetchScalarGridSpec(
        num_scalar_prefetch=2,
        grid=(nt, nj),
        in_specs=[
            pl.BlockSpec((tm, D), lambda t, j, te, nv: (tcl(t, nv), 0)),
            pl.BlockSpec((None, D, tf), lambda t, j, te, nv: (te[tcl(t, nv)], 0, jcl(t, j, nv))),
            pl.BlockSpec((None, D, tf), lambda t, j, te, nv: (te[tcl(t, nv)], 0, nj + jcl(t, j, nv))),
            pl.BlockSpec((None, 1, tf), lambda t, j, te, nv: (te[tcl(t, nv)], 0, jcl(t, j, nv))),
            pl.BlockSpec((None, 1, tf), lambda t, j, te, nv: (te[tcl(t, nv)], 0, nj + jcl(t, j, nv))),
            pl.BlockSpec((None, tf, D), lambda t, j, te, nv: (te[tcl(t, nv)], jcl(t, j, nv), 0)),
            pl.BlockSpec((None, 1, D), lambda t, j, te, nv: (te[tcl(t, nv)], 0, 0)),
        ],
        out_specs=pl.BlockSpec((tm, D), lambda t, j, te, nv: (tcl(t, nv), 0)),
        scratch_shapes=[pltpu.VMEM((tm, D), f32)],
    )
    return pl.pallas_call(
        _moe_kernel,
        out_shape=jax.ShapeDtypeStruct((P, D), bf16),
        grid_spec=grid_spec,
        compiler_params=pltpu.CompilerParams(
            dimension_semantics=("arbitrary", "arbitrary"), vmem_limit_bytes=VMEM_LIMIT),
        name="moe",
    )(tile_e, n_valid, xs, w1, w1, b1, b1, w2, b2)


def _final_kernel(x1_ref, y_ref, gate_ref, g_ref, o_ref, *, apply_norm):
    D = x1_ref.shape[1]
    gates = gate_ref[...]
    y = gates[:, 0:1] * y_ref[:, 0:D].astype(f32)
    for kk in range(1, TOP_K):
        y = y + gates[:, kk:kk + 1] * y_ref[:, kk * D:(kk + 1) * D].astype(f32)
    x2 = x1_ref[...] + y
    if apply_norm:
        ms = jnp.mean(x2 * x2, axis=-1, keepdims=True)
        x2 = x2 * lax.rsqrt(ms + NORM_EPS) * g_ref[...]
    o_ref[...] = x2


def _final(x1, y4, gates, g, apply_norm, tm=256):
    T, D = x1.shape
    return pl.pallas_call(
        functools.partial(_final_kernel, apply_norm=apply_norm),
        out_shape=jax.ShapeDtypeStruct((T, D), f32),
        grid=(T // tm,),
        in_specs=[pl.BlockSpec((tm, D), lambda i: (i, 0)),
                  pl.BlockSpec((tm, TOP_K * D), lambda i: (i, 0)),
                  pl.BlockSpec((tm, TOP_K), lambda i: (i, 0)),
                  pl.BlockSpec((1, D), lambda i: (0, 0))],
        out_specs=pl.BlockSpec((tm, D), lambda i: (i, 0)),
        compiler_params=pltpu.CompilerParams(
            dimension_semantics=("arbitrary",), vmem_limit_bytes=VMEM_LIMIT),
        name="final",
    )(x1, y4, gates, g)


def _pad_cols(a, n):
    return jnp.pad(a, ((0, 0), (0, n - a.shape[1])))


def _pad_rows(a, n):
    return jnp.pad(a, ((0, n - a.shape[0]), (0, 0)))


def _regroup_cols(a):
    W = RWKV_WIDTH
    o = 3 * W
    return jnp.concatenate([
        a[:, :o],
        _pad_cols(a[:, o:o + DECAY_LORA], 128),
        _pad_cols(a[:, o + DECAY_LORA:o + DECAY_LORA + ICLR_LORA], 128),
        _pad_cols(a[:, o + DECAY_LORA + ICLR_LORA:o + DECAY_LORA + ICLR_LORA + GATE_LORA], 256),
        a[:, o + DECAY_LORA + ICLR_LORA + GATE_LORA:],
    ], axis=1)


def _rope_tables(positions):
    half = ROPE_DIM // 2
    inv_freq = ROPE_THETA ** (-jnp.arange(half, dtype=f32) / half)
    ang = positions.astype(f32)[..., None] * inv_freq
    cos, sin = jnp.cos(ang), jnp.sin(ang)
    B, S = positions.shape
    cosf = jnp.concatenate([cos, cos, jnp.ones((B, S, ATTN_E - ROPE_DIM), f32)], axis=-1)
    sinf = jnp.concatenate([-sin, sin, jnp.zeros((B, S, ATTN_E - ROPE_DIM), f32)], axis=-1)
    return cosf, sinf


def _route(logits, tm, n_tiles):
    T, E = logits.shape
    n = T * TOP_K
    i32 = jnp.int32
    top_val, top_idx = lax.top_k(logits, TOP_K)
    gates = jax.nn.softmax(top_val, axis=-1)
    flat_e = top_idx.reshape(-1).astype(i32)
    iota = jnp.arange(n, dtype=i32)
    experts = jnp.arange(E, dtype=i32)

    def lookup(table, idx):
        return jnp.sum(jnp.where(idx[:, None] == experts[None, :], table[None, :], 0), axis=1)

    _, order = lax.sort((flat_e, iota), num_keys=1, is_stable=True)
    _, rank = lax.sort((order, iota), num_keys=1)
    sizes = jnp.sum((flat_e[:, None] == experts[None, :]).astype(i32), axis=0)
    padded = ((sizes + tm - 1) // tm) * tm
    pad_end = jnp.cumsum(padded)
    pad_start = pad_end - padded
    start = jnp.cumsum(sizes) - sizes
    pos = lookup(pad_start - start, flat_e) + rank
    tile_start = jnp.arange(n_tiles, dtype=i32) * tm
    tile_e = jnp.minimum(jnp.sum((pad_end[None, :] <= tile_start[:, None]).astype(i32), axis=1), E - 1)
    n_valid = (pad_end[-1] // tm).astype(i32).reshape(1)
    row_e = jnp.repeat(tile_e, tm)
    off = jnp.arange(n_tiles * tm, dtype=i32) - lookup(pad_start, row_e)
    src = jnp.clip(lookup(start, row_e) + off, 0, n - 1)
    tok = jnp.where(off < lookup(sizes, row_e), order[src] // TOP_K, 0)
    return tok, gates, pos, tile_e, n_valid


def kernel(x, positions, norm_mix, w_in, shift_mu, w0, w2_decay, a0, a2_iclr, g2_gate, k_k, k_a,
           r_k, ln_x_w, ln_x_b, w_o_rwkv, w_o_attn, w_out, norm_ffn, w_router, b_router,
           w1, b1, w2, b2, norm_final):
    B, S, D = x.shape
    T = B * S
    E = w_router.shape[-1]
    depth = norm_mix.shape[0]
    xt = x.reshape(T, D)
    cosf, sinf = _rope_tables(positions)
    moe_tm = 512
    n_tiles = (T * TOP_K) // moe_tm + E
    for layer in range(depth):
        wp = _regroup_cols(w_in[layer]).astype(bf16)
        mu = _regroup_cols(shift_mu[layer][None, :])
        proj = _proj(xt, norm_mix[layer][None, :], wp)
        proj3 = proj.reshape(B, S, -1)
        rw = _rwkv(proj3, mu, w0[layer][None, :], _stack_rhs(_pad_rows(w2_decay[layer], 128)),
                   a0[layer][None, :], _stack_rhs(_pad_rows(a2_iclr[layer], 128)),
                   _pad_rows(g2_gate[layer], 256).astype(bf16), k_k[layer][None, :],
                   k_a[layer][None, :],
                   r_k[layer].reshape(1, -1), ln_x_w[layer][None, :], ln_x_b[layer][None, :])
        outs, lses = [], []
        for gi, (window, dil) in enumerate(ATTN_GROUPS):
            o_g, l_g = _attn_group(proj3, cosf, sinf, gi, dil)
            outs.append(o_g)
            lses.append(l_g)
        x1, h2, logits = _post(
            xt, rw.reshape(T, -1), outs, lses, proj,
            w_o_rwkv[layer].astype(bf16), w_o_attn[layer].astype(bf16), w_out[layer].astype(bf16),
            norm_ffn[layer][None, :], _pad_cols(w_router[layer], 128),
            _pad_cols(b_router[layer][None, :], 128))
        tok, gates, pos, tile_e, n_valid = _route(logits[:, :E], moe_tm, n_tiles)
        xs = h2[tok]
        ys = _moe(tile_e, n_valid, xs, w1[layer], b1[layer][:, None, :], w2[layer],
                  b2[layer][:, None, :], moe_tm)
        y4 = ys[pos].reshape(T, TOP_K * D)
        xt = _final(x1, y4, gates, norm_final[None, :], apply_norm=layer + 1 == depth)
    return xt.reshape(B, S, D)
```

```python
import functools

import jax
import jax.numpy as jnp
from jax import lax
from jax.experimental import pallas as pl
from jax.experimental.pallas import tpu as pltpu

f32 = jnp.float32
bf16 = jnp.bfloat16
HIGHEST = lax.Precision.HIGHEST

NORM_EPS = 1e-5
RWKV_GN_EPS = 64e-5
ROPE_THETA = 500000.0
SWIGLU_LIMIT = 7.0
SWIGLU_ALPHA = 1.702

RWKV_HEADS = 16
HEAD_N = 64
RWKV_WIDTH = RWKV_HEADS * HEAD_N
DECAY_LORA = 64
ICLR_LORA = 64
GATE_LORA = 160
ATTN_GROUPS = ((128, 1), (512, 4), (2048, 16))
HEADS_PER_GROUP = 4
ATTN_E = 128
ATTN_GW = HEADS_PER_GROUP * ATTN_E
ATTN_WIDTH = len(ATTN_GROUPS) * ATTN_GW
ROPE_DIM = ATTN_E // 4
TOP_K = 4

LORA_PAD = 512
ZW_OFF, ZA_OFF, ZG_OFF = 3 * RWKV_WIDTH, 3 * RWKV_WIDTH + 128, 3 * RWKV_WIDTH + 256
SHIFT_PAD = 3 * RWKV_WIDTH + LORA_PAD
ATTN_OFF = SHIFT_PAD
GATE_OFF = ATTN_OFF + 3 * ATTN_WIDTH

RWKV_CHUNK = 64
_RWKV_NWIDE = 12
ATTN_BLK = 128
VMEM_LIMIT = 56 * 1024 * 1024


def _dot(a, b, dims=(((1,), (0,)), ((), ())), precision=None):
    return lax.dot_general(a, b, dims, precision=precision, preferred_element_type=f32)


_NT = (((1,), (1,)), ((), ()))
_TN = (((0,), (0,)), ((), ()))


def _split2(a):
    hi = a.astype(bf16)
    return hi, (a - hi.astype(f32)).astype(bf16)


def _split3(a):
    hi = a.astype(bf16)
    rem = a - hi.astype(f32)
    mid = rem.astype(bf16)
    return hi, mid, (rem - mid.astype(f32)).astype(bf16)


def _stack_lhs(a):
    hi, lo = _split2(a)
    return jnp.concatenate([hi, lo, hi], axis=1)


def _stack_rhs(w):
    hi, lo = _split2(w)
    return jnp.concatenate([hi, hi, lo], axis=0)


def _proj_kernel(x_ref, g_ref, w_ref, o_ref, h_ref):
    @pl.when(pl.program_id(1) == 0)
    def _():
        x = x_ref[...]
        ms = jnp.mean(x * x, axis=-1, keepdims=True)
        h_ref[...] = (x * lax.rsqrt(ms + NORM_EPS) * g_ref[...]).astype(bf16)

    o_ref[...] = _dot(h_ref[...], w_ref[...])


def _proj(x2, g, wp, tm=512, tn=1024):
    T, D = x2.shape
    NP = wp.shape[1]
    return pl.pallas_call(
        _proj_kernel,
        out_shape=jax.ShapeDtypeStruct((T, NP), f32),
        grid=(T // tm, NP // tn),
        in_specs=[pl.BlockSpec((tm, D), lambda i, j: (i, 0)),
                  pl.BlockSpec((1, D), lambda i, j: (0, 0)),
                  pl.BlockSpec((D, tn), lambda i, j: (0, j))],
        out_specs=pl.BlockSpec((tm, tn), lambda i, j: (i, j)),
        scratch_shapes=[pltpu.VMEM((tm, D), bf16)],
        compiler_params=pltpu.CompilerParams(
            dimension_semantics=("arbitrary", "arbitrary"), vmem_limit_bytes=VMEM_LIMIT),
        name="proj",
    )(x2, g, wp)


def _rwkv_kernel(p_ref, mu_ref, w0_ref, w2_ref, a0_ref, a2_ref, g2_ref, kk_ref, ka_ref,
                 rk_ref, lnw_ref, lnb_ref, o_ref, state_ref, prev_ref, dec_ref, wide_ref):
    (at0_ref, at_ref, rt0_ref, rt_ref, bt_ref, kt_ref, bh_ref, kh_ref, v_ref, rkb_ref, g_ref,
     kkr_ref) = [wide_ref.at[i] for i in range(_RWKV_NWIDE)]
    L = RWKV_CHUNK
    W = RWKV_WIDTH
    c_idx = pl.program_id(1)

    @pl.when(c_idx == 0)
    def _():
        state_ref[...] = jnp.zeros_like(state_ref)
        prev_ref[...] = jnp.zeros_like(prev_ref)

    p = p_ref[0]
    row = lax.broadcasted_iota(jnp.int32, (L, 1), 0)
    shifted = jnp.where(row == 0, prev_ref[...], pltpu.roll(p, 1, axis=0))
    prev_ref[...] = p[L - 1:L, :]
    z = p + (shifted - p) * mu_ref[...]

    r = z[:, 0:W]
    k = z[:, W:2 * W]
    v = z[:, 2 * W:3 * W]
    zw = z[:, ZW_OFF:ZW_OFF + 128]
    za = z[:, ZA_OFF:ZA_OFF + 128]
    zg = z[:, ZG_OFF:ZG_OFF + 256]

    w_raw = w0_ref[...] + _dot(_stack_lhs(jnp.tanh(zw)), w2_ref[...])
    sp = jnp.maximum(-w_raw, 0.0) + jnp.log(1.0 + jnp.exp(-jnp.abs(w_raw)))
    lw = -jnp.exp(-sp - 0.5)
    a = jax.nn.sigmoid(a0_ref[...] + _dot(_stack_lhs(za), a2_ref[...]))
    g_ref[...] = _dot(jax.nn.sigmoid(zg).astype(bf16), g2_ref[...])

    ti = lax.broadcasted_iota(jnp.int32, (L, L), 0)
    si = lax.broadcasted_iota(jnp.int32, (L, L), 1)
    tril_incl = ti >= si
    tril_strict = ti > si
    tri = tril_incl.astype(bf16)
    lw_h, lw_m, lw_l = _split3(lw)
    c = _dot(jnp.concatenate([tri, tri, tri], axis=1), jnp.concatenate([lw_h, lw_m, lw_l], axis=0))
    cex = c - lw
    cm = c[L // 2 - 1:L // 2, :]
    cl = c[L - 1:L, :]
    e_m = jnp.exp(-cm)
    e_c = jnp.exp(c)
    e_ex = jnp.exp(cex)
    e_inv = jnp.exp(cm - c)
    e_tail = jnp.exp(cl - c)

    kkv = k * kk_ref[...]
    kmod = k * (1.0 + (a - 1.0) * ka_ref[...])
    at0_ref[...] = e_ex
    rt0 = r * e_c
    rt0_ref[...] = rt0
    rt_ref[...] = rt0 * e_m
    kt_ref[...] = kmod * e_inv
    kh_ref[...] = kmod * e_tail
    bt_ref[...] = a * e_inv
    bh_ref[...] = a * e_tail
    at_ref[...] = e_ex * e_m
    v_ref[...] = v
    dec_ref[...] = jnp.exp(cl)
    rkb_ref[...] = r * kmod * rk_ref[...]
    kkr_ref[...] = kkv

    PW = 2 * HEAD_N
    pairs = range(RWKV_HEADS // 2)
    lane = lax.broadcasted_iota(jnp.int32, (L, PW), 1)
    trow = lax.broadcasted_iota(jnp.int32, (L, PW), 0)
    first = lane < HEAD_N
    scol = jnp.where(first, lane, lane - HEAD_N)
    strict_p = trow > scol
    incl_p = trow >= scol
    eye_p = (trow == scol).astype(f32)
    r2 = lax.broadcasted_iota(jnp.int32, (2 * PW, PW), 0)
    l2 = lax.broadcasted_iota(jnp.int32, (2 * PW, PW), 1)
    seg_ones = (((r2 % PW) < HEAD_N) == (l2 < HEAD_N)).astype(bf16)
    rb = lax.broadcasted_iota(jnp.int32, (PW, PW), 0)
    lb = lax.broadcasted_iota(jnp.int32, (PW, PW), 1)
    bd_mask = (rb < HEAD_N) == (lb < HEAD_N)

    def bd(x):
        zero = jnp.zeros_like(x)
        return jnp.concatenate([jnp.where(first, x, zero), jnp.where(first, zero, x)], axis=0)

    def segsum(x):
        hi, lo = _split2(x)
        return _dot(jnp.concatenate([hi, lo], axis=1), seg_ones)

    ps = [slice(p * PW, (p + 1) * PW) for p in pairs]
    kkr = [kkr_ref[:, s] for s in ps]
    ssq = [segsum(x * x) for x in kkr]
    kkh = [x / jnp.maximum(jnp.sqrt(q), 1e-12) for x, q in zip(kkr, ssq)]
    at0 = [(-kh_ * at0_ref[:, s]).astype(bf16) for kh_, s in zip(kkh, ps)]
    at = [(-kh_ * at_ref[:, s]).astype(bf16) for kh_, s in zip(kkh, ps)]
    bt = [(kh_ * bt_ref[:, s]).astype(bf16) for kh_, s in zip(kkh, ps)]
    bh = [(kh_ * bh_ref[:, s]).astype(bf16) for kh_, s in zip(kkh, ps)]
    vb = [v_ref[:, s].astype(bf16) for s in ps]

    lhs = [jnp.concatenate([a_, rt_ref[:, s].astype(bf16)], axis=0) for a_, s in zip(at, ps)]
    sb = [_dot(l_, bd(b_), _NT) for l_, b_ in zip(lhs, bt)]
    sk = [_dot(l_, bd(kt_ref[:, s].astype(bf16)), _NT) for l_, s in zip(lhs, ps)]
    a_ab = [jnp.where(strict_p, x[:L], 0.0) for x in sb]
    a_rb = [jnp.where(incl_p, x[L:], 0.0).astype(bf16) for x in sb]
    a_ak = [jnp.where(strict_p, x[:L], 0.0).astype(bf16) for x in sk]
    a_rk = [jnp.where(incl_p, x[L:], 0.0).astype(bf16) for x in sk]

    tinv = [eye_p + x for x in a_ab]
    xp = a_ab
    n = 2
    while n < L:
        xb = [x.astype(bf16) for x in xp]
        xp = [_dot(x, bd(x)) for x in xb]
        tinv = [t + _dot(t.astype(bf16), bd(x.astype(bf16))) for t, x in zip(tinv, xp)]
        n *= 2
    tb = [t.astype(bf16) for t in tinv]

    akv = [_dot(a_, bd(v_)).astype(bf16) for a_, v_ in zip(a_ak, vb)]
    w12 = [_dot(t, jnp.concatenate([bd(a_), bd(k_)], axis=1)) for t, a_, k_ in zip(tb, at0, akv)]

    s0 = [state_ref[p] for p in pairs]
    s0b = [s.astype(bf16) for s in s0]
    uy = [_dot(jnp.concatenate([w[:, :PW].astype(bf16), rt0_ref[:, s].astype(bf16)], axis=0), sb_, _NT)
          for w, s, sb_ in zip(w12, ps, s0b)]
    u = [x[:L] + w[:, PW:] for x, w in zip(uy, w12)]
    ub = [x.astype(bf16) for x in u]
    y = [x[L:] + _dot(jnp.concatenate([rb_, rk_], axis=1), jnp.concatenate([bd(u_), bd(v_)], axis=0))
         for x, rb_, rk_, u_, v_ in zip(uy, a_rb, a_rk, ub, vb)]
    for p in pairs:
        upd = _dot(jnp.concatenate([ub[p], vb[p]], axis=0),
                   jnp.concatenate([bh[p], kh_ref[:, ps[p]].astype(bf16)], axis=0), _TN)
        state_ref[p] = s0[p] * dec_ref[:, ps[p]] + jnp.where(bd_mask, upd, 0.0)

    inv_n = 1.0 / HEAD_N
    mean = [segsum(x) * inv_n for x in y]
    yc = [x - m for x, m in zip(y, mean)]
    var = [segsum(x * x) * inv_n for x in yc]
    bonus = [segsum(rkb_ref[:, s]) for s in ps]
    for p in pairs:
        s = ps[p]
        yn = yc[p] * lax.rsqrt(var[p] + RWKV_GN_EPS) * lnw_ref[:, s] + lnb_ref[:, s]
        o_ref[0, :, s] = ((yn + bonus[p] * v_ref[:, s]) * g_ref[:, s]).astype(o_ref.dtype)


def _rwkv(proj3, mu, w0, w2d, a0, a2, g2, k_k, k_a, r_k, ln_w, ln_b):
    B, S, NP = proj3.shape
    L, W = RWKV_CHUNK, RWKV_WIDTH
    row = lambda n: pl.BlockSpec((1, n), lambda b, c: (0, 0))
    full = lambda a: pl.BlockSpec(a.shape, lambda b, c: (0, 0))
    return pl.pallas_call(
        _rwkv_kernel,
        out_shape=jax.ShapeDtypeStruct((B, S, W), bf16),
        grid=(B, S // L),
        in_specs=[pl.BlockSpec((1, L, SHIFT_PAD), lambda b, c: (b, c, 0)),
                  row(SHIFT_PAD), row(W), full(w2d), row(W), full(a2), full(g2),
                  row(W), row(W), row(W), row(W), row(W)],
        out_specs=pl.BlockSpec((1, L, W), lambda b, c: (b, c, 0)),
        scratch_shapes=[pltpu.VMEM((RWKV_HEADS // 2, 2 * HEAD_N, 2 * HEAD_N), f32),
                        pltpu.VMEM((1, SHIFT_PAD), f32), pltpu.VMEM((1, W), f32),
                        pltpu.VMEM((_RWKV_NWIDE, L, W), f32)],
        compiler_params=pltpu.CompilerParams(
            dimension_semantics=("arbitrary", "arbitrary"), vmem_limit_bytes=VMEM_LIMIT),
        name="rwkv",
    )(proj3, mu, w0, w2d, a0, a2, g2, k_k, k_a, r_k, ln_w, ln_b)


def _rope(x, cosf, sinf, lane):
    rot = jnp.where(lane < ROPE_DIM // 2, pltpu.roll(x, ATTN_E - ROPE_DIM // 2, axis=1),
                    pltpu.roll(x, ROPE_DIM // 2, axis=1))
    return x * cosf + rot * sinf


def _attn_kernel(q_ref, kc_ref, kp_ref, vc_ref, vp_ref, cc_ref, sc_ref, cp_ref, sp_ref,
                 o_ref, l_ref, *, dil, hps):
    n = pl.program_id(1)
    blk = ATTN_BLK
    qi = lax.broadcasted_iota(jnp.int32, (blk, blk), 0)
    kj = lax.broadcasted_iota(jnp.int32, (blk, blk), 1)
    lane = lax.broadcasted_iota(jnp.int32, (blk, ATTN_E), 1)
    cur_ok = kj <= qi
    prev_ok = jnp.logical_and(kj >= qi, n > 0)
    scale = ATTN_E ** -0.5
    for r in range(dil):
        rows = pl.ds(r, blk, stride=dil) if dil > 1 else slice(None)
        cc, sc, cp, sp = cc_ref[0, rows, :], sc_ref[0, rows, :], cp_ref[0, rows, :], sp_ref[0, rows, :]
        for hh in range(hps):
            sl = slice(hh * ATTN_E, (hh + 1) * ATTN_E)
            q = _rope(q_ref[0, rows, sl], cc, sc, lane).astype(bf16)
            kc = _rope(kc_ref[0, rows, sl], cc, sc, lane).astype(bf16)
            kp = _rope(kp_ref[0, rows, sl], cp, sp, lane).astype(bf16)
            s_c = jnp.where(cur_ok, _dot(q, kc, _NT) * scale, -jnp.inf)
            s_p = jnp.where(prev_ok, _dot(q, kp, _NT) * scale, -jnp.inf)
            m = jnp.maximum(jnp.max(s_c, axis=-1, keepdims=True),
                            jnp.max(s_p, axis=-1, keepdims=True))
            p_c = jnp.exp(s_c - m)
            p_p = jnp.exp(s_p - m)
            l = jnp.sum(p_c, axis=-1, keepdims=True) + jnp.sum(p_p, axis=-1, keepdims=True)
            acc = (_dot(p_c.astype(bf16), vc_ref[0, rows, sl].astype(bf16))
                   + _dot(p_p.astype(bf16), vp_ref[0, rows, sl].astype(bf16)))
            o_ref[0, rows, sl] = acc / l
            l_ref[0, rows, sl] = jnp.broadcast_to(m + jnp.log(l), (blk, ATTN_E))


def _attn_group(proj3, cosf, sinf, gi, dil):
    B, S, NP = proj3.shape
    R = ATTN_BLK * dil
    hps = max(1, HEADS_PER_GROUP // dil)
    cw = hps * ATTN_E
    per = ATTN_GW // cw
    qo = (ATTN_OFF + gi * ATTN_GW) // cw
    ko = qo + ATTN_WIDTH // cw
    vo = ko + ATTN_WIDTH // cw

    def cur(off):
        return pl.BlockSpec((1, R, cw), lambda b, n, h: (b, n, off + h))

    def prev(off):
        return pl.BlockSpec((1, R, cw), lambda b, n, h: (b, jnp.maximum(n - 1, 0), off + h))

    tc = pl.BlockSpec((1, R, ATTN_E), lambda b, n, h: (b, n, 0))
    tp = pl.BlockSpec((1, R, ATTN_E), lambda b, n, h: (b, jnp.maximum(n - 1, 0), 0))
    out = pl.BlockSpec((1, R, cw), lambda b, n, h: (b, n, h))
    o, l = pl.pallas_call(
        functools.partial(_attn_kernel, dil=dil, hps=hps),
        out_shape=[jax.ShapeDtypeStruct((B, S, ATTN_GW), f32)] * 2,
        grid=(B, S // R, per),
        in_specs=[cur(qo), cur(ko), prev(ko), cur(vo), prev(vo), tc, tc, tp, tp],
        out_specs=[out, out],
        compiler_params=pltpu.CompilerParams(
            dimension_semantics=("arbitrary", "arbitrary", "arbitrary"),
            vmem_limit_bytes=VMEM_LIMIT),
        name=f"attn_g{gi}",
    )(proj3, proj3, proj3, proj3, proj3, cosf, sinf, cosf, sinf)
    return o.reshape(B * S, ATTN_GW), l.reshape(B * S, ATTN_GW)


def _post_kernel(x_ref, rw_ref, o0_ref, o1_ref, o2_ref, l0_ref, l1_ref, l2_ref, ga_ref, gb_ref,
                 wor_ref, woa_ref, wout_ref, gn_ref, wr_ref, br_ref,
                 x1_ref, h2_ref, lg_ref):
    l0, l1, l2 = l0_ref[...], l1_ref[...], l2_ref[...]
    m = jnp.maximum(jnp.maximum(l0, l1), l2)
    e0, e1, e2 = jnp.exp(l0 - m), jnp.exp(l1 - m), jnp.exp(l2 - m)
    o = (e0 * o0_ref[...] + e1 * o1_ref[...] + e2 * o2_ref[...]) / (e0 + e1 + e2)
    y_b = _dot(o.astype(bf16), woa_ref[...])
    y_a = _dot(rw_ref[...], wor_ref[...])
    merged = jax.nn.sigmoid(ga_ref[...]) * y_a + jax.nn.sigmoid(gb_ref[...]) * y_b
    x1 = x_ref[...] + _dot(merged.astype(bf16), wout_ref[...])
    x1_ref[...] = x1
    ms = jnp.mean(x1 * x1, axis=-1, keepdims=True)
    h2 = x1 * lax.rsqrt(ms + NORM_EPS) * gn_ref[...]
    h2_ref[...] = h2.astype(bf16)
    lg_ref[...] = _dot(h2, wr_ref[...], precision=HIGHEST) + br_ref[...]


def _post(x2, rw, outs, lses, proj, wor, woa, wout, gn, wr, br, tm=256):
    T, D = x2.shape
    NP = proj.shape[1]
    row = lambda w: pl.BlockSpec((tm, w), lambda i: (i, 0))
    const = lambda a: pl.BlockSpec(a.shape, lambda i: (0, 0), pipeline_mode=pl.Buffered(1))
    ga_blk = GATE_OFF // D
    return pl.pallas_call(
        _post_kernel,
        out_shape=[jax.ShapeDtypeStruct((T, D), f32), jax.ShapeDtypeStruct((T, D), bf16),
                   jax.ShapeDtypeStruct((T, 128), f32)],
        grid=(T // tm,),
        in_specs=[row(D), row(RWKV_WIDTH)] + [row(ATTN_GW)] * 6
                 + [pl.BlockSpec((tm, D), lambda i: (i, ga_blk)),
                    pl.BlockSpec((tm, D), lambda i: (i, ga_blk + 1)),
                    const(wor), const(woa), const(wout), const(gn), const(wr), const(br)],
        out_specs=[row(D), row(D), row(128)],
        compiler_params=pltpu.CompilerParams(
            dimension_semantics=("arbitrary",), vmem_limit_bytes=VMEM_LIMIT),
        name="post",
    )(x2, rw, *outs, *lses, proj, proj, wor, woa, wout, gn, wr, br)


def _moe_up_kernel(st, sc, se, sf, nv, xs_ref, w1g_ref, w1l_ref, b1g_ref, b1l_ref, o_ref,
                   wg_ref, wl_ref):
    s = pl.program_id(0)

    @pl.when(s < nv[0])
    def _():
        @pl.when(sf[s] == 1)
        def _():
            wg_ref[...] = w1g_ref[...].astype(bf16)
            wl_ref[...] = w1l_ref[...].astype(bf16)

        xs = xs_ref[...]
        hg = _dot(xs, wg_ref[...]) + b1g_ref[...]
        hl = _dot(xs, wl_ref[...]) + b1l_ref[...]
        x_glu = jnp.minimum(hg, SWIGLU_LIMIT)
        x_lin = jnp.clip(hl, -SWIGLU_LIMIT, SWIGLU_LIMIT)
        act = x_glu * jax.nn.sigmoid(SWIGLU_ALPHA * x_glu) * (x_lin + 1.0)
        o_ref[...] = act.astype(o_ref.dtype)


def _moe_down_kernel(st, sc, se, sf, nv, a_ref, w2_ref, b2_ref, o_ref, w_ref):
    s = pl.program_id(0)

    @pl.when(s < nv[0])
    def _():
        @pl.when(sf[s] == 1)
        def _():
            w_ref[...] = w2_ref[...].astype(bf16)

        o_ref[...] = (_dot(a_ref[...], w_ref[...]) + b2_ref[...]).astype(o_ref.dtype)


def _moe_schedule(tile_e, n_valid, first_tile, n_tiles_e, nc):
    i32 = jnp.int32
    nt = tile_e.shape[0]
    E = first_tile.shape[0]
    experts = jnp.arange(E, dtype=i32)
    s = jnp.arange(nt * nc, dtype=i32)
    s = jnp.minimum(s, nc * n_valid[0] - 1)
    e = tile_e[s // nc]
    pick = lambda table: jnp.sum(jnp.where(e[:, None] == experts[None, :], table[None, :], 0), axis=1)
    ft, ne = pick(first_tile), jnp.maximum(pick(n_tiles_e), 1)
    local = s - nc * ft
    col = local // ne
    row = ft + local % ne
    changed = jnp.logical_or(e != jnp.roll(e, 1), col != jnp.roll(col, 1)).at[0].set(True)
    return row.astype(i32), col.astype(i32), e.astype(i32), changed.astype(i32), (nc * n_valid).astype(i32)


def _moe(tile_e, n_valid, first_tile, n_tiles_e, xs, w1, b1, w2, b2, tm, tf=1024, tn=1024):
    P, D = xs.shape
    E, _, F2 = w1.shape
    F = F2 // 2
    nj = F // tf
    nn = D // tn
    nt = P // tm
    params = pltpu.CompilerParams(dimension_semantics=("arbitrary",), vmem_limit_bytes=VMEM_LIMIT)

    sched = _moe_schedule(tile_e, n_valid, first_tile, n_tiles_e, nj)
    act = pl.pallas_call(
        _moe_up_kernel,
        out_shape=jax.ShapeDtypeStruct((P, F), bf16),
        grid_spec=pltpu.PrefetchScalarGridSpec(
            num_scalar_prefetch=5,
            grid=(nt * nj,),
            in_specs=[
                pl.BlockSpec((tm, D), lambda s, st, sc, se, sf, nv: (st[s], 0)),
                pl.BlockSpec((None, D, tf), lambda s, st, sc, se, sf, nv: (se[s], 0, sc[s])),
                pl.BlockSpec((None, D, tf), lambda s, st, sc, se, sf, nv: (se[s], 0, nj + sc[s])),
                pl.BlockSpec((None, 1, tf), lambda s, st, sc, se, sf, nv: (se[s], 0, sc[s])),
                pl.BlockSpec((None, 1, tf), lambda s, st, sc, se, sf, nv: (se[s], 0, nj + sc[s])),
            ],
            out_specs=pl.BlockSpec((tm, tf), lambda s, st, sc, se, sf, nv: (st[s], sc[s])),
            scratch_shapes=[pltpu.VMEM((D, tf), bf16), pltpu.VMEM((D, tf), bf16)],
        ),
        compiler_params=params,
        name="moe_up",
    )(*sched, xs, w1, w1, b1, b1)

    sched = _moe_schedule(tile_e, n_valid, first_tile, n_tiles_e, nn)
    return pl.pallas_call(
        _moe_down_kernel,
        out_shape=jax.ShapeDtypeStruct((P, D), bf16),
        grid_spec=pltpu.PrefetchScalarGridSpec(
            num_scalar_prefetch=5,
            grid=(nt * nn,),
            in_specs=[
                pl.BlockSpec((tm, F), lambda s, st, sc, se, sf, nv: (st[s], 0)),
                pl.BlockSpec((None, F, tn), lambda s, st, sc, se, sf, nv: (se[s], 0, sc[s])),
                pl.BlockSpec((None, 1, tn), lambda s, st, sc, se, sf, nv: (se[s], 0, sc[s])),
            ],
            out_specs=pl.BlockSpec((tm, tn), lambda s, st, sc, se, sf, nv: (st[s], sc[s])),
            scratch_shapes=[pltpu.VMEM((F, tn), bf16)],
        ),
        compiler_params=params,
        name="moe_down",
    )(*sched, act, w2, b2)


def _final_kernel(x1_ref, y_ref, gate_ref, g_ref, o_ref, *, apply_norm):
    gates = gate_ref[...]
    y = gates[:, 0:1] * y_ref[0].astype(f32)
    for kk in range(1, TOP_K):
        y = y + gates[:, kk:kk + 1] * y_ref[kk].astype(f32)
    x2 = x1_ref[...] + y
    if apply_norm:
        ms = jnp.mean(x2 * x2, axis=-1, keepdims=True)
        x2 = x2 * lax.rsqrt(ms + NORM_EPS) * g_ref[...]
    o_ref[...] = x2


def _final(x1, y4, gates, g, apply_norm, tm=256):
    T, D = x1.shape
    return pl.pallas_call(
        functools.partial(_final_kernel, apply_norm=apply_norm),
        out_shape=jax.ShapeDtypeStruct((T, D), f32),
        grid=(T // tm,),
        in_specs=[pl.BlockSpec((tm, D), lambda i: (i, 0)),
                  pl.BlockSpec((TOP_K, tm, D), lambda i: (0, i, 0)),
                  pl.BlockSpec((tm, TOP_K), lambda i: (i, 0)),
                  pl.BlockSpec((1, D), lambda i: (0, 0))],
        out_specs=pl.BlockSpec((tm, D), lambda i: (i, 0)),
        compiler_params=pltpu.CompilerParams(
            dimension_semantics=("arbitrary",), vmem_limit_bytes=VMEM_LIMIT),
        name="final",
    )(x1, y4, gates, g)


def _pad_cols(a, n):
    return jnp.pad(a, ((0, 0), (0, n - a.shape[1])))


def _pad_rows(a, n):
    return jnp.pad(a, ((0, n - a.shape[0]), (0, 0)))


def _regroup_cols(a):
    W = RWKV_WIDTH
    o = 3 * W
    return jnp.concatenate([
        a[:, :o],
        _pad_cols(a[:, o:o + DECAY_LORA], 128),
        _pad_cols(a[:, o + DECAY_LORA:o + DECAY_LORA + ICLR_LORA], 128),
        _pad_cols(a[:, o + DECAY_LORA + ICLR_LORA:o + DECAY_LORA + ICLR_LORA + GATE_LORA], 256),
        a[:, o + DECAY_LORA + ICLR_LORA + GATE_LORA:],
    ], axis=1)


def _rope_tables(positions):
    half = ROPE_DIM // 2
    inv_freq = ROPE_THETA ** (-jnp.arange(half, dtype=f32) / half)
    ang = positions.astype(f32)[..., None] * inv_freq
    cos, sin = jnp.cos(ang), jnp.sin(ang)
    B, S = positions.shape
    cosf = jnp.concatenate([cos, cos, jnp.ones((B, S, ATTN_E - ROPE_DIM), f32)], axis=-1)
    sinf = jnp.concatenate([-sin, sin, jnp.zeros((B, S, ATTN_E - ROPE_DIM), f32)], axis=-1)
    return cosf, sinf


def _route(logits, tm, n_tiles):
    T, E = logits.shape
    n = T * TOP_K
    i32 = jnp.int32
    top_val, top_idx = lax.top_k(logits, TOP_K)
    gates = jax.nn.softmax(top_val, axis=-1)
    flat_e = top_idx.reshape(-1).astype(i32)
    iota = jnp.arange(n, dtype=i32)
    experts = jnp.arange(E, dtype=i32)

    def lookup(table, idx):
        return jnp.sum(jnp.where(idx[:, None] == experts[None, :], table[None, :], 0), axis=1)

    _, order = lax.sort((flat_e, iota), num_keys=1, is_stable=True)
    _, rank = lax.sort((order, iota), num_keys=1)
    sizes = jnp.sum((flat_e[:, None] == experts[None, :]).astype(i32), axis=0)
    padded = ((sizes + tm - 1) // tm) * tm
    pad_end = jnp.cumsum(padded)
    pad_start = pad_end - padded
    start = jnp.cumsum(sizes) - sizes
    pos = lookup(pad_start - start, flat_e) + rank
    tile_start = jnp.arange(n_tiles, dtype=i32) * tm
    tile_e = jnp.minimum(jnp.sum((pad_end[None, :] <= tile_start[:, None]).astype(i32), axis=1), E - 1)
    n_valid = (pad_end[-1] // tm).astype(i32).reshape(1)
    row_e = jnp.repeat(tile_e, tm)
    rows = jnp.arange(n_tiles * tm, dtype=i32)
    off = rows - lookup(pad_start, row_e)
    src = jnp.clip(lookup(start, row_e) + off, 0, n - 1)
    tok = jnp.where(off < lookup(sizes, row_e), order[src] // TOP_K, rows % T)
    pos_slot_major = pos.reshape(T, TOP_K).T.reshape(-1)
    return tok, gates, pos_slot_major, tile_e, n_valid, pad_start // tm, padded // tm


def kernel(x, positions, norm_mix, w_in, shift_mu, w0, w2_decay, a0, a2_iclr, g2_gate, k_k, k_a,
           r_k, ln_x_w, ln_x_b, w_o_rwkv, w_o_attn, w_out, norm_ffn, w_router, b_router,
           w1, b1, w2, b2, norm_final):
    B, S, D = x.shape
    T = B * S
    E = w_router.shape[-1]
    depth = norm_mix.shape[0]
    xt = x.reshape(T, D)
    cosf, sinf = _rope_tables(positions)
    moe_tm = 512
    n_tiles = (T * TOP_K) // moe_tm + E
    for layer in range(depth):
        wp = _regroup_cols(w_in[layer]).astype(bf16)
        mu = _regroup_cols(shift_mu[layer][None, :])
        proj = _proj(xt, norm_mix[layer][None, :], wp)
        proj3 = proj.reshape(B, S, -1)
        rw = _rwkv(proj3, mu, w0[layer][None, :], _stack_rhs(_pad_rows(w2_decay[layer], 128)),
                   a0[layer][None, :], _stack_rhs(_pad_rows(a2_iclr[layer], 128)),
                   _pad_rows(g2_gate[layer], 256).astype(bf16), k_k[layer][None, :],
                   k_a[layer][None, :],
                   r_k[layer].reshape(1, -1), ln_x_w[layer][None, :], ln_x_b[layer][None, :])
        outs, lses = [], []
        for gi, (window, dil) in enumerate(ATTN_GROUPS):
            o_g, l_g = _attn_group(proj3, cosf, sinf, gi, dil)
            outs.append(o_g)
            lses.append(l_g)
        x1, h2, logits = _post(
            xt, rw.reshape(T, -1), outs, lses, proj,
            w_o_rwkv[layer].astype(bf16), w_o_attn[layer].astype(bf16), w_out[layer].astype(bf16),
            norm_ffn[layer][None, :], _pad_cols(w_router[layer], 128),
            _pad_cols(b_router[layer][None, :], 128))
        tok, gates, pos, tile_e, n_valid, first_tile, n_tiles_e = _route(logits[:, :E], moe_tm, n_tiles)
        xs = h2[tok]
        ys = _moe(tile_e, n_valid, first_tile, n_tiles_e, xs, w1[layer], b1[layer][:, None, :],
                  w2[layer], b2[layer][:, None, :], moe_tm)
        y4 = ys[pos].reshape(TOP_K, T, D)
        xt = _final(x1, y4, gates, norm_final[None, :], apply_norm=layer + 1 == depth)
    return xt.reshape(B, S, D)
```

```python
import functools

import jax
import jax.numpy as jnp
from jax import lax
from jax.experimental import pallas as pl
from jax.experimental.pallas import tpu as pltpu

f32 = jnp.float32
bf16 = jnp.bfloat16
HIGHEST = lax.Precision.HIGHEST

NORM_EPS = 1e-5
RWKV_GN_EPS = 64e-5
ROPE_THETA = 500000.0
SWIGLU_LIMIT = 7.0
SWIGLU_ALPHA = 1.702

RWKV_HEADS = 16
HEAD_N = 64
RWKV_WIDTH = RWKV_HEADS * HEAD_N
DECAY_LORA = 64
ICLR_LORA = 64
GATE_LORA = 160
ATTN_GROUPS = ((128, 1), (512, 4), (2048, 16))
HEADS_PER_GROUP = 4
ATTN_E = 128
ATTN_GW = HEADS_PER_GROUP * ATTN_E
ATTN_WIDTH = len(ATTN_GROUPS) * ATTN_GW
ROPE_DIM = ATTN_E // 4
TOP_K = 4

LORA_PAD = 512
ZW_OFF, ZA_OFF, ZG_OFF = 3 * RWKV_WIDTH, 3 * RWKV_WIDTH + 128, 3 * RWKV_WIDTH + 256
SHIFT_PAD = 3 * RWKV_WIDTH + LORA_PAD
ATTN_OFF = SHIFT_PAD
GATE_OFF = ATTN_OFF + 3 * ATTN_WIDTH

RWKV_CHUNK = 64
_RWKV_NWIDE = 12
ATTN_BLK = 128
_ATTN_UNITS = 16
VMEM_LIMIT = 56 * 1024 * 1024


def _dot(a, b, dims=(((1,), (0,)), ((), ())), precision=None):
    return lax.dot_general(a, b, dims, precision=precision, preferred_element_type=f32)


_NT = (((1,), (1,)), ((), ()))
_TN = (((0,), (0,)), ((), ()))


def _split2(a):
    hi = a.astype(bf16)
    return hi, (a - hi.astype(f32)).astype(bf16)


def _split3(a):
    hi = a.astype(bf16)
    rem = a - hi.astype(f32)
    mid = rem.astype(bf16)
    return hi, mid, (rem - mid.astype(f32)).astype(bf16)


def _stack_lhs(a):
    hi, lo = _split2(a)
    return jnp.concatenate([hi, lo, hi], axis=1)


def _stack_rhs(w):
    hi, lo = _split2(w)
    return jnp.concatenate([hi, hi, lo], axis=0)


def _proj_kernel(x_ref, g_ref, w_ref, o_ref, h_ref):
    @pl.when(pl.program_id(1) == 0)
    def _():
        x = x_ref[...]
        ms = jnp.mean(x * x, axis=-1, keepdims=True)
        h_ref[...] = (x * lax.rsqrt(ms + NORM_EPS) * g_ref[...]).astype(bf16)

    o_ref[...] = _dot(h_ref[...], w_ref[...])


def _proj(x2, g, wp, tm=512, tn=2048):
    T, D = x2.shape
    NP = wp.shape[1]
    return pl.pallas_call(
        _proj_kernel,
        out_shape=jax.ShapeDtypeStruct((T, NP), f32),
        grid=(T // tm, NP // tn),
        in_specs=[pl.BlockSpec((tm, D), lambda i, j: (i, 0)),
                  pl.BlockSpec((1, D), lambda i, j: (0, 0)),
                  pl.BlockSpec((D, tn), lambda i, j: (0, j))],
        out_specs=pl.BlockSpec((tm, tn), lambda i, j: (i, j)),
        scratch_shapes=[pltpu.VMEM((tm, D), bf16)],
        compiler_params=pltpu.CompilerParams(
            dimension_semantics=("arbitrary", "arbitrary"), vmem_limit_bytes=VMEM_LIMIT),
        name="proj",
    )(x2, g, wp)


def _rwkv_kernel(p_ref, mu_ref, w0_ref, w2_ref, a0_ref, a2_ref, g2_ref, kk_ref, ka_ref,
                 rk_ref, lnw_ref, lnb_ref, o_ref, state_ref, prev_ref, dec_ref, wide_ref):
    (at0_ref, at_ref, rt0_ref, rt_ref, bt_ref, kt_ref, bh_ref, kh_ref, v_ref, rkb_ref, g_ref,
     kkr_ref) = [wide_ref.at[i] for i in range(_RWKV_NWIDE)]
    L = RWKV_CHUNK
    W = RWKV_WIDTH
    c_idx = pl.program_id(1)

    @pl.when(c_idx == 0)
    def _():
        state_ref[...] = jnp.zeros_like(state_ref)
        prev_ref[...] = jnp.zeros_like(prev_ref)

    p = p_ref[0]
    row = lax.broadcasted_iota(jnp.int32, (L, 1), 0)
    shifted = jnp.where(row == 0, prev_ref[...], pltpu.roll(p, 1, axis=0))
    prev_ref[...] = p[L - 1:L, :]
    z = p + (shifted - p) * mu_ref[...]

    r = z[:, 0:W]
    k = z[:, W:2 * W]
    v = z[:, 2 * W:3 * W]
    zw = z[:, ZW_OFF:ZW_OFF + 128]
    za = z[:, ZA_OFF:ZA_OFF + 128]
    zg = z[:, ZG_OFF:ZG_OFF + 256]

    w_raw = w0_ref[...] + _dot(_stack_lhs(jnp.tanh(zw)), w2_ref[...])
    sp = jnp.maximum(-w_raw, 0.0) + jnp.log(1.0 + jnp.exp(-jnp.abs(w_raw)))
    lw = -jnp.exp(-sp - 0.5)
    a = jax.nn.sigmoid(a0_ref[...] + _dot(_stack_lhs(za), a2_ref[...]))
    g_ref[...] = _dot(jax.nn.sigmoid(zg).astype(bf16), g2_ref[...])

    ti = lax.broadcasted_iota(jnp.int32, (L, L), 0)
    si = lax.broadcasted_iota(jnp.int32, (L, L), 1)
    tril_incl = ti >= si
    tril_strict = ti > si
    tri = tril_incl.astype(bf16)
    lw_h, lw_m, lw_l = _split3(lw)
    c = _dot(jnp.concatenate([tri, tri, tri], axis=1), jnp.concatenate([lw_h, lw_m, lw_l], axis=0))
    cex = c - lw
    cm = c[L // 2 - 1:L // 2, :]
    cl = c[L - 1:L, :]
    e_m = jnp.exp(-cm)
    e_c = jnp.exp(c)
    e_ex = jnp.exp(cex)
    e_inv = jnp.exp(cm - c)
    e_tail = jnp.exp(cl - c)

    kkv = k * kk_ref[...]
    kmod = k * (1.0 + (a - 1.0) * ka_ref[...])
    at0_ref[...] = e_ex
    rt0 = r * e_c
    rt0_ref[...] = rt0
    rt_ref[...] = rt0 * e_m
    kt_ref[...] = kmod * e_inv
    kh_ref[...] = kmod * e_tail
    bt_ref[...] = a * e_inv
    bh_ref[...] = a * e_tail
    at_ref[...] = e_ex * e_m
    v_ref[...] = v
    dec_ref[...] = jnp.exp(cl)
    rkb_ref[...] = r * kmod * rk_ref[...]
    kkr_ref[...] = kkv

    PW = 2 * HEAD_N
    pairs = range(RWKV_HEADS // 2)
    lane = lax.broadcasted_iota(jnp.int32, (L, PW), 1)
    trow = lax.broadcasted_iota(jnp.int32, (L, PW), 0)
    first = lane < HEAD_N
    scol = jnp.where(first, lane, lane - HEAD_N)
    strict_p = trow > scol
    incl_p = trow >= scol
    eye_p = (trow == scol).astype(f32)
    rb =lax.broadcasted_iota(jnp.int32, (PW, PW), 0)
    lb = lax.broadcasted_iota(jnp.int32, (PW, PW), 1)
    bd_mask = (rb < HEAD_N) == (lb < HEAD_N)

    def bd(x):
        zero = jnp.zeros_like(x)
        return jnp.concatenate([jnp.where(first, x, zero), jnp.where(first, zero, x)], axis=0)

    def segsum(x):
        s0 = jnp.sum(jnp.where(first, x, 0.0), axis=-1, keepdims=True)
        s1 = jnp.sum(jnp.where(first, 0.0, x), axis=-1, keepdims=True)
        return jnp.where(first, s0, s1)

    ps = [slice(p * PW, (p + 1) * PW) for p in pairs]
    kkr = [kkr_ref[:, s] for s in ps]
    ssq = [segsum(x * x) for x in kkr]
    kkh = [x / jnp.maximum(jnp.sqrt(q), 1e-12) for x, q in zip(kkr, ssq)]
    at0 = [(-kh_ * at0_ref[:, s]).astype(bf16) for kh_, s in zip(kkh, ps)]
    at = [(-kh_ * at_ref[:, s]).astype(bf16) for kh_, s in zip(kkh, ps)]
    bt = [(kh_ * bt_ref[:, s]).astype(bf16) for kh_, s in zip(kkh, ps)]
    bh = [(kh_ * bh_ref[:, s]).astype(bf16) for kh_, s in zip(kkh, ps)]
    vb = [v_ref[:, s].astype(bf16) for s in ps]

    lhs = [jnp.concatenate([a_, rt_ref[:, s].astype(bf16)], axis=0) for a_, s in zip(at, ps)]
    sb = [_dot(l_, bd(b_), _NT) for l_, b_ in zip(lhs, bt)]
    sk = [_dot(l_, bd(kt_ref[:, s].astype(bf16)), _NT) for l_, s in zip(lhs, ps)]
    a_ab = [jnp.where(strict_p, x[:L], 0.0) for x in sb]
    a_rb = [jnp.where(incl_p, x[L:], 0.0).astype(bf16) for x in sb]
    a_ak = [jnp.where(strict_p, x[:L], 0.0).astype(bf16) for x in sk]
    a_rk = [jnp.where(incl_p, x[L:], 0.0).astype(bf16) for x in sk]

    tinv = [eye_p + x for x in a_ab]
    xp = a_ab
    n = 2
    while n < L:
        xb = [x.astype(bf16) for x in xp]
        xp = [_dot(x, bd(x)) for x in xb]
        tinv = [t + _dot(t.astype(bf16), bd(x.astype(bf16))) for t, x in zip(tinv, xp)]
        n *= 2
    tb = [t.astype(bf16) for t in tinv]

    akv = [_dot(a_, bd(v_)).astype(bf16) for a_, v_ in zip(a_ak, vb)]
    w12 = [_dot(t, jnp.concatenate([bd(a_), bd(k_)], axis=1)) for t, a_, k_ in zip(tb, at0, akv)]

    s0 = [state_ref[p] for p in pairs]
    s0b = [s.astype(bf16) for s in s0]
    uy = [_dot(jnp.concatenate([w[:, :PW].astype(bf16), rt0_ref[:, s].astype(bf16)], axis=0), sb_, _NT)
          for w, s, sb_ in zip(w12, ps, s0b)]
    u = [x[:L] + w[:, PW:] for x, w in zip(uy, w12)]
    ub = [x.astype(bf16) for x in u]
    y = [x[L:] + _dot(jnp.concatenate([rb_, rk_], axis=1), jnp.concatenate([bd(u_), bd(v_)], axis=0))
         for x, rb_, rk_, u_, v_ in zip(uy, a_rb, a_rk, ub, vb)]
    for p in pairs:
        upd = _dot(jnp.concatenate([ub[p], vb[p]], axis=0),
                   jnp.concatenate([bh[p], kh_ref[:, ps[p]].astype(bf16)], axis=0), _TN)
        state_ref[p] = s0[p] * dec_ref[:, ps[p]] + jnp.where(bd_mask, upd, 0.0)

    inv_n = 1.0 / HEAD_N
    mean = [segsum(x) * inv_n for x in y]
    yc = [x - m for x, m in zip(y, mean)]
    var = [segsum(x * x) * inv_n for x in yc]
    bonus = [segsum(rkb_ref[:, s]) for s in ps]
    for p in pairs:
        s = ps[p]
        yn = yc[p] * lax.rsqrt(var[p] + RWKV_GN_EPS) * lnw_ref[:, s] + lnb_ref[:, s]
        o_ref[0, :, s] = ((yn + bonus[p] * v_ref[:, s]) * g_ref[:, s]).astype(o_ref.dtype)


def _rwkv(proj3, mu, w0, w2d, a0, a2, g2, k_k, k_a, r_k, ln_w, ln_b):
    B, S, NP = proj3.shape
    L, W = RWKV_CHUNK, RWKV_WIDTH
    row = lambda n: pl.BlockSpec((1, n), lambda b, c: (0, 0))
    full = lambda a: pl.BlockSpec(a.shape, lambda b, c: (0, 0))
    return pl.pallas_call(
        _rwkv_kernel,
        out_shape=jax.ShapeDtypeStruct((B, S, W), bf16),
        grid=(B, S // L),
        in_specs=[pl.BlockSpec((1, L, SHIFT_PAD), lambda b, c: (b, c, 0)),
                  row(SHIFT_PAD), row(W), full(w2d), row(W), full(a2), full(g2),
                  row(W), row(W), row(W), row(W), row(W)],
        out_specs=pl.BlockSpec((1, L, W), lambda b, c: (b, c, 0)),
        scratch_shapes=[pltpu.VMEM((RWKV_HEADS // 2, 2 * HEAD_N, 2 * HEAD_N), f32),
                        pltpu.VMEM((1, SHIFT_PAD), f32), pltpu.VMEM((1, W), f32),
                        pltpu.VMEM((_RWKV_NWIDE, L, W), f32)],
        compiler_params=pltpu.CompilerParams(
            dimension_semantics=("arbitrary", "arbitrary"), vmem_limit_bytes=VMEM_LIMIT),
        name="rwkv",
    )(proj3, mu, w0, w2d, a0, a2, g2, k_k, k_a, r_k, ln_w, ln_b)


def _rope(x, cosf, sinf, lane):
    rot = jnp.where(lane < ROPE_DIM // 2, pltpu.roll(x, ATTN_E - ROPE_DIM // 2, axis=1),
                    pltpu.roll(x, ROPE_DIM // 2, axis=1))
    return x * cosf + rot * sinf


def _attn_kernel(q_ref, kc_ref, kp_ref, vc_ref, vp_ref, cc_ref, sc_ref, cp_ref, sp_ref,
                 o_ref, l_ref, *, dil, hps, nblk, wave):
    n = pl.program_id(1)
    blk = ATTN_BLK
    span = blk * dil
    qi = lax.broadcasted_iota(jnp.int32, (blk, blk), 0)
    kj = lax.broadcasted_iota(jnp.int32, (blk, blk), 1)
    lane = lax.broadcasted_iota(jnp.int32, (blk, ATTN_E), 1)
    cur_ok = kj <= qi
    prev_ok = kj >= qi
    first_ok = jnp.logical_and(prev_ok, n > 0)
    scale = ATTN_E ** -0.5

    def rows(b, r):
        return pl.ds(b * span + r, blk, stride=dil) if dil > 1 else pl.ds(b * span, blk)

    def cols(hh):
        return slice(hh * ATTN_E, (hh + 1) * ATTN_E)

    all_units = [(b, r, hh) for b in range(nblk) for r in range(dil) for hh in range(hps)]
    kc, vc = {}, {}
    for w0 in range(0, len(all_units), wave):
        units = all_units[w0:w0 + wave]
        tab = {u[:2]: (cc_ref[0, rows(u[0], u[1]), :], sc_ref[0, rows(u[0], u[1]), :]) for u in units}
        q = {u: _rope(q_ref[0, rows(u[0], u[1]), cols(u[2])], *tab[u[:2]], lane).astype(bf16)
             for u in units}
        for u in units:
            kc[u] = _rope(kc_ref[0, rows(u[0], u[1]), cols(u[2])], *tab[u[:2]], lane).astype(bf16)
            vc[u] = vc_ref[0, rows(u[0], u[1]), cols(u[2])].astype(bf16)
        kp, vp = {}, {}
        for (b, r, hh) in units:
            if b == 0:
                ptab = (cp_ref[0, rows(0, r), :], sp_ref[0, rows(0, r), :])
                kp[(b, r, hh)] = _rope(kp_ref[0, rows(0, r), cols(hh)], *ptab, lane).astype(bf16)
                vp[(b, r, hh)] = vp_ref[0, rows(0, r), cols(hh)].astype(bf16)
            else:
                kp[(b, r, hh)] = kc[(b - 1, r, hh)]
                vp[(b, r, hh)] = vc[(b - 1, r, hh)]
        s_c = {u: jnp.where(cur_ok, _dot(q[u], kc[u], _NT) * scale, -jnp.inf) for u in units}
        s_p = {u: jnp.where(first_ok if u[0] == 0 else prev_ok, _dot(q[u], kp[u], _NT) * scale, -jnp.inf)
               for u in units}
        m = {u: jnp.maximum(jnp.max(s_c[u], axis=-1, keepdims=True),
                            jnp.max(s_p[u], axis=-1, keepdims=True)) for u in units}
        p_c = {u: jnp.exp(s_c[u] - m[u]) for u in units}
        p_p = {u: jnp.exp(s_p[u] - m[u]) for u in units}
        l = {u: jnp.sum(p_c[u], axis=-1, keepdims=True) + jnp.sum(p_p[u], axis=-1, keepdims=True)
             for u in units}
        acc = {u: _dot(p_c[u].astype(bf16), vc[u]) + _dot(p_p[u].astype(bf16), vp[u]) for u in units}
        for u in units:
            o_ref[0, rows(u[0], u[1]), cols(u[2])] = acc[u] / l[u]
            l_ref[0, rows(u[0], u[1]), cols(u[2])] = jnp.broadcast_to(m[u] + jnp.log(l[u]), (blk, ATTN_E))


def _attn_group(proj3, cosf, sinf, gi, dil):
    B, S, NP = proj3.shape
    span = ATTN_BLK * dil
    hps = HEADS_PER_GROUP if dil == 1 else 1
    nblk = _ATTN_UNITS // (dil * hps)
    R = span * nblk
    cw = hps * ATTN_E
    per = ATTN_GW // cw
    qo = (ATTN_OFF + gi * ATTN_GW) // cw
    ko = qo + ATTN_WIDTH // cw
    vo = ko + ATTN_WIDTH // cw

    def cur(off):
        return pl.BlockSpec((1, R, cw), lambda b, n, h: (b, n, off + h))

    def prev(off):
        return pl.BlockSpec((1, span, cw), lambda b, n, h: (b, jnp.maximum(n * nblk - 1, 0), off + h))

    tc = pl.BlockSpec((1, R, ATTN_E), lambda b, n, h: (b, n, 0))
    tp = pl.BlockSpec((1, span, ATTN_E), lambda b, n, h: (b, jnp.maximum(n * nblk - 1, 0), 0))
    out = pl.BlockSpec((1, R, cw), lambda b, n, h: (b, n, h))
    o, l = pl.pallas_call(
        functools.partial(_attn_kernel, dil=dil, hps=hps, nblk=nblk,
                          wave=_ATTN_UNITS // 2 if dil > HEADS_PER_GROUP else _ATTN_UNITS),
        out_shape=[jax.ShapeDtypeStruct((B, S, ATTN_GW), f32)] * 2,
        grid=(B, S // R, per),
        in_specs=[cur(qo), cur(ko), prev(ko), cur(vo), prev(vo), tc, tc, tp, tp],
        out_specs=[out, out],
        compiler_params=pltpu.CompilerParams(
            dimension_semantics=("arbitrary", "arbitrary", "arbitrary"),
            vmem_limit_bytes=VMEM_LIMIT),
        name=f"attn_g{gi}",
    )(proj3, proj3, proj3, proj3, proj3, cosf, sinf, cosf, sinf)
    return o.reshape(B * S, ATTN_GW), l.reshape(B * S, ATTN_GW)


def _merge_kernel(rw_ref, o0_ref, o1_ref, o2_ref, l0_ref, l1_ref, l2_ref, ga_ref, gb_ref,
                  wor_ref, woa_ref, mg_ref):
    l0, l1, l2 = l0_ref[...], l1_ref[...], l2_ref[...]
    m = jnp.maximum(jnp.maximum(l0, l1), l2)
    e0, e1, e2 = jnp.exp(l0 - m), jnp.exp(l1 - m), jnp.exp(l2 - m)
    o = (e0 * o0_ref[...] + e1 * o1_ref[...] + e2 * o2_ref[...]) / (e0 + e1 + e2)
    y_b = _dot(o.astype(bf16), woa_ref[...])
    y_a = _dot(rw_ref[...], wor_ref[...])
    merged = jax.nn.sigmoid(ga_ref[...]) * y_a + jax.nn.sigmoid(gb_ref[...]) * y_b
    mg_ref[...] = merged.astype(bf16)


def _out_kernel(x_ref, mg_ref, wout_ref, gn_ref, wr_ref, br_ref, x1_ref, h2_ref, lg_ref):
    x1 = x_ref[...] + _dot(mg_ref[...], wout_ref[...])
    x1_ref[...] = x1
    ms = jnp.mean(x1 * x1, axis=-1, keepdims=True)
    h2 = x1 * lax.rsqrt(ms + NORM_EPS) * gn_ref[...]
    h2_ref[...] = h2.astype(bf16)
    lg_ref[...] = _dot(_stack_lhs(h2), wr_ref[...]) + br_ref[...]


def _post(x2, rw, outs, lses, proj, wor, woa, wout, gn, wr, br, tm=512):
    T, D = x2.shape
    row = lambda w: pl.BlockSpec((tm, w), lambda i: (i, 0))
    const = lambda a: pl.BlockSpec(a.shape, lambda i: (0, 0), pipeline_mode=pl.Buffered(1))
    ga_blk = GATE_OFF // D
    params = pltpu.CompilerParams(dimension_semantics=("arbitrary",), vmem_limit_bytes=VMEM_LIMIT)
    merged = pl.pallas_call(
        _merge_kernel,
        out_shape=jax.ShapeDtypeStruct((T, D), bf16),
        grid=(T // tm,),
        in_specs=[row(RWKV_WIDTH)] + [row(ATTN_GW)] * 6
                 + [pl.BlockSpec((tm, D), lambda i: (i, ga_blk)),
                    pl.BlockSpec((tm, D), lambda i: (i, ga_blk + 1)),
                    const(wor), const(woa)],
        out_specs=row(D),
        compiler_params=params,
        name="merge",
    )(rw, *outs, *lses, proj, proj, wor, woa)
    return pl.pallas_call(
        _out_kernel,
        out_shape=[jax.ShapeDtypeStruct((T, D), f32), jax.ShapeDtypeStruct((T, D), bf16),
                   jax.ShapeDtypeStruct((T, 128), f32)],
        grid=(T // tm,),
        in_specs=[row(D), row(D), const(wout), const(gn), const(wr), const(br)],
        out_specs=[row(D), row(D), row(128)],
        compiler_params=params,
        name="out",
    )(x2, merged, wout, gn, wr, br)


def _moe_up_kernel(st, sc, se, sf, nv, xs_ref, w1g_ref, w1l_ref, b1g_ref, b1l_ref, o_ref,
                   wg_ref, wl_ref):
    s = pl.program_id(0)

    @pl.when(s < nv[0])
    def _():
        @pl.when(sf[s] == 1)
        def _():
            wg_ref[...] = w1g_ref[...].astype(bf16)
            wl_ref[...] = w1l_ref[...].astype(bf16)

        xs = xs_ref[...]
        hg = _dot(xs, wg_ref[...]) + b1g_ref[...]
        hl = _dot(xs, wl_ref[...]) + b1l_ref[...]
        x_glu = jnp.minimum(hg, SWIGLU_LIMIT)
        x_lin = jnp.clip(hl, -SWIGLU_LIMIT, SWIGLU_LIMIT)
        act = x_glu * jax.nn.sigmoid(SWIGLU_ALPHA * x_glu) * (x_lin + 1.0)
        o_ref[...] = act.astype(o_ref.dtype)


def _moe_down_kernel(st, sc, se, sf, nv, a_ref, w2_ref, b2_ref, o_ref, w_ref):
    s = pl.program_id(0)

    @pl.when(s < nv[0])
    def _():
        @pl.when(sf[s] == 1)
        def _():
            w_ref[...] = w2_ref[...].astype(bf16)

        o_ref[...] = (_dot(a_ref[...], w_ref[...]) + b2_ref[...]).astype(o_ref.dtype)


def _moe_schedule(tile_e, n_valid, first_tile, n_tiles_e, nc):
    i32 = jnp.int32
    nt = tile_e.shape[0]
    E = first_tile.shape[0]
    experts = jnp.arange(E, dtype=i32)
    s = jnp.arange(nt * nc, dtype=i32)
    s = jnp.minimum(s, nc * n_valid[0] - 1)
    e = tile_e[s // nc]
    pick = lambda table: jnp.sum(jnp.where(e[:, None] == experts[None, :], table[None, :], 0), axis=1)
    ft, ne = pick(first_tile), jnp.maximum(pick(n_tiles_e), 1)
    local = s - nc * ft
    col = local // ne
    row = ft + local % ne
    changed = jnp.logical_or(e != jnp.roll(e, 1), col != jnp.roll(col, 1)).at[0].set(True)
    return row.astype(i32), col.astype(i32), e.astype(i32), changed.astype(i32), (nc * n_valid).astype(i32)


def _moe(tile_e, n_valid, first_tile, n_tiles_e, xs, w1, b1, w2, b2, tm, tf=1024, tn=1024):
    P, D = xs.shape
    E, _, F2 = w1.shape
    F = F2 // 2
    nj = F // tf
    nn = D // tn
    nt = P // tm
    params = pltpu.CompilerParams(dimension_semantics=("arbitrary",), vmem_limit_bytes=VMEM_LIMIT)

    sched = _moe_schedule(tile_e, n_valid, first_tile, n_tiles_e, nj)
    act = pl.pallas_call(
        _moe_up_kernel,
        out_shape=jax.ShapeDtypeStruct((P, F), bf16),
        grid_spec=pltpu.PrefetchScalarGridSpec(
            num_scalar_prefetch=5,
            grid=(nt * nj,),
            in_specs=[
                pl.BlockSpec((tm, D), lambda s, st, sc, se, sf, nv: (st[s], 0)),
                pl.BlockSpec((None, D, tf), lambda s, st, sc, se, sf, nv: (se[s], 0, sc[s])),
                pl.BlockSpec((None, D, tf), lambda s, st, sc, se, sf, nv: (se[s], 0, nj + sc[s])),
                pl.BlockSpec((None, 1, tf), lambda s, st, sc, se, sf, nv: (se[s], 0, sc[s])),
                pl.BlockSpec((None, 1, tf), lambda s, st, sc, se, sf, nv: (se[s], 0, nj + sc[s])),
            ],
            out_specs=pl.BlockSpec((tm, tf), lambda s, st, sc, se, sf, nv: (st[s], sc[s])),
            scratch_shapes=[pltpu.VMEM((D, tf), bf16), pltpu.VMEM((D, tf), bf16)],
        ),
        compiler_params=params,
        name="moe_up",
    )(*sched, xs, w1, w1, b1, b1)

    sched = _moe_schedule(tile_e, n_valid, first_tile, n_tiles_e, nn)
    return pl.pallas_call(
        _moe_down_kernel,
        out_shape=jax.ShapeDtypeStruct((P, D), bf16),
        grid_spec=pltpu.PrefetchScalarGridSpec(
            num_scalar_prefetch=5,
            grid=(nt * nn,),
            in_specs=[
                pl.BlockSpec((tm, F), lambda s, st, sc, se, sf, nv: (st[s], 0)),
                pl.BlockSpec((None, F, tn), lambda s, st, sc, se, sf, nv: (se[s], 0, sc[s])),
                pl.BlockSpec((None, 1, tn), lambda s, st, sc, se, sf, nv: (se[s], 0, sc[s])),
            ],
            out_specs=pl.BlockSpec((tm, tn), lambda s, st, sc, se, sf, nv: (st[s], sc[s])),
            scratch_shapes=[pltpu.VMEM((F, tn), bf16)],
        ),
        compiler_params=params,
        name="moe_down",
    )(*sched, act, w2, b2)


def _final_kernel(x1_ref, y_ref, gate_ref, g_ref, o_ref, *, apply_norm):
    gates = gate_ref[...]
    y = gates[:, 0:1] * y_ref[0].astype(f32)
    for kk in range(1, TOP_K):
        y = y + gates[:, kk:kk + 1] * y_ref[kk].astype(f32)
    x2 = x1_ref[...] + y
    if apply_norm:
        ms = jnp.mean(x2 * x2, axis=-1, keepdims=True)
        x2 = x2 * lax.rsqrt(ms + NORM_EPS) * g_ref[...]
    o_ref[...] = x2


def _final(x1, y4, gates, g, apply_norm, tm=256):
    T, D = x1.shape
    return pl.pallas_call(
        functools.partial(_final_kernel, apply_norm=apply_norm),
        out_shape=jax.ShapeDtypeStruct((T, D), f32),
        grid=(T // tm,),
        in_specs=[pl.BlockSpec((tm, D), lambda i: (i, 0)),
                  pl.BlockSpec((TOP_K, tm, D), lambda i: (0, i, 0)),
                  pl.BlockSpec((tm, TOP_K), lambda i: (i, 0)),
                  pl.BlockSpec((1, D), lambda i: (0, 0))],
        out_specs=pl.BlockSpec((tm, D), lambda i: (i, 0)),
        compiler_params=pltpu.CompilerParams(
            dimension_semantics=("arbitrary",), vmem_limit_bytes=VMEM_LIMIT),
        name="final",
    )(x1, y4, gates, g)


def _pad_cols(a, n):
    return jnp.pad(a, ((0, 0), (0, n - a.shape[1])))


def _pad_rows(a, n):
    return jnp.pad(a, ((0, n - a.shape[0]), (0, 0)))


def _regroup_cols(a):
    W = RWKV_WIDTH
    o = 3 * W
    return jnp.concatenate([
        a[:, :o],
        _pad_cols(a[:, o:o + DECAY_LORA], 128),
        _pad_cols(a[:, o + DECAY_LORA:o + DECAY_LORA + ICLR_LORA], 128),
        _pad_cols(a[:, o + DECAY_LORA + ICLR_LORA:o + DECAY_LORA + ICLR_LORA + GATE_LORA], 256),
        a[:, o + DECAY_LORA + ICLR_LORA + GATE_LORA:],
    ], axis=1)


def _rope_tables(positions):
    half = ROPE_DIM // 2
    inv_freq = ROPE_THETA ** (-jnp.arange(half, dtype=f32) / half)
    ang = positions.astype(f32)[..., None] * inv_freq
    cos, sin = jnp.cos(ang), jnp.sin(ang)
    B, S = positions.shape
    cosf = jnp.concatenate([cos, cos, jnp.ones((B, S, ATTN_E - ROPE_DIM), f32)], axis=-1)
    sinf = jnp.concatenate([-sin, sin, jnp.zeros((B, S, ATTN_E - ROPE_DIM), f32)], axis=-1)
    return cosf, sinf


def _route(logits, tm, n_tiles):
    T, E = logits.shape
    n = T * TOP_K
    i32 = jnp.int32
    top_val, top_idx = lax.top_k(logits, TOP_K)
    gates = jax.nn.softmax(top_val, axis=-1)
    flat_e = top_idx.reshape(-1).astype(i32)
    iota = jnp.arange(n, dtype=i32)
    experts = jnp.arange(E, dtype=i32)

    def lookup(table, idx):
        return jnp.sum(jnp.where(idx[:, None] == experts[None, :], table[None, :], 0), axis=1)

    _, order = lax.sort((flat_e, iota), num_keys=1, is_stable=True)
    _, rank = lax.sort((order, iota), num_keys=1)
    sizes = jnp.sum((flat_e[:, None] == experts[None, :]).astype(i32), axis=0)
    padded = ((sizes + tm - 1) // tm) * tm
    pad_end = jnp.cumsum(padded)
    pad_start = pad_end - padded
    start = jnp.cumsum(sizes) - sizes
    pos = lookup(pad_start - start, flat_e) + rank
    tile_start = jnp.arange(n_tiles, dtype=i32) * tm
    tile_e = jnp.minimum(jnp.sum((pad_end[None, :] <= tile_start[:, None]).astype(i32), axis=1), E - 1)
    n_valid = (pad_end[-1] // tm).astype(i32).reshape(1)
    row_e = jnp.repeat(tile_e, tm)
    rows = jnp.arange(n_tiles * tm, dtype=i32)
    off = rows - lookup(pad_start, row_e)
    src = jnp.clip(lookup(start, row_e) + off, 0, n - 1)
    tok = jnp.where(off < lookup(sizes, row_e), order[src] // TOP_K, rows % T)
    pos_slot_major = pos.reshape(T, TOP_K).T.reshape(-1)
    return tok, gates, pos_slot_major, tile_e, n_valid, pad_start // tm, padded // tm


def kernel(x, positions, norm_mix, w_in, shift_mu, w0, w2_decay, a0, a2_iclr, g2_gate, k_k, k_a,
           r_k, ln_x_w, ln_x_b, w_o_rwkv, w_o_attn, w_out, norm_ffn, w_router, b_router,
           w1, b1, w2, b2, norm_final):
    B, S, D = x.shape
    T = B * S
    E = w_router.shape[-1]
    depth = norm_mix.shape[0]
    xt = x.reshape(T, D)
    cosf, sinf = _rope_tables(positions)
    moe_tm = 512
    n_tiles = (T * TOP_K) // moe_tm + E
    for layer in range(depth):
        wp = _regroup_cols(w_in[layer]).astype(bf16)
        mu = _regroup_cols(shift_mu[layer][None, :])
        proj = _proj(xt, norm_mix[layer][None, :], wp)
        proj3 = proj.reshape(B, S, -1)
        rw = _rwkv(proj3, mu, w0[layer][None, :], _stack_rhs(_pad_rows(w2_decay[layer], 128)),
                   a0[layer][None, :], _stack_rhs(_pad_rows(a2_iclr[layer], 128)),
                   _pad_rows(g2_gate[layer], 256).astype(bf16), k_k[layer][None, :],
                   k_a[layer][None, :],
                   r_k[layer].reshape(1, -1), ln_x_w[layer][None, :], ln_x_b[layer][None, :])
        outs, lses = [], []
        for gi, (window, dil) in enumerate(ATTN_GROUPS):
            o_g, l_g = _attn_group(proj3, cosf, sinf, gi, dil)
            outs.append(o_g)
            lses.append(l_g)
        x1, h2, logits = _post(
            xt, rw.reshape(T, -1), outs, lses, proj,
            w_o_rwkv[layer].astype(bf16), w_o_attn[layer].astype(bf16), w_out[layer].astype(bf16),
            norm_ffn[layer][None, :], _stack_rhs(_pad_cols(w_router[layer], 128)),
            _pad_cols(b_router[layer][None, :], 128))
        tok, gates, pos, tile_e, n_valid, first_tile, n_tiles_e = _route(logits[:, :E], moe_tm, n_tiles)
        xs = h2[tok]
        ys = _moe(tile_e, n_valid, first_tile, n_tiles_e, xs, w1[layer], b1[layer][:, None, :],
                  w2[layer], b2[layer][:, None, :], moe_tm)
        y4 = ys[pos].reshape(TOP_K, T, D)
        xt = _final(x1, y4, gates, norm_final[None, :], apply_norm=layer + 1 == depth)
    return xt.reshape(B, S, D)
```

```python
import functools

import jax
import jax.numpy as jnp
from jax import lax
from jax.experimental import pallas as pl
from jax.experimental.pallas import tpu as pltpu

f32 = jnp.float32
bf16 = jnp.bfloat16
HIGHEST = lax.Precision.HIGHEST

NORM_EPS = 1e-5
RWKV_GN_EPS = 64e-5
ROPE_THETA = 500000.0
SWIGLU_LIMIT = 7.0
SWIGLU_ALPHA = 1.702

RWKV_HEADS = 16
HEAD_N = 64
RWKV_WIDTH = RWKV_HEADS * HEAD_N
DECAY_LORA = 64
ICLR_LORA = 64
GATE_LORA = 160
ATTN_GROUPS = ((128, 1), (512, 4), (2048, 16))
HEADS_PER_GROUP = 4
ATTN_E = 128
ATTN_GW = HEADS_PER_GROUP * ATTN_E
ATTN_WIDTH = len(ATTN_GROUPS) * ATTN_GW
ROPE_DIM = ATTN_E // 4
TOP_K = 4

LORA_PAD = 512
ZW_OFF, ZA_OFF, ZG_OFF = 3 * RWKV_WIDTH, 3 * RWKV_WIDTH + 128, 3 * RWKV_WIDTH + 256
SHIFT_PAD = 3 * RWKV_WIDTH + LORA_PAD
ATTN_OFF = SHIFT_PAD
GATE_OFF = ATTN_OFF + 3 * ATTN_WIDTH

RWKV_CHUNK = 64
_MOE_CHUNKS = 4
_RWKV_SEQS = 2
_RWKV_NWIDE = 12
ATTN_BLK = 128
_ATTN_UNITS = 16
VMEM_LIMIT = 56 * 1024 * 1024


def _dot(a, b, dims=(((1,), (0,)), ((), ())), precision=None):
    return lax.dot_general(a, b, dims, precision=precision, preferred_element_type=f32)


_NT = (((1,), (1,)), ((), ()))
_TN = (((0,), (0,)), ((), ()))


def _split2(a):
    hi = a.astype(bf16)
    return hi, (a - hi.astype(f32)).astype(bf16)


def _split3(a):
    hi = a.astype(bf16)
    rem = a - hi.astype(f32)
    mid = rem.astype(bf16)
    return hi, mid, (rem - mid.astype(f32)).astype(bf16)


def _stack_lhs(a):
    hi, lo = _split2(a)
    return jnp.concatenate([hi, lo, hi], axis=1)


def _stack_rhs(w):
    hi, lo = _split2(w)
    return jnp.concatenate([hi, hi, lo], axis=0)


def _proj_kernel(x_ref, g_ref, w_ref, o_ref, h_ref):
    @pl.when(pl.program_id(1) == 0)
    def _():
        x = x_ref[...]
        ms = jnp.mean(x * x, axis=-1, keepdims=True)
        h_ref[...] = (x * lax.rsqrt(ms + NORM_EPS) * g_ref[...]).astype(bf16)

    o_ref[...] = _dot(h_ref[...], w_ref[...])


def _proj(x2, g, wp, tm=512, tn=2048):
    T, D = x2.shape
    NP = wp.shape[1]
    return pl.pallas_call(
        _proj_kernel,
        out_shape=jax.ShapeDtypeStruct((T, NP), f32),
        grid=(T // tm, NP // tn),
        in_specs=[pl.BlockSpec((tm, D), lambda i, j: (i, 0)),
                  pl.BlockSpec((1, D), lambda i, j: (0, 0)),
                  pl.BlockSpec((D, tn), lambda i, j: (0, j))],
        out_specs=pl.BlockSpec((tm, tn), lambda i, j: (i, j)),
        scratch_shapes=[pltpu.VMEM((tm, D), bf16)],
        compiler_params=pltpu.CompilerParams(
            dimension_semantics=("arbitrary", "arbitrary"), vmem_limit_bytes=VMEM_LIMIT),
        name="proj",
    )(x2, g, wp)


def _rwkv_kernel(p_ref, mu_ref, w0_ref, w2_ref, a0_ref, a2_ref, g2_ref, kk_ref, ka_ref,
                 rk_ref, lnw_ref, lnb_ref, o_ref, state_ref, prev_ref, dec_ref, wide_ref):
    (at0_ref, at_ref, rt0_ref, rt_ref, bt_ref, kt_ref, bh_ref, kh_ref, v_ref, rkb_ref, g_ref,
     kkr_ref) = [wide_ref.at[i] for i in range(_RWKV_NWIDE)]
    L = RWKV_CHUNK
    W = RWKV_WIDTH
    c_idx = pl.program_id(1)

    @pl.when(c_idx == 0)
    def _():
        state_ref[...] = jnp.zeros_like(state_ref)
        prev_ref[...] = jnp.zeros_like(prev_ref)

    RB = p_ref.shape[0]
    RL = RB * L
    p = p_ref[...].reshape(RL, p_ref.shape[2])
    row = lax.broadcasted_iota(jnp.int32, (RL, 1), 0)
    shifted = pltpu.roll(p, 1, axis=0)
    for bi in range(RB):
        shifted = jnp.where(row == bi * L, prev_ref[bi:bi + 1, :], shifted)
        prev_ref[bi:bi + 1, :] = p[bi * L + L - 1:bi * L + L, :]
    z = p + (shifted - p) * mu_ref[...]

    def per_seq(x, r):
        return jnp.concatenate(
            [jnp.broadcast_to(x[bi * L + r:bi * L + r + 1, :], (L, x.shape[1])) for bi in range(RB)],
            axis=0)

    r = z[:, 0:W]
    k = z[:, W:2 * W]
    v = z[:, 2 * W:3 * W]
    zw = z[:, ZW_OFF:ZW_OFF + 128]
    za = z[:, ZA_OFF:ZA_OFF + 128]
    zg = z[:, ZG_OFF:ZG_OFF + 256]

    w_raw = w0_ref[...] + _dot(_stack_lhs(jnp.tanh(zw)), w2_ref[...])
    sp = jnp.maximum(-w_raw, 0.0) + jnp.log(1.0 + jnp.exp(-jnp.abs(w_raw)))
    lw = -jnp.exp(-sp - 0.5)
    a = jax.nn.sigmoid(a0_ref[...] + _dot(_stack_lhs(za), a2_ref[...]))
    g_ref[...] = _dot(jax.nn.sigmoid(zg).astype(bf16), g2_ref[...])

    ti = lax.broadcasted_iota(jnp.int32, (RL, RL), 0)
    si = lax.broadcasted_iota(jnp.int32, (RL, RL), 1)
    tri = jnp.logical_and(ti >= si, ti // L == si // L).astype(bf16)
    lw_h, lw_m, lw_l = _split3(lw)
    c = _dot(jnp.concatenate([tri, tri, tri], axis=1), jnp.concatenate([lw_h, lw_m, lw_l], axis=0))
    cex = c - lw
    cm = per_seq(c, L // 2 - 1)
    cl = per_seq(c, L - 1)
    e_m = jnp.exp(-cm)
    e_c = jnp.exp(c)
    e_ex = jnp.exp(cex)
    e_inv = jnp.exp(cm - c)
    e_tail = jnp.exp(cl - c)

    kkv = k * kk_ref[...]
    kmod = k * (1.0 + (a - 1.0) * ka_ref[...])
    at0_ref[...] = e_ex
    rt0 = r * e_c
    rt0_ref[...] = rt0
    rt_ref[...] = rt0 * e_m
    kt_ref[...] = kmod * e_inv
    kh_ref[...] = kmod * e_tail
    bt_ref[...] = a * e_inv
    bh_ref[...] = a * e_tail
    at_ref[...] = e_ex * e_m
    v_ref[...] = v
    for bi in range(RB):
        dec_ref[bi:bi + 1, :] = jnp.exp(c[bi * L + L - 1:bi * L + L, :])
    rkb_ref[...] = r * kmod * rk_ref[...]
    kkr_ref[...] = kkv

    PW = 2 * HEAD_N
    pairs = range(RWKV_HEADS // 2)
    lane = lax.broadcasted_iota(jnp.int32, (L, PW), 1)
    trow = lax.broadcasted_iota(jnp.int32, (L, PW), 0)
    first = lane < HEAD_N
    scol = jnp.where(first, lane, lane - HEAD_N)
    strict_p = trow > scol
    incl_p = trow >= scol
    eye_p = (trow == scol).astype(f32)
    rb =lax.broadcasted_iota(jnp.int32, (PW, PW), 0)
    lb = lax.broadcasted_iota(jnp.int32, (PW, PW), 1)
    bd_mask = (rb < HEAD_N) == (lb < HEAD_N)

    def bd(x):
        zero = jnp.zeros_like(x)
        return jnp.concatenate([jnp.where(first, x, zero), jnp.where(first, zero, x)], axis=0)

    def segsum(x):
        s0 = jnp.sum(jnp.where(first, x, 0.0), axis=-1, keepdims=True)
        s1 = jnp.sum(jnp.where(first, 0.0, x), axis=-1, keepdims=True)
        return jnp.where(first, s0, s1)

    units = [(bi, p) for bi in range(RB) for p in pairs]
    ps = [(slice(bi * L, (bi + 1) * L), slice(p * PW, (p + 1) * PW)) for bi, p in units]
    kkr = [kkr_ref[s] for s in ps]
    ssq = [segsum(x * x) for x in kkr]
    kkh = [x / jnp.maximum(jnp.sqrt(q), 1e-12) for x, q in zip(kkr, ssq)]
    at0 = [(-kh_ * at0_ref[s]).astype(bf16) for kh_, s in zip(kkh, ps)]
    at = [(-kh_ * at_ref[s]).astype(bf16) for kh_, s in zip(kkh, ps)]
    bt = [(kh_ * bt_ref[s]).astype(bf16) for kh_, s in zip(kkh, ps)]
    bh = [(kh_ * bh_ref[s]).astype(bf16) for kh_, s in zip(kkh, ps)]
    vb = [v_ref[s].astype(bf16) for s in ps]

    lhs = [jnp.concatenate([a_, rt_ref[s].astype(bf16)], axis=0) for a_, s in zip(at, ps)]
    sb = [_dot(l_, bd(b_), _NT) for l_, b_ in zip(lhs, bt)]
    sk = [_dot(l_, bd(kt_ref[s].astype(bf16)), _NT) for l_, s in zip(lhs, ps)]
    a_ab = [jnp.where(strict_p, x[:L], 0.0) for x in sb]
    a_rb = [jnp.where(incl_p, x[L:], 0.0).astype(bf16) for x in sb]
    a_ak = [jnp.where(strict_p, x[:L], 0.0).astype(bf16) for x in sk]
    a_rk = [jnp.where(incl_p, x[L:], 0.0).astype(bf16) for x in sk]

    tinv = [eye_p + x for x in a_ab]
    xp = a_ab
    n = 2
    while n < L:
        xb = [x.astype(bf16) for x in xp]
        xp = [_dot(x, bd(x)) for x in xb]
        tinv = [t + _dot(t.astype(bf16), bd(x.astype(bf16))) for t, x in zip(tinv, xp)]
        n *= 2
    tb = [t.astype(bf16) for t in tinv]

    akv = [_dot(a_, bd(v_)).astype(bf16) for a_, v_ in zip(a_ak, vb)]
    w12 = [_dot(t, jnp.concatenate([bd(a_), bd(k_)], axis=1)) for t, a_, k_ in zip(tb, at0, akv)]

    n_pairs = len(pairs)
    s0 = [state_ref[bi * n_pairs + p] for bi, p in units]
    s0b = [s.astype(bf16) for s in s0]
    uy = [_dot(jnp.concatenate([w[:, :PW].astype(bf16), rt0_ref[s].astype(bf16)], axis=0), sb_, _NT)
          for w, s, sb_ in zip(w12, ps, s0b)]
    u = [x[:L] + w[:, PW:] for x, w in zip(uy, w12)]
    ub = [x.astype(bf16) for x in u]
    y = [x[L:] + _dot(jnp.concatenate([rb_, rk_], axis=1), jnp.concatenate([bd(u_), bd(v_)], axis=0))
         for x, rb_, rk_, u_, v_ in zip(uy, a_rb, a_rk, ub, vb)]
    for i, (bi, p) in enumerate(units):
        upd = _dot(jnp.concatenate([ub[i], vb[i]], axis=0),
                   jnp.concatenate([bh[i], kh_ref[ps[i]].astype(bf16)], axis=0), _TN)
        state_ref[bi * n_pairs + p] = (s0[i] * dec_ref[bi:bi + 1, ps[i][1]]
                                       + jnp.where(bd_mask, upd, 0.0))

    inv_n = 1.0 / HEAD_N
    mean = [segsum(x) * inv_n for x in y]
    yc = [x - m for x, m in zip(y, mean)]
    var = [segsum(x * x) * inv_n for x in yc]
    bonus = [segsum(rkb_ref[s]) for s in ps]
    for i, (bi, p) in enumerate(units):
        s = ps[i]
        yn = yc[i] * lax.rsqrt(var[i] + RWKV_GN_EPS) * lnw_ref[:, s[1]] + lnb_ref[:, s[1]]
        o_ref[bi, :, s[1]] = ((yn + bonus[i] * v_ref[s]) * g_ref[s]).astype(o_ref.dtype)


def _rwkv(proj3, mu, w0, w2d, a0, a2, g2, k_k, k_a, r_k, ln_w, ln_b):
    B, S, NP = proj3.shape
    L, W = RWKV_CHUNK, RWKV_WIDTH
    row = lambda n: pl.BlockSpec((1, n), lambda b, c: (0, 0))
    full = lambda a: pl.BlockSpec(a.shape, lambda b, c: (0, 0))
    RB = _RWKV_SEQS if B % _RWKV_SEQS == 0 else 1
    return pl.pallas_call(
        _rwkv_kernel,
        out_shape=jax.ShapeDtypeStruct((B, S, W), bf16),
        grid=(B // RB, S // L),
        in_specs=[pl.BlockSpec((RB, L, SHIFT_PAD), lambda b, c: (b, c, 0)),
                  row(SHIFT_PAD), row(W), full(w2d), row(W), full(a2), full(g2),
                  row(W), row(W), row(W), row(W), row(W)],
        out_specs=pl.BlockSpec((RB, L, W), lambda b, c: (b, c, 0)),
        scratch_shapes=[pltpu.VMEM((RB * RWKV_HEADS // 2, 2 * HEAD_N, 2 * HEAD_N), f32),
                        pltpu.VMEM((RB, SHIFT_PAD), f32), pltpu.VMEM((RB, W), f32),
                        pltpu.VMEM((_RWKV_NWIDE, RB * L, W), f32)],
        compiler_params=pltpu.CompilerParams(
            dimension_semantics=("arbitrary", "arbitrary"), vmem_limit_bytes=VMEM_LIMIT),
        name="rwkv",
    )(proj3, mu, w0, w2d, a0, a2, g2, k_k, k_a, r_k, ln_w, ln_b)


def _rope(x, cosf, sinf, lane):
    rot = jnp.where(lane < ROPE_DIM // 2, pltpu.roll(x, ATTN_E - ROPE_DIM // 2, axis=1),
                    pltpu.roll(x, ROPE_DIM // 2, axis=1))
    return x * cosf + rot * sinf


def _attn_kernel(q_ref, kc_ref, kp_ref, vc_ref, vp_ref, cc_ref, sc_ref, cp_ref, sp_ref,
                 o_ref, l_ref, *, dil, hps, nblk, wave):
    n = pl.program_id(1)
    blk = ATTN_BLK
    span = blk * dil
    qi = lax.broadcasted_iota(jnp.int32, (blk, blk), 0)
    kj = lax.broadcasted_iota(jnp.int32, (blk, blk), 1)
    lane = lax.broadcasted_iota(jnp.int32, (blk, ATTN_E), 1)
    cur_ok = kj <= qi
    prev_ok = kj >= qi
    first_ok = jnp.logical_and(prev_ok, n > 0)
    scale = ATTN_E ** -0.5

    def rows(b, r):
        return pl.ds(b * span + r, blk, stride=dil) if dil > 1 else pl.ds(b * span, blk)

    def cols(hh):
        return slice(hh * ATTN_E, (hh + 1) * ATTN_E)

    all_units = [(b, r, hh) for b in range(nblk) for r in range(dil) for hh in range(hps)]
    kc, vc = {}, {}
    for w0 in range(0, len(all_units), wave):
        units = all_units[w0:w0 + wave]
        tab = {u[:2]: (cc_ref[0, rows(u[0], u[1]), :], sc_ref[0, rows(u[0], u[1]), :]) for u in units}
        q = {u: _rope(q_ref[0, rows(u[0], u[1]), cols(u[2])], *tab[u[:2]], lane).astype(bf16)
             for u in units}
        for u in units:
            kc[u] = _rope(kc_ref[0, rows(u[0], u[1]), cols(u[2])], *tab[u[:2]], lane).astype(bf16)
            vc[u] = vc_ref[0, rows(u[0], u[1]), cols(u[2])].astype(bf16)
        kp, vp = {}, {}
        for (b, r, hh) in units:
            if b == 0:
                ptab = (cp_ref[0, rows(0, r), :], sp_ref[0, rows(0, r), :])
                kp[(b, r, hh)] = _rope(kp_ref[0, rows(0, r), cols(hh)], *ptab, lane).astype(bf16)
                vp[(b, r, hh)] = vp_ref[0, rows(0, r), cols(hh)].astype(bf16)
            else:
                kp[(b, r, hh)] = kc[(b - 1, r, hh)]
                vp[(b, r, hh)] = vc[(b - 1, r, hh)]
        s_c = {u: jnp.where(cur_ok, _dot(q[u], kc[u], _NT) * scale, -jnp.inf) for u in units}
        s_p = {u: jnp.where(first_ok if u[0] == 0 else prev_ok, _dot(q[u], kp[u], _NT) * scale, -jnp.inf)
               for u in units}
        m = {u: jnp.maximum(jnp.max(s_c[u], axis=-1, keepdims=True),
                            jnp.max(s_p[u], axis=-1, keepdims=True)) for u in units}
        p_c = {u: jnp.exp(s_c[u] - m[u]) for u in units}
        p_p = {u: jnp.exp(s_p[u] - m[u]) for u in units}
        l = {u: jnp.sum(p_c[u], axis=-1, keepdims=True) + jnp.sum(p_p[u], axis=-1, keepdims=True)
             for u in units}
        acc = {u: _dot(p_c[u].astype(bf16), vc[u]) + _dot(p_p[u].astype(bf16), vp[u]) for u in units}
        for u in units:
            o_ref[0, rows(u[0], u[1]), cols(u[2])] = acc[u] / l[u]
            l_ref[0, rows(u[0], u[1]), cols(u[2])] = jnp.broadcast_to(m[u] + jnp.log(l[u]), (blk, ATTN_E))


def _attn_group(proj3, cosf, sinf, gi, dil):
    B, S, NP = proj3.shape
    span = ATTN_BLK * dil
    hps = HEADS_PER_GROUP if dil == 1 else 1
    nblk = _ATTN_UNITS // (dil * hps)
    R = span * nblk
    cw = hps * ATTN_E
    per = ATTN_GW // cw
    qo = (ATTN_OFF + gi * ATTN_GW) // cw
    ko = qo + ATTN_WIDTH // cw
    vo = ko + ATTN_WIDTH // cw

    def cur(off):
        return pl.BlockSpec((1, R, cw), lambda b, n, h: (b, n, off + h))

    def prev(off):
        return pl.BlockSpec((1, span, cw), lambda b, n, h: (b, jnp.maximum(n * nblk - 1, 0), off + h))

    tc = pl.BlockSpec((1, R, ATTN_E), lambda b, n, h: (b, n, 0))
    tp = pl.BlockSpec((1, span, ATTN_E), lambda b, n, h: (b, jnp.maximum(n * nblk - 1, 0), 0))
    out = pl.BlockSpec((1, R, cw), lambda b, n, h: (b, n, h))
    o, l = pl.pallas_call(
        functools.partial(_attn_kernel, dil=dil, hps=hps, nblk=nblk,
                          wave=_ATTN_UNITS // 2 if dil > HEADS_PER_GROUP else _ATTN_UNITS),
        out_shape=[jax.ShapeDtypeStruct((B, S, ATTN_GW), f32)] * 2,
        grid=(B, S // R, per),
        in_specs=[cur(qo), cur(ko), prev(ko), cur(vo), prev(vo), tc, tc, tp, tp],
        out_specs=[out, out],
        compiler_params=pltpu.CompilerParams(
            dimension_semantics=("arbitrary", "arbitrary", "arbitrary"),
            vmem_limit_bytes=VMEM_LIMIT),
        name=f"attn_g{gi}",
    )(proj3, proj3, proj3, proj3, proj3, cosf, sinf, cosf, sinf)
    return o.reshape(B * S, ATTN_GW), l.reshape(B * S, ATTN_GW)


def _merge_kernel(rw_ref, o0_ref, o1_ref, o2_ref, l0_ref, l1_ref, l2_ref, ga_ref, gb_ref,
                  wor_ref, woa_ref, mg_ref):
    l0, l1, l2 = l0_ref[...], l1_ref[...], l2_ref[...]
    m = jnp.maximum(jnp.maximum(l0, l1), l2)
    e0, e1, e2 = jnp.exp(l0 - m), jnp.exp(l1 - m), jnp.exp(l2 - m)
    o = (e0 * o0_ref[...] + e1 * o1_ref[...] + e2 * o2_ref[...]) / (e0 + e1 + e2)
    y_b = _dot(o.astype(bf16), woa_ref[...])
    y_a = _dot(rw_ref[...], wor_ref[...])
    merged = jax.nn.sigmoid(ga_ref[...]) * y_a + jax.nn.sigmoid(gb_ref[...]) * y_b
    mg_ref[...] = merged.astype(bf16)


def _out_kernel(x_ref, mg_ref, wout_ref, gn_ref, wr_ref, br_ref, x1_ref, h2_ref, lg_ref):
    x1 = x_ref[...] + _dot(mg_ref[...], wout_ref[...])
    x1_ref[...] = x1
    ms = jnp.mean(x1 * x1, axis=-1, keepdims=True)
    h2 = x1 * lax.rsqrt(ms + NORM_EPS) * gn_ref[...]
    h2_ref[...] = h2.astype(bf16)
    lg_ref[...] = _dot(_stack_lhs(h2), wr_ref[...]) + br_ref[...]


def _post(x2, rw, outs, lses, proj, wor, woa, wout, gn, wr, br, tm=512):
    T, D = x2.shape
    row = lambda w: pl.BlockSpec((tm, w), lambda i: (i, 0))
    const = lambda a: pl.BlockSpec(a.shape, lambda i: (0, 0), pipeline_mode=pl.Buffered(1))
    ga_blk = GATE_OFF // D
    params = pltpu.CompilerParams(dimension_semantics=("arbitrary",), vmem_limit_bytes=VMEM_LIMIT)
    merged = pl.pallas_call(
        _merge_kernel,
        out_shape=jax.ShapeDtypeStruct((T, D), bf16),
        grid=(T // tm,),
        in_specs=[row(RWKV_WIDTH)] + [row(ATTN_GW)] * 6
                 + [pl.BlockSpec((tm, D), lambda i: (i, ga_blk)),
                    pl.BlockSpec((tm, D), lambda i: (i, ga_blk + 1)),
                    const(wor), const(woa)],
        out_specs=row(D),
        compiler_params=params,
        name="merge",
    )(rw, *outs, *lses, proj, proj, wor, woa)
    return pl.pallas_call(
        _out_kernel,
        out_shape=[jax.ShapeDtypeStruct((T, D), f32), jax.ShapeDtypeStruct((T, D), bf16),
                   jax.ShapeDtypeStruct((T, 128), f32)],
        grid=(T // tm,),
        in_specs=[row(D), row(D), const(wout), const(gn), const(wr), const(br)],
        out_specs=[row(D), row(D), row(128)],
        compiler_params=params,
        name="out",
    )(x2, merged, wout, gn, wr, br)


def _moe_up_kernel(st, sc, se, sf, nv, xs_ref, w1g_ref, w1l_ref, b1g_ref, b1l_ref, *rest):
    o_ref, wg_ref, wl_ref = rest[-3:]
    s = pl.program_id(0)

    @pl.when(s < nv[0])
    def _():
        @pl.when(sf[s] == 1)
        def _():
            wg_ref[...] = w1g_ref[...].astype(bf16)
            wl_ref[...] = w1l_ref[...].astype(bf16)

        xs = xs_ref[...]
        hg = _dot(xs, wg_ref[...]) + b1g_ref[...]
        hl = _dot(xs, wl_ref[...]) + b1l_ref[...]
        x_glu = jnp.minimum(hg, SWIGLU_LIMIT)
        x_lin = jnp.clip(hl, -SWIGLU_LIMIT, SWIGLU_LIMIT)
        act = x_glu * jax.nn.sigmoid(SWIGLU_ALPHA * x_glu) * (x_lin + 1.0)
        o_ref[...] = act.astype(o_ref.dtype)


def _moe_down_kernel(st, sc, se, sf, nv, a_ref, w2_ref, b2_ref, o_ref, w_ref):
    s = pl.program_id(0)

    @pl.when(s < nv[0])
    def _():
        @pl.when(sf[s] == 1)
        def _():
            w_ref[...] = w2_ref[...].astype(bf16)

        o_ref[...] = (_dot(a_ref[...], w_ref[...]) + b2_ref[...]).astype(o_ref.dtype)


def _moe_schedule(tile_e, n_valid, first_tile, n_tiles_e, nc, lo, hi):
    i32 = jnp.int32
    E = first_tile.shape[0]
    experts = jnp.arange(E, dtype=i32)
    nv = jnp.clip(n_valid[0], lo, hi) - lo
    s = jnp.arange((hi - lo) * nc, dtype=i32)
    s = jnp.clip(s, 0, jnp.maximum(nc * nv - 1, 0))
    e = tile_e[lo:hi][s // nc]
    pick = lambda table: jnp.sum(jnp.where(e[:, None] == experts[None, :], table[None, :], 0), axis=1)
    ft = jnp.clip(first_tile, lo, hi)
    ne = jnp.clip(first_tile + n_tiles_e, lo, hi) - ft
    ft, ne = pick(ft), jnp.maximum(pick(ne), 1)
    local = s - nc * (ft - lo)
    col = local // ne
    row = ft + local % ne
    changed = jnp.logical_or(e != jnp.roll(e, 1), col != jnp.roll(col, 1)).at[0].set(True)
    return (row.astype(i32), col.astype(i32), e.astype(i32), changed.astype(i32),
            (nc * nv).astype(i32).reshape(1))


def _moe(tile_e, n_valid, first_tile, n_tiles_e, h2, tok, w1, b1, w2, b2, tm, tf=1024, tn=1024):
    P = tok.shape[0]
    D = h2.shape[1]
    E, _, F2 = w1.shape
    F = F2 // 2
    nj = F // tf
    nn = D // tn
    nt = P // tm
    params = pltpu.CompilerParams(dimension_semantics=("arbitrary",), vmem_limit_bytes=VMEM_LIMIT)

    n_chunks = _MOE_CHUNKS if nt % _MOE_CHUNKS == 0 else 1
    ct = nt // n_chunks
    act = None
    for ck in range(n_chunks):
        lo = ck * ct
        sched = _moe_schedule(tile_e, n_valid, first_tile, n_tiles_e, nj, lo, lo + ct)
        xs_k = h2[tok[lo * tm:(lo + ct) * tm]]
        carry = [] if act is None else [act]
        act = pl.pallas_call(
            _moe_up_kernel,
            out_shape=jax.ShapeDtypeStruct((P, F), bf16),
            grid_spec=pltpu.PrefetchScalarGridSpec(
                num_scalar_prefetch=5,
                grid=(ct * nj,),
                in_specs=[
                    pl.BlockSpec((tm, D), lambda s, st, sc, se, sf, nv, lo=lo: (st[s] - lo, 0)),
                    pl.BlockSpec((None, D, tf), lambda s, st, sc, se, sf, nv: (se[s], 0, sc[s])),
                    pl.BlockSpec((None, D, tf), lambda s, st, sc, se, sf, nv: (se[s], 0, nj + sc[s])),
                    pl.BlockSpec((None, 1, tf), lambda s, st, sc, se, sf, nv: (se[s], 0, sc[s])),
                    pl.BlockSpec((None, 1, tf), lambda s, st, sc, se, sf, nv: (se[s], 0, nj + sc[s])),
                ] + [pl.BlockSpec(memory_space=pl.ANY)] * len(carry),
                out_specs=pl.BlockSpec((tm, tf), lambda s, st, sc, se, sf, nv: (st[s], sc[s])),
                scratch_shapes=[pltpu.VMEM((D, tf), bf16), pltpu.VMEM((D, tf), bf16)],
            ),
            input_output_aliases={10: 0} if carry else {},
            compiler_params=params,
            name=f"moe_up{ck}",
        )(*sched, xs_k, w1, w1, b1, b1, *carry)

    sched = _moe_schedule(tile_e, n_valid, first_tile, n_tiles_e, nn, 0, nt)
    return pl.pallas_call(
        _moe_down_kernel,
        out_shape=jax.ShapeDtypeStruct((P, D), bf16),
        grid_spec=pltpu.PrefetchScalarGridSpec(
            num_scalar_prefetch=5,
            grid=(nt * nn,),
            in_specs=[
                pl.BlockSpec((tm, F), lambda s, st, sc, se, sf, nv: (st[s], 0)),
                pl.BlockSpec((None, F, tn), lambda s, st, sc, se, sf, nv: (se[s], 0, sc[s])),
                pl.BlockSpec((None, 1, tn), lambda s, st, sc, se, sf, nv: (se[s], 0, sc[s])),
            ],
            out_specs=pl.BlockSpec((tm, tn), lambda s, st, sc, se, sf, nv: (st[s], sc[s])),
            scratch_shapes=[pltpu.VMEM((F, tn), bf16)],
        ),
        compiler_params=params,
        name="moe_down",
    )(*sched, act, w2, b2)


def _final_kernel(x1_ref, y_ref, gate_ref, g_ref, o_ref, *, apply_norm):
    gates = gate_ref[...]
    y = gates[:, 0:1] * y_ref[0].astype(f32)
    for kk in range(1, TOP_K):
        y = y + gates[:, kk:kk + 1] * y_ref[kk].astype(f32)
    x2 = x1_ref[...] + y
    if apply_norm:
        ms = jnp.mean(x2 * x2, axis=-1, keepdims=True)
        x2 = x2 * lax.rsqrt(ms + NORM_EPS) * g_ref[...]
    o_ref[...] = x2


def _final(x1, y4, gates, g, apply_norm, tm=256):
    T, D = x1.shape
    return pl.pallas_call(
        functools.partial(_final_kernel, apply_norm=apply_norm),
        out_shape=jax.ShapeDtypeStruct((T, D), f32),
        grid=(T // tm,),
        in_specs=[pl.BlockSpec((tm, D), lambda i: (i, 0)),
                  pl.BlockSpec((TOP_K, tm, D), lambda i: (0, i, 0)),
                  pl.BlockSpec((tm, TOP_K), lambda i: (i, 0)),
                  pl.BlockSpec((1, D), lambda i: (0, 0))],
        out_specs=pl.BlockSpec((tm, D), lambda i: (i, 0)),
        compiler_params=pltpu.CompilerParams(
            dimension_semantics=("arbitrary",), vmem_limit_bytes=VMEM_LIMIT),
        name="final",
    )(x1, y4, gates, g)


def _pad_cols(a, n):
    return jnp.pad(a, ((0, 0), (0, n - a.shape[1])))


def _pad_rows(a, n):
    return jnp.pad(a, ((0, n - a.shape[0]), (0, 0)))


def _regroup_cols(a):
    W = RWKV_WIDTH
    o = 3 * W
    return jnp.concatenate([
        a[:, :o],
        _pad_cols(a[:, o:o + DECAY_LORA], 128),
        _pad_cols(a[:, o + DECAY_LORA:o + DECAY_LORA + ICLR_LORA], 128),
        _pad_cols(a[:, o + DECAY_LORA + ICLR_LORA:o + DECAY_LORA + ICLR_LORA + GATE_LORA], 256),
        a[:, o + DECAY_LORA + ICLR_LORA + GATE_LORA:],
    ], axis=1)


def _rope_tables(positions):
    half = ROPE_DIM // 2
    inv_freq = ROPE_THETA ** (-jnp.arange(half, dtype=f32) / half)
    ang = positions.astype(f32)[..., None] * inv_freq
    cos, sin = jnp.cos(ang), jnp.sin(ang)
    B, S = positions.shape
    cosf = jnp.concatenate([cos, cos, jnp.ones((B, S, ATTN_E - ROPE_DIM), f32)], axis=-1)
    sinf = jnp.concatenate([-sin, sin, jnp.zeros((B, S, ATTN_E - ROPE_DIM), f32)], axis=-1)
    return cosf, sinf


def _route(logits, tm, n_tiles):
    T, E = logits.shape
    n = T * TOP_K
    i32 = jnp.int32
    top_val, top_idx = lax.top_k(logits, TOP_K)
    gates = jax.nn.softmax(top_val, axis=-1)
    flat_e = top_idx.reshape(-1).astype(i32)
    iota = jnp.arange(n, dtype=i32)
    experts = jnp.arange(E, dtype=i32)

    def lookup(table, idx):
        return jnp.sum(jnp.where(idx[:, None] == experts[None, :], table[None, :], 0), axis=1)

    _, order = lax.sort((flat_e, iota), num_keys=1, is_stable=True)
    _, rank = lax.sort((order, iota), num_keys=1)
    sizes = jnp.sum((flat_e[:, None] == experts[None, :]).astype(i32), axis=0)
    padded = ((sizes + tm - 1) // tm) * tm
    pad_end = jnp.cumsum(padded)
    pad_start = pad_end - padded
    start = jnp.cumsum(sizes) - sizes
    pos = lookup(pad_start - start, flat_e) + rank
    tile_start = jnp.arange(n_tiles, dtype=i32) * tm
    tile_e = jnp.minimum(jnp.sum((pad_end[None, :] <= tile_start[:, None]).astype(i32), axis=1), E - 1)
    n_valid = (pad_end[-1] // tm).astype(i32).reshape(1)
    row_e = jnp.repeat(tile_e, tm)
    rows = jnp.arange(n_tiles * tm, dtype=i32)
    off = rows - lookup(pad_start, row_e)
    src = jnp.clip(lookup(start, row_e) + off, 0, n - 1)
    tok = jnp.where(off < lookup(sizes, row_e), order[src] // TOP_K, rows % T)
    pos_slot_major = pos.reshape(T, TOP_K).T.reshape(-1)
    return tok, gates, pos_slot_major, tile_e, n_valid, pad_start // tm, padded // tm


def kernel(x, positions, norm_mix, w_in, shift_mu, w0, w2_decay, a0, a2_iclr, g2_gate, k_k, k_a,
           r_k, ln_x_w, ln_x_b, w_o_rwkv, w_o_attn, w_out, norm_ffn, w_router, b_router,
           w1, b1, w2, b2, norm_final):
    B, S, D = x.shape
    T = B * S
    E = w_router.shape[-1]
    depth = norm_mix.shape[0]
    xt = x.reshape(T, D)
    cosf, sinf = _rope_tables(positions)
    moe_tm = 512
    n_tiles = (T * TOP_K) // moe_tm + E
    for layer in range(depth):
        wp = _regroup_cols(w_in[layer]).astype(bf16)
        mu = _regroup_cols(shift_mu[layer][None, :])
        proj = _proj(xt, norm_mix[layer][None, :], wp)
        proj3 = proj.reshape(B, S, -1)
        rw = _rwkv(proj3, mu, w0[layer][None, :], _stack_rhs(_pad_rows(w2_decay[layer], 128)),
                   a0[layer][None, :], _stack_rhs(_pad_rows(a2_iclr[layer], 128)),
                   _pad_rows(g2_gate[layer], 256).astype(bf16), k_k[layer][None, :],
                   k_a[layer][None, :],
                   r_k[layer].reshape(1, -1), ln_x_w[layer][None, :], ln_x_b[layer][None, :])
        outs, lses = [], []
        for gi, (window, dil) in enumerate(ATTN_GROUPS):
            o_g, l_g = _attn_group(proj3, cosf, sinf, gi, dil)
            outs.append(o_g)
            lses.append(l_g)
        x1, h2, logits = _post(
            xt, rw.reshape(T, -1), outs, lses, proj,
            w_o_rwkv[layer].astype(bf16), w_o_attn[layer].astype(bf16), w_out[layer].astype(bf16),
            norm_ffn[layer][None, :], _stack_rhs(_pad_cols(w_router[layer], 128)),
            _pad_cols(b_router[layer][None, :], 128))
        tok, gates, pos, tile_e, n_valid, first_tile, n_tiles_e = _route(logits[:, :E], moe_tm, n_tiles)
        ys = _moe(tile_e, n_valid, first_tile, n_tiles_e, h2, tok, w1[layer], b1[layer][:, None, :],
                  w2[layer], b2[layer][:, None, :], moe_tm)
        y4 = ys[pos].reshape(TOP_K, T, D)
        xt = _final(x1, y4, gates, norm_final[None, :], apply_norm=layer + 1 == depth)
    return xt.reshape(B, S, D)
```

```python
import functools

import jax
import jax.numpy as jnp
from jax import lax
from jax.experimental import pallas as pl
from jax.experimental.pallas import tpu as pltpu

f32 = jnp.float32
bf16 = jnp.bfloat16
HIGHEST = lax.Precision.HIGHEST

NORM_EPS = 1e-5
RWKV_GN_EPS = 64e-5
ROPE_THETA = 500000.0
SWIGLU_LIMIT = 7.0
SWIGLU_ALPHA = 1.702

RWKV_HEADS = 16
HEAD_N = 64
RWKV_WIDTH = RWKV_HEADS * HEAD_N
DECAY_LORA = 64
ICLR_LORA = 64
GATE_LORA = 160
ATTN_GROUPS = ((128, 1), (512, 4), (2048, 16))
HEADS_PER_GROUP = 4
ATTN_E = 128
ATTN_GW = HEADS_PER_GROUP * ATTN_E
ATTN_WIDTH = len(ATTN_GROUPS) * ATTN_GW
ROPE_DIM = ATTN_E // 4
TOP_K = 4

LORA_PAD = 512
ZW_OFF, ZA_OFF, ZG_OFF = 3 * RWKV_WIDTH, 3 * RWKV_WIDTH + 128, 3 * RWKV_WIDTH + 256
SHIFT_PAD = 3 * RWKV_WIDTH + LORA_PAD
ATTN_OFF = SHIFT_PAD
GATE_OFF = ATTN_OFF + 3 * ATTN_WIDTH

RWKV_CHUNK = 64
_MOE_CHUNK_TENTHS = (0, 1, 4, 7, 10)
_FINAL_CHUNKS = 4
_RWKV_SEQS = 2
_RWKV_NWIDE = 12
ATTN_BLK = 128
_ATTN_UNITS = 16
VMEM_LIMIT = 56 * 1024 * 1024


def _dot(a, b, dims=(((1,), (0,)), ((), ())), precision=None):
    return lax.dot_general(a, b, dims, precision=precision, preferred_element_type=f32)


_NT = (((1,), (1,)), ((), ()))
_TN = (((0,), (0,)), ((), ()))


def _split2(a):
    hi = a.astype(bf16)
    return hi, (a - hi.astype(f32)).astype(bf16)


def _split3(a):
    hi = a.astype(bf16)
    rem = a - hi.astype(f32)
    mid = rem.astype(bf16)
    return hi, mid, (rem - mid.astype(f32)).astype(bf16)


def _stack_lhs(a):
    hi, lo = _split2(a)
    return jnp.concatenate([hi, lo, hi], axis=1)


def _stack_rhs(w):
    hi, lo = _split2(w)
    return jnp.concatenate([hi, hi, lo], axis=0)


def _proj_kernel(x_ref, g_ref, w_ref, o_ref, h_ref):
    @pl.when(pl.program_id(1) == 0)
    def _():
        x = x_ref[...]
        ms = jnp.mean(x * x, axis=-1, keepdims=True)
        h_ref[...] = (x * lax.rsqrt(ms + NORM_EPS) * g_ref[...]).astype(bf16)

    o_ref[...] = _dot(h_ref[...], w_ref[...])


def _proj(x2, g, wp, tm=512, tn=2048):
    T, D = x2.shape
    NP = wp.shape[1]
    return pl.pallas_call(
        _proj_kernel,
        out_shape=jax.ShapeDtypeStruct((T, NP), f32),
        grid=(T // tm, NP // tn),
        in_specs=[pl.BlockSpec((tm, D), lambda i, j: (i, 0)),
                  pl.BlockSpec((1, D), lambda i, j: (0, 0)),
                  pl.BlockSpec((D, tn), lambda i, j: (0, j))],
        out_specs=pl.BlockSpec((tm, tn), lambda i, j: (i, j)),
        scratch_shapes=[pltpu.VMEM((tm, D), bf16)],
        compiler_params=pltpu.CompilerParams(
            dimension_semantics=("arbitrary", "arbitrary"), vmem_limit_bytes=VMEM_LIMIT),
        name="proj",
    )(x2, g, wp)


def _rwkv_kernel(p_ref, mu_ref, w0_ref, w2_ref, a0_ref, a2_ref, g2_ref, kk_ref, ka_ref,
                 rk_ref, lnw_ref, lnb_ref, o_ref, state_ref, prev_ref, dec_ref, wide_ref):
    (at0_ref, at_ref, rt0_ref, rt_ref, bt_ref, kt_ref, bh_ref, kh_ref, v_ref, rkb_ref, g_ref,
     kkr_ref) = [wide_ref.at[i] for i in range(_RWKV_NWIDE)]
    L = RWKV_CHUNK
    W = RWKV_WIDTH
    c_idx = pl.program_id(1)

    @pl.when(c_idx == 0)
    def _():
        state_ref[...] = jnp.zeros_like(state_ref)
        prev_ref[...] = jnp.zeros_like(prev_ref)

    RB = p_ref.shape[0]
    RL = RB * L
    p = p_ref[...].reshape(RL, p_ref.shape[2])
    row = lax.broadcasted_iota(jnp.int32, (RL, 1), 0)
    shifted = pltpu.roll(p, 1, axis=0)
    for bi in range(RB):
        shifted = jnp.where(row == bi * L, prev_ref[bi:bi + 1, :], shifted)
        prev_ref[bi:bi + 1, :] = p[bi * L + L - 1:bi * L + L, :]
    z = p + (shifted - p) * mu_ref[...]

    def per_seq(x, r):
        return jnp.concatenate(
            [jnp.broadcast_to(x[bi * L + r:bi * L + r + 1, :], (L, x.shape[1])) for bi in range(RB)],
            axis=0)

    r = z[:, 0:W]
    k = z[:, W:2 * W]
    v = z[:, 2 * W:3 * W]
    zw = z[:, ZW_OFF:ZW_OFF + 128]
    za = z[:, ZA_OFF:ZA_OFF + 128]
    zg = z[:, ZG_OFF:ZG_OFF + 256]

    w_raw = w0_ref[...] + _dot(_stack_lhs(jnp.tanh(zw)), w2_ref[...])
    sp = jnp.maximum(-w_raw, 0.0) + jnp.log(1.0 + jnp.exp(-jnp.abs(w_raw)))
    lw = -jnp.exp(-sp - 0.5)
    a = jax.nn.sigmoid(a0_ref[...] + _dot(_stack_lhs(za), a2_ref[...]))
    g_ref[...] = _dot(jax.nn.sigmoid(zg).astype(bf16), g2_ref[...])

    ti = lax.broadcasted_iota(jnp.int32, (RL, RL), 0)
    si = lax.broadcasted_iota(jnp.int32, (RL, RL), 1)
    tri = jnp.logical_and(ti >= si, ti // L == si // L).astype(bf16)
    lw_h, lw_m, lw_l = _split3(lw)
    c = _dot(jnp.concatenate([tri, tri, tri], axis=1), jnp.concatenate([lw_h, lw_m, lw_l], axis=0))
    cex = c - lw
    cm = per_seq(c, L // 2 - 1)
    cl = per_seq(c, L - 1)
    e_m = jnp.exp(-cm)
    e_c = jnp.exp(c)
    e_ex = jnp.exp(cex)
    e_inv = jnp.exp(cm - c)
    e_tail = jnp.exp(cl - c)

    kkv = k * kk_ref[...]
    kmod = k * (1.0 + (a - 1.0) * ka_ref[...])
    at0_ref[...] = e_ex
    rt0 = r * e_c
    rt0_ref[...] = rt0
    rt_ref[...] = rt0 * e_m
    kt_ref[...] = kmod * e_inv
    kh_ref[...] = kmod * e_tail
    bt_ref[...] = a * e_inv
    bh_ref[...] = a * e_tail
    at_ref[...] = e_ex * e_m
    v_ref[...] = v
    for bi in range(RB):
        dec_ref[bi:bi + 1, :] = jnp.exp(c[bi * L + L - 1:bi * L + L, :])
    rkb_ref[...] = r * kmod * rk_ref[...]
    kkr_ref[...] = kkv

    PW = 2 * HEAD_N
    pairs = range(RWKV_HEADS // 2)
    lane = lax.broadcasted_iota(jnp.int32, (L, PW), 1)
    trow = lax.broadcasted_iota(jnp.int32, (L, PW), 0)
    first = lane < HEAD_N
    scol = jnp.where(first, lane, lane - HEAD_N)
    strict_p = trow > scol
    incl_p = trow >= scol
    eye_p = (trow == scol).astype(f32)
    rb =lax.broadcasted_iota(jnp.int32, (PW, PW), 0)
    lb = lax.broadcasted_iota(jnp.int32, (PW, PW), 1)
    bd_mask = (rb < HEAD_N) == (lb < HEAD_N)

    def bd(x):
        zero = jnp.zeros_like(x)
        return jnp.concatenate([jnp.where(first, x, zero), jnp.where(first, zero, x)], axis=0)

    def segsum(x):
        s0 = jnp.sum(jnp.where(first, x, 0.0), axis=-1, keepdims=True)
        s1 = jnp.sum(jnp.where(first, 0.0, x), axis=-1, keepdims=True)
        return jnp.where(first, s0, s1)

    units = [(bi, p) for bi in range(RB) for p in pairs]
    ps = [(slice(bi * L, (bi + 1) * L), slice(p * PW, (p + 1) * PW)) for bi, p in units]
    kkr = [kkr_ref[s] for s in ps]
    ssq = [segsum(x * x) for x in kkr]
    kkh = [x / jnp.maximum(jnp.sqrt(q), 1e-12) for x, q in zip(kkr, ssq)]
    at0 = [(-kh_ * at0_ref[s]).astype(bf16) for kh_, s in zip(kkh, ps)]
    at = [(-kh_ * at_ref[s]).astype(bf16) for kh_, s in zip(kkh, ps)]
    bt = [(kh_ * bt_ref[s]).astype(bf16) for kh_, s in zip(kkh, ps)]
    bh = [(kh_ * bh_ref[s]).astype(bf16) for kh_, s in zip(kkh, ps)]
    vb = [v_ref[s].astype(bf16) for s in ps]

    lhs = [jnp.concatenate([a_, rt_ref[s].astype(bf16)], axis=0) for a_, s in zip(at, ps)]
    sb = [_dot(l_, bd(b_), _NT) for l_, b_ in zip(lhs, bt)]
    sk = [_dot(l_, bd(kt_ref[s].astype(bf16)), _NT) for l_, s in zip(lhs, ps)]
    a_ab = [jnp.where(strict_p, x[:L], 0.0) for x in sb]
    a_rb = [jnp.where(incl_p, x[L:], 0.0).astype(bf16) for x in sb]
    a_ak = [jnp.where(strict_p, x[:L], 0.0).astype(bf16) for x in sk]
    a_rk = [jnp.where(incl_p, x[L:], 0.0).astype(bf16) for x in sk]

    tinv = [eye_p + x for x in a_ab]
    xp = a_ab
    n = 2
    while n < L:
        xb = [x.astype(bf16) for x in xp]
        xp = [_dot(x, bd(x)) for x in xb]
        tinv = [t + _dot(t.astype(bf16), bd(x.astype(bf16))) for t, x in zip(tinv, xp)]
        n *= 2
    tb = [t.astype(bf16) for t in tinv]

    akv = [_dot(a_, bd(v_)).astype(bf16) for a_, v_ in zip(a_ak, vb)]
    w12 = [_dot(t, jnp.concatenate([bd(a_), bd(k_)], axis=1)) for t, a_, k_ in zip(tb, at0, akv)]

    n_pairs = len(pairs)
    s0 = [state_ref[bi * n_pairs + p] for bi, p in units]
    s0b = [s.astype(bf16) for s in s0]
    uy = [_dot(jnp.concatenate([w[:, :PW].astype(bf16), rt0_ref[s].astype(bf16)], axis=0), sb_, _NT)
          for w, s, sb_ in zip(w12, ps, s0b)]
    u = [x[:L] + w[:, PW:] for x, w in zip(uy, w12)]
    ub = [x.astype(bf16) for x in u]
    y = [x[L:] + _dot(jnp.concatenate([rb_, rk_], axis=1), jnp.concatenate([bd(u_), bd(v_)], axis=0))
         for x, rb_, rk_, u_, v_ in zip(uy, a_rb, a_rk, ub, vb)]
    for i, (bi, p) in enumerate(units):
        upd = _dot(jnp.concatenate([ub[i], vb[i]], axis=0),
                   jnp.concatenate([bh[i], kh_ref[ps[i]].astype(bf16)], axis=0), _TN)
        state_ref[bi * n_pairs + p] = (s0[i] * dec_ref[bi:bi + 1, ps[i][1]]
                                       + jnp.where(bd_mask, upd, 0.0))

    inv_n = 1.0 / HEAD_N
    mean = [segsum(x) * inv_n for x in y]
    yc = [x - m for x, m in zip(y, mean)]
    var = [segsum(x * x) * inv_n for x in yc]
    bonus = [segsum(rkb_ref[s]) for s in ps]
    for i, (bi, p) in enumerate(units):
        s = ps[i]
        yn = yc[i] * lax.rsqrt(var[i] + RWKV_GN_EPS) * lnw_ref[:, s[1]] + lnb_ref[:, s[1]]
        o_ref[bi, :, s[1]] = ((yn + bonus[i] * v_ref[s]) * g_ref[s]).astype(o_ref.dtype)


def _rwkv(proj3, mu, w0, w2d, a0, a2, g2, k_k, k_a, r_k, ln_w, ln_b):
    B, S, NP = proj3.shape
    L, W = RWKV_CHUNK, RWKV_WIDTH
    row = lambda n: pl.BlockSpec((1, n), lambda b, c: (0, 0))
    full = lambda a: pl.BlockSpec(a.shape, lambda b, c: (0, 0))
    RB = _RWKV_SEQS if B % _RWKV_SEQS == 0 else 1
    return pl.pallas_call(
        _rwkv_kernel,
        out_shape=jax.ShapeDtypeStruct((B, S, W), bf16),
        grid=(B // RB, S // L),
        in_specs=[pl.BlockSpec((RB, L, SHIFT_PAD), lambda b, c: (b, c, 0)),
                  row(SHIFT_PAD), row(W), full(w2d), row(W), full(a2), full(g2),
                  row(W), row(W), row(W), row(W), row(W)],
        out_specs=pl.BlockSpec((RB, L, W), lambda b, c: (b, c, 0)),
        scratch_shapes=[pltpu.VMEM((RB * RWKV_HEADS // 2, 2 * HEAD_N, 2 * HEAD_N), f32),
                        pltpu.VMEM((RB, SHIFT_PAD), f32), pltpu.VMEM((RB, W), f32),
                        pltpu.VMEM((_RWKV_NWIDE, RB * L, W), f32)],
        compiler_params=pltpu.CompilerParams(
            dimension_semantics=("arbitrary", "arbitrary"), vmem_limit_bytes=VMEM_LIMIT),
        name="rwkv",
    )(proj3, mu, w0, w2d, a0, a2, g2, k_k, k_a, r_k, ln_w, ln_b)


def _rope(x, cosf, sinf, lane):
    rot = jnp.where(lane < ROPE_DIM // 2, pltpu.roll(x, ATTN_E - ROPE_DIM // 2, axis=1),
                    pltpu.roll(x, ROPE_DIM // 2, axis=1))
    return x * cosf + rot * sinf


def _attn_kernel(q_ref, kc_ref, kp_ref, vc_ref, vp_ref, cc_ref, sc_ref, cp_ref, sp_ref,
                 o_ref, l_ref, *, dil, hps, nblk, wave):
    n = pl.program_id(1)
    blk = ATTN_BLK
    span = blk * dil
    qi = lax.broadcasted_iota(jnp.int32, (blk, blk), 0)
    kj = lax.broadcasted_iota(jnp.int32, (blk, blk), 1)
    lane = lax.broadcasted_iota(jnp.int32, (blk, ATTN_E), 1)
    cur_ok = kj <= qi
    prev_ok = kj >= qi
    first_ok = jnp.logical_and(prev_ok, n > 0)
    scale = ATTN_E ** -0.5

    def rows(b, r):
        return pl.ds(b * span + r, blk, stride=dil) if dil > 1 else pl.ds(b * span, blk)

    def cols(hh):
        return slice(hh * ATTN_E, (hh + 1) * ATTN_E)

    all_units = [(b, r, hh) for b in range(nblk) for r in range(dil) for hh in range(hps)]
    kc, vc = {}, {}
    for w0 in range(0, len(all_units), wave):
        units = all_units[w0:w0 + wave]
        tab = {u[:2]: (cc_ref[0, rows(u[0], u[1]), :], sc_ref[0, rows(u[0], u[1]), :]) for u in units}
        q = {u: _rope(q_ref[0, rows(u[0], u[1]), cols(u[2])], *tab[u[:2]], lane).astype(bf16)
             for u in units}
        for u in units:
            kc[u] = _rope(kc_ref[0, rows(u[0], u[1]), cols(u[2])], *tab[u[:2]], lane).astype(bf16)
            vc[u] = vc_ref[0, rows(u[0], u[1]), cols(u[2])].astype(bf16)
        kp, vp = {}, {}
        for (b, r, hh) in units:
            if b == 0:
                ptab = (cp_ref[0, rows(0, r), :], sp_ref[0, rows(0, r), :])
                kp[(b, r, hh)] = _rope(kp_ref[0, rows(0, r), cols(hh)], *ptab, lane).astype(bf16)
                vp[(b, r, hh)] = vp_ref[0, rows(0, r), cols(hh)].astype(bf16)
            else:
                kp[(b, r, hh)] = kc[(b - 1, r, hh)]
                vp[(b, r, hh)] = vc[(b - 1, r, hh)]
        s_c = {u: jnp.where(cur_ok, _dot(q[u], kc[u], _NT) * scale, -jnp.inf) for u in units}
        s_p = {u: jnp.where(first_ok if u[0] == 0 else prev_ok, _dot(q[u], kp[u], _NT) * scale, -jnp.inf)
               for u in units}
        m = {u: jnp.maximum(jnp.max(s_c[u], axis=-1, keepdims=True),
                            jnp.max(s_p[u], axis=-1, keepdims=True)) for u in units}
        p_c = {u: jnp.exp(s_c[u] - m[u]) for u in units}
        p_p = {u: jnp.exp(s_p[u] - m[u]) for u in units}
        l = {u: jnp.sum(p_c[u], axis=-1, keepdims=True) + jnp.sum(p_p[u], axis=-1, keepdims=True)
             for u in units}
        acc = {u: _dot(p_c[u].astype(bf16), vc[u]) + _dot(p_p[u].astype(bf16), vp[u]) for u in units}
        for u in units:
            o_ref[0, rows(u[0], u[1]), cols(u[2])] = acc[u] / l[u]
            l_ref[0, rows(u[0], u[1]), cols(u[2])] = jnp.broadcast_to(m[u] + jnp.log(l[u]), (blk, ATTN_E))


def _attn_group(proj3, cosf, sinf, gi, dil):
    B, S, NP = proj3.shape
    span = ATTN_BLK * dil
    hps = HEADS_PER_GROUP if dil == 1 else 1
    nblk = _ATTN_UNITS // (dil * hps)
    R = span * nblk
    cw = hps * ATTN_E
    per = ATTN_GW // cw
    qo = (ATTN_OFF + gi * ATTN_GW) // cw
    ko = qo + ATTN_WIDTH // cw
    vo = ko + ATTN_WIDTH // cw

    def cur(off):
        return pl.BlockSpec((1, R, cw), lambda b, n, h: (b, n, off + h))

    def prev(off):
        return pl.BlockSpec((1, span, cw), lambda b, n, h: (b, jnp.maximum(n * nblk - 1, 0), off + h))

    tc = pl.BlockSpec((1, R, ATTN_E), lambda b, n, h: (b, n, 0))
    tp = pl.BlockSpec((1, span, ATTN_E), lambda b, n, h: (b, jnp.maximum(n * nblk - 1, 0), 0))
    out = pl.BlockSpec((1, R, cw), lambda b, n, h: (b, n, h))
    o, l = pl.pallas_call(
        functools.partial(_attn_kernel, dil=dil, hps=hps, nblk=nblk,
                          wave=_ATTN_UNITS // 2 if dil > HEADS_PER_GROUP else _ATTN_UNITS),
        out_shape=[jax.ShapeDtypeStruct((B, S, ATTN_GW), f32)] * 2,
        grid=(B, S // R, per),
        in_specs=[cur(qo), cur(ko), prev(ko), cur(vo), prev(vo), tc, tc, tp, tp],
        out_specs=[out, out],
        compiler_params=pltpu.CompilerParams(
            dimension_semantics=("arbitrary", "arbitrary", "arbitrary"),
            vmem_limit_bytes=VMEM_LIMIT),
        name=f"attn_g{gi}",
    )(proj3, proj3, proj3, proj3, proj3, cosf, sinf, cosf, sinf)
    return o.reshape(B * S, ATTN_GW), l.reshape(B * S, ATTN_GW)


def _merge_kernel(rw_ref, o0_ref, o1_ref, o2_ref, l0_ref, l1_ref, l2_ref, ga_ref, gb_ref,
                  wor_ref, woa_ref, mg_ref):
    l0, l1, l2 = l0_ref[...], l1_ref[...], l2_ref[...]
    m = jnp.maximum(jnp.maximum(l0, l1), l2)
    e0, e1, e2 = jnp.exp(l0 - m), jnp.exp(l1 - m), jnp.exp(l2 - m)
    o = (e0 * o0_ref[...] + e1 * o1_ref[...] + e2 * o2_ref[...]) / (e0 + e1 + e2)
    y_b = _dot(o.astype(bf16), woa_ref[...])
    y_a = _dot(rw_ref[...], wor_ref[...])
    merged = jax.nn.sigmoid(ga_ref[...]) * y_a + jax.nn.sigmoid(gb_ref[...]) * y_b
    mg_ref[...] = merged.astype(bf16)


def _out_kernel(x_ref, mg_ref, wout_ref, gn_ref, wr_ref, br_ref, x1_ref, h2_ref, lg_ref):
    x1 = x_ref[...] + _dot(mg_ref[...], wout_ref[...])
    x1_ref[...] = x1
    ms = jnp.mean(x1 * x1, axis=-1, keepdims=True)
    h2 = x1 * lax.rsqrt(ms + NORM_EPS) * gn_ref[...]
    h2_ref[...] = h2.astype(bf16)
    lg_ref[...] = _dot(_stack_lhs(h2), wr_ref[...]) + br_ref[...]


def _post(x2, rw, outs, lses, proj, wor, woa, wout, gn, wr, br, tm=512):
    T, D = x2.shape
    row = lambda w: pl.BlockSpec((tm, w), lambda i: (i, 0))
    const = lambda a: pl.BlockSpec(a.shape, lambda i: (0, 0), pipeline_mode=pl.Buffered(1))
    ga_blk = GATE_OFF // D
    params = pltpu.CompilerParams(dimension_semantics=("arbitrary",), vmem_limit_bytes=VMEM_LIMIT)
    merged = pl.pallas_call(
        _merge_kernel,
        out_shape=jax.ShapeDtypeStruct((T, D), bf16),
        grid=(T // tm,),
        in_specs=[row(RWKV_WIDTH)] + [row(ATTN_GW)] * 6
                 + [pl.BlockSpec((tm, D), lambda i: (i, ga_blk)),
                    pl.BlockSpec((tm, D), lambda i: (i, ga_blk + 1)),
                    const(wor), const(woa)],
        out_specs=row(D),
        compiler_params=params,
        name="merge",
    )(rw, *outs, *lses, proj, proj, wor, woa)
    return pl.pallas_call(
        _out_kernel,
        out_shape=[jax.ShapeDtypeStruct((T, D), f32), jax.ShapeDtypeStruct((T, D), bf16),
                   jax.ShapeDtypeStruct((T, 128), f32)],
        grid=(T // tm,),
        in_specs=[row(D), row(D), const(wout), const(gn), const(wr), const(br)],
        out_specs=[row(D), row(D), row(128)],
        compiler_params=params,
        name="out",
    )(x2, merged, wout, gn, wr, br)


def _moe_up_kernel(st, sc, se, sf, nv, xs_ref, w1g_ref, w1l_ref, b1g_ref, b1l_ref, *rest):
    o_ref, wg_ref, wl_ref = rest[-3:]
    s = pl.program_id(0)

    @pl.when(s < nv[0])
    def _():
        @pl.when(sf[s] == 1)
        def _():
            wg_ref[...] = w1g_ref[...].astype(bf16)
            wl_ref[...] = w1l_ref[...].astype(bf16)

        xs = xs_ref[...]
        hg = _dot(xs, wg_ref[...]) + b1g_ref[...]
        hl = _dot(xs, wl_ref[...]) + b1l_ref[...]
        x_glu = jnp.minimum(hg, SWIGLU_LIMIT)
        x_lin = jnp.clip(hl, -SWIGLU_LIMIT, SWIGLU_LIMIT)
        act = x_glu * jax.nn.sigmoid(SWIGLU_ALPHA * x_glu) * (x_lin + 1.0)
        o_ref[...] = act.astype(o_ref.dtype)


def _moe_down_kernel(st, sc, se, sf, nv, a_ref, w2_ref, b2_ref, o_ref, w_ref):
    s = pl.program_id(0)

    @pl.when(s < nv[0])
    def _():
        @pl.when(sf[s] == 1)
        def _():
            w_ref[...] = w2_ref[...].astype(bf16)

        o_ref[...] = (_dot(a_ref[...], w_ref[...]) + b2_ref[...]).astype(o_ref.dtype)


def _moe_schedule(tile_e, n_valid, first_tile, n_tiles_e, nc, lo, hi):
    i32 = jnp.int32
    E = first_tile.shape[0]
    experts = jnp.arange(E, dtype=i32)
    nv = jnp.clip(n_valid[0], lo, hi) - lo
    s = jnp.arange((hi - lo) * nc, dtype=i32)
    s = jnp.clip(s, 0, jnp.maximum(nc * nv - 1, 0))
    e = tile_e[lo:hi][s // nc]
    pick = lambda table: jnp.sum(jnp.where(e[:, None] == experts[None, :], table[None, :], 0), axis=1)
    ft = jnp.clip(first_tile, lo, hi)
    ne = jnp.clip(first_tile + n_tiles_e, lo, hi) - ft
    ft, ne = pick(ft), jnp.maximum(pick(ne), 1)
    local = s - nc * (ft - lo)
    col = local // ne
    row = ft + local % ne
    changed = jnp.logical_or(e != jnp.roll(e, 1), col != jnp.roll(col, 1)).at[0].set(True)
    return (row.astype(i32), col.astype(i32), e.astype(i32), changed.astype(i32),
            (nc * nv).astype(i32).reshape(1))


def _moe(tile_e, n_valid, first_tile, n_tiles_e, h2, tok, w1, b1, w2, b2, tm, tf=1024, tn=1024):
    P = tok.shape[0]
    D = h2.shape[1]
    E, _, F2 = w1.shape
    F = F2 // 2
    nj = F // tf
    nn = D // tn
    nt = P // tm
    params = pltpu.CompilerParams(dimension_semantics=("arbitrary",), vmem_limit_bytes=VMEM_LIMIT)

    bounds = sorted({(f * nt) // _MOE_CHUNK_TENTHS[-1] for f in _MOE_CHUNK_TENTHS})
    act = None
    for ck, (lo, hi) in enumerate(zip(bounds[:-1], bounds[1:])):
        ct = hi - lo
        sched = _moe_schedule(tile_e, n_valid, first_tile, n_tiles_e, nj, lo, hi)
        xs_k = h2[tok[lo * tm:hi * tm]]
        carry = [] if act is None else [act]
        act = pl.pallas_call(
            _moe_up_kernel,
            out_shape=jax.ShapeDtypeStruct((P, F), bf16),
            grid_spec=pltpu.PrefetchScalarGridSpec(
                num_scalar_prefetch=5,
                grid=(ct * nj,),
                in_specs=[
                    pl.BlockSpec((tm, D), lambda s, st, sc, se, sf, nv, lo=lo: (st[s] - lo, 0)),
                    pl.BlockSpec((None, D, tf), lambda s, st, sc, se, sf, nv: (se[s], 0, sc[s])),
                    pl.BlockSpec((None, D, tf), lambda s, st, sc, se, sf, nv: (se[s], 0, nj + sc[s])),
                    pl.BlockSpec((None, 1, tf), lambda s, st, sc, se, sf, nv: (se[s], 0, sc[s])),
                    pl.BlockSpec((None, 1, tf), lambda s, st, sc, se, sf, nv: (se[s], 0, nj + sc[s])),
                ] + [pl.BlockSpec(memory_space=pl.ANY)] * len(carry),
                out_specs=pl.BlockSpec((tm, tf), lambda s, st, sc, se, sf, nv: (st[s], sc[s])),
                scratch_shapes=[pltpu.VMEM((D, tf), bf16), pltpu.VMEM((D, tf), bf16)],
            ),
            input_output_aliases={10: 0} if carry else {},
            compiler_params=params,
            name=f"moe_up{ck}",
        )(*sched, xs_k, w1, w1, b1, b1, *carry)

    sched = _moe_schedule(tile_e, n_valid, first_tile, n_tiles_e, nn, 0, nt)
    return pl.pallas_call(
        _moe_down_kernel,
        out_shape=jax.ShapeDtypeStruct((P, D), bf16),
        grid_spec=pltpu.PrefetchScalarGridSpec(
            num_scalar_prefetch=5,
            grid=(nt * nn,),
            in_specs=[
                pl.BlockSpec((tm, F), lambda s, st, sc, se, sf, nv: (st[s], 0)),
                pl.BlockSpec((None, F, tn), lambda s, st, sc, se, sf, nv: (se[s], 0, sc[s])),
                pl.BlockSpec((None, 1, tn), lambda s, st, sc, se, sf, nv: (se[s], 0, sc[s])),
            ],
            out_specs=pl.BlockSpec((tm, tn), lambda s, st, sc, se, sf, nv: (st[s], sc[s])),
            scratch_shapes=[pltpu.VMEM((F, tn), bf16)],
        ),
        compiler_params=params,
        name="moe_down",
    )(*sched, act, w2, b2)


def _final_kernel(x1_ref, y_ref, gate_ref, g_ref, *rest, apply_norm):
    o_ref = rest[-1]
    gates = gate_ref[...]
    y = gates[:, 0:1] * y_ref[0].astype(f32)
    for kk in range(1, TOP_K):
        y = y + gates[:, kk:kk + 1] * y_ref[kk].astype(f32)
    x2 = x1_ref[...] + y
    if apply_norm:
        ms = jnp.mean(x2 * x2, axis=-1, keepdims=True)
        x2 = x2 * lax.rsqrt(ms + NORM_EPS) * g_ref[...]
    o_ref[...] = x2


def _final(x1, ys, pos, gates, g, apply_norm, tm=256):
    T, D = x1.shape
    n_chunks = _FINAL_CHUNKS if T % (_FINAL_CHUNKS * tm) == 0 else 1
    tc = T // n_chunks
    nb = tc // tm
    pos = pos.reshape(TOP_K, T)
    out = None
    for ck in range(n_chunks):
        y4 = ys[pos[:, ck * tc:(ck + 1) * tc].reshape(-1)].reshape(TOP_K, tc, D)
        carry = [] if out is None else [out]
        out = pl.pallas_call(
            functools.partial(_final_kernel, apply_norm=apply_norm),
            out_shape=jax.ShapeDtypeStruct((T, D), f32),
            grid=(nb,),
            in_specs=[pl.BlockSpec((tm, D), lambda i, o=ck * nb: (o + i, 0)),
                      pl.BlockSpec((TOP_K, tm, D), lambda i: (0, i, 0)),
                      pl.BlockSpec((tm, TOP_K), lambda i, o=ck * nb: (o + i, 0)),
                      pl.BlockSpec((1, D), lambda i: (0, 0))]
                     + [pl.BlockSpec(memory_space=pl.ANY)] * len(carry),
            out_specs=pl.BlockSpec((tm, D), lambda i, o=ck * nb: (o + i, 0)),
            input_output_aliases={4: 0} if carry else {},
            compiler_params=pltpu.CompilerParams(
                dimension_semantics=("arbitrary",), vmem_limit_bytes=VMEM_LIMIT),
            name=f"final{ck}",
        )(x1, y4, gates, g, *carry)
    return out


def _pad_cols(a, n):
    return jnp.pad(a, ((0, 0), (0, n - a.shape[1])))


def _pad_rows(a, n):
    return jnp.pad(a, ((0, n - a.shape[0]), (0, 0)))


def _regroup_cols(a):
    W = RWKV_WIDTH
    o = 3 * W
    return jnp.concatenate([
        a[:, :o],
        _pad_cols(a[:, o:o + DECAY_LORA], 128),
        _pad_cols(a[:, o + DECAY_LORA:o + DECAY_LORA + ICLR_LORA], 128),
        _pad_cols(a[:, o + DECAY_LORA + ICLR_LORA:o + DECAY_LORA + ICLR_LORA + GATE_LORA], 256),
        a[:, o + DECAY_LORA + ICLR_LORA + GATE_LORA:],
    ], axis=1)


def _rope_tables(positions):
    half = ROPE_DIM // 2
    inv_freq = ROPE_THETA ** (-jnp.arange(half, dtype=f32) / half)
    ang = positions.astype(f32)[..., None] * inv_freq
    cos, sin = jnp.cos(ang), jnp.sin(ang)
    B, S = positions.shape
    cosf = jnp.concatenate([cos, cos, jnp.ones((B, S, ATTN_E - ROPE_DIM), f32)], axis=-1)
    sinf = jnp.concatenate([-sin, sin, jnp.zeros((B, S, ATTN_E - ROPE_DIM), f32)], axis=-1)
    return cosf, sinf


def _route(logits, tm, n_tiles):
    T, E = logits.shape
    n = T * TOP_K
    i32 = jnp.int32
    top_val, top_idx = lax.top_k(logits, TOP_K)
    gates = jax.nn.softmax(top_val, axis=-1)
    flat_e = top_idx.reshape(-1).astype(i32)
    iota = jnp.arange(n, dtype=i32)
    experts = jnp.arange(E, dtype=i32)

    def lookup(table, idx):
        return jnp.sum(jnp.where(idx[:, None] == experts[None, :], table[None, :], 0), axis=1)

    _, order = lax.sort((flat_e, iota), num_keys=1, is_stable=True)
    _, rank = lax.sort((order, iota), num_keys=1)
    sizes = jnp.sum((flat_e[:, None] == experts[None, :]).astype(i32), axis=0)
    padded = ((sizes + tm - 1) // tm) * tm
    pad_end = jnp.cumsum(padded)
    pad_start = pad_end - padded
    start = jnp.cumsum(sizes) - sizes
    pos = lookup(pad_start - start, flat_e) + rank
    tile_start = jnp.arange(n_tiles, dtype=i32) * tm
    tile_e = jnp.minimum(jnp.sum((pad_end[None, :] <= tile_start[:, None]).astype(i32), axis=1), E - 1)
    n_valid = (pad_end[-1] // tm).astype(i32).reshape(1)
    row_e = jnp.repeat(tile_e, tm)
    rows = jnp.arange(n_tiles * tm, dtype=i32)
    off = rows - lookup(pad_start, row_e)
    src = jnp.clip(lookup(start, row_e) + off, 0, n - 1)
    tok = jnp.where(off < lookup(sizes, row_e), order[src] // TOP_K, rows % T)
    pos_slot_major = pos.reshape(T, TOP_K).T.reshape(-1)
    return tok, gates, pos_slot_major, tile_e, n_valid, pad_start // tm, padded // tm


def kernel(x, positions, norm_mix, w_in, shift_mu, w0, w2_decay, a0, a2_iclr, g2_gate, k_k, k_a,
           r_k, ln_x_w, ln_x_b, w_o_rwkv, w_o_attn, w_out, norm_ffn, w_router, b_router,
           w1, b1, w2, b2, norm_final):
    B, S, D = x.shape
    T = B * S
    E = w_router.shape[-1]
    depth = norm_mix.shape[0]
    xt = x.reshape(T, D)
    cosf, sinf = _rope_tables(positions)
    moe_tm = 512
    n_tiles = (T * TOP_K) // moe_tm + E
    for layer in range(depth):
        wp = _regroup_cols(w_in[layer]).astype(bf16)
        mu = _regroup_cols(shift_mu[layer][None, :])
        proj = _proj(xt, norm_mix[layer][None, :], wp)
        proj3 = proj.reshape(B, S, -1)
        rw = _rwkv(proj3, mu, w0[layer][None, :], _stack_rhs(_pad_rows(w2_decay[layer], 128)),
                   a0[layer][None, :], _stack_rhs(_pad_rows(a2_iclr[layer], 128)),
                   _pad_rows(g2_gate[layer], 256).astype(bf16), k_k[layer][None, :],
                   k_a[layer][None, :],
                   r_k[layer].reshape(1, -1), ln_x_w[layer][None, :], ln_x_b[layer][None, :])
        outs, lses = [], []
        for gi, (window, dil) in enumerate(ATTN_GROUPS):
            o_g, l_g = _attn_group(proj3, cosf, sinf, gi, dil)
            outs.append(o_g)
            lses.append(l_g)
        x1, h2, logits = _post(
            xt, rw.reshape(T, -1), outs, lses, proj,
            w_o_rwkv[layer].astype(bf16), w_o_attn[layer].astype(bf16), w_out[layer].astype(bf16),
            norm_ffn[layer][None, :], _stack_rhs(_pad_cols(w_router[layer], 128)),
            _pad_cols(b_router[layer][None, :], 128))
        tok, gates, pos, tile_e, n_valid, first_tile, n_tiles_e = _route(logits[:, :E], moe_tm, n_tiles)
        ys = _moe(tile_e, n_valid, first_tile, n_tiles_e, h2, tok, w1[layer], b1[layer][:, None, :],
                  w2[layer], b2[layer][:, None, :], moe_tm)
        xt = _final(x1, ys, pos, gates, norm_final[None, :], apply_norm=layer + 1 == depth)
    return xt.reshape(B, S, D)
```

```python
import functools

import jax
import jax.numpy as jnp
from jax import lax
from jax.experimental import pallas as pl
from jax.experimental.pallas import tpu as pltpu

f32 = jnp.float32
bf16 = jnp.bfloat16
HIGHEST = lax.Precision.HIGHEST

NORM_EPS = 1e-5
RWKV_GN_EPS = 64e-5
ROPE_THETA = 500000.0
SWIGLU_LIMIT = 7.0
SWIGLU_ALPHA = 1.702

RWKV_HEADS = 16
HEAD_N = 64
RWKV_WIDTH = RWKV_HEADS * HEAD_N
DECAY_LORA = 64
ICLR_LORA = 64
GATE_LORA = 160
ATTN_GROUPS = ((128, 1), (512, 4), (2048, 16))
HEADS_PER_GROUP = 4
ATTN_E = 128
ATTN_GW = HEADS_PER_GROUP * ATTN_E
ATTN_WIDTH = len(ATTN_GROUPS) * ATTN_GW
ROPE_DIM = ATTN_E // 4
TOP_K = 4

LORA_PAD = 512
ZW_OFF, ZA_OFF, ZG_OFF = 3 * RWKV_WIDTH, 3 * RWKV_WIDTH + 128, 3 * RWKV_WIDTH + 256
SHIFT_PAD = 3 * RWKV_WIDTH + LORA_PAD
ATTN_OFF = SHIFT_PAD
GATE_OFF = ATTN_OFF + 3 * ATTN_WIDTH

RWKV_CHUNK = 64
_MOE_CHUNK_TENTHS = (0, 1, 4, 7, 10)
_FINAL_CHUNKS = 1
_RWKV_SEQS = 2
_RWKV_NWIDE = 12
ATTN_BLK = 128
_STRIDE_SPLIT = 4
_ATTN_UNITS = 16
VMEM_LIMIT = 56 * 1024 * 1024


def _dot(a, b, dims=(((1,), (0,)), ((), ())), precision=None):
    return lax.dot_general(a, b, dims, precision=precision, preferred_element_type=f32)


_NT = (((1,), (1,)), ((), ()))
_TN = (((0,), (0,)), ((), ()))


def _split2(a):
    hi = a.astype(bf16)
    return hi, (a - hi.astype(f32)).astype(bf16)


def _split3(a):
    hi = a.astype(bf16)
    rem = a - hi.astype(f32)
    mid = rem.astype(bf16)
    return hi, mid, (rem - mid.astype(f32)).astype(bf16)


def _stack_lhs(a):
    hi, lo = _split2(a)
    return jnp.concatenate([hi, lo, hi], axis=1)


def _stack_rhs(w):
    hi, lo = _split2(w)
    return jnp.concatenate([hi, hi, lo], axis=0)


def _proj_kernel(x_ref, g_ref, w_ref, o_ref, h_ref):
    @pl.when(pl.program_id(1) == 0)
    def _():
        x = x_ref[...]
        ms = jnp.mean(x * x, axis=-1, keepdims=True)
        h_ref[...] = (x * lax.rsqrt(ms + NORM_EPS) * g_ref[...]).astype(bf16)

    o_ref[...] = _dot(h_ref[...], w_ref[...])


def _proj(x2, g, wp, tm=512, tn=2048):
    T, D = x2.shape
    NP = wp.shape[1]
    return pl.pallas_call(
        _proj_kernel,
        out_shape=jax.ShapeDtypeStruct((T, NP), f32),
        grid=(T // tm, NP // tn),
        in_specs=[pl.BlockSpec((tm, D), lambda i, j: (i, 0)),
                  pl.BlockSpec((1, D), lambda i, j: (0, 0)),
                  pl.BlockSpec((D, tn), lambda i, j: (0, j))],
        out_specs=pl.BlockSpec((tm, tn), lambda i, j: (i, j)),
        scratch_shapes=[pltpu.VMEM((tm, D), bf16)],
        compiler_params=pltpu.CompilerParams(
            dimension_semantics=("arbitrary", "arbitrary"), vmem_limit_bytes=VMEM_LIMIT),
        name="proj",
    )(x2, g, wp)


def _rwkv_kernel(p_ref, mu_ref, w0_ref, w2_ref, a0_ref, a2_ref, g2_ref, kk_ref, ka_ref,
                 rk_ref, lnw_ref, lnb_ref, o_ref, state_ref, prev_ref, dec_ref, wide_ref):
    (at0_ref, at_ref, rt0_ref, rt_ref, bt_ref, kt_ref, bh_ref, kh_ref, v_ref, rkb_ref, g_ref,
     kkr_ref) = [wide_ref.at[i] for i in range(_RWKV_NWIDE)]
    L = RWKV_CHUNK
    W = RWKV_WIDTH
    c_idx = pl.program_id(1)

    @pl.when(c_idx == 0)
    def _():
        state_ref[...] = jnp.zeros_like(state_ref)
        prev_ref[...] = jnp.zeros_like(prev_ref)

    RB = p_ref.shape[0]
    RL = RB * L
    p = p_ref[...].reshape(RL, p_ref.shape[2])
    row = lax.broadcasted_iota(jnp.int32, (RL, 1), 0)
    shifted = pltpu.roll(p, 1, axis=0)
    for bi in range(RB):
        shifted = jnp.where(row == bi * L, prev_ref[bi:bi + 1, :], shifted)
        prev_ref[bi:bi + 1, :] = p[bi * L + L - 1:bi * L + L, :]
    z = p + (shifted - p) * mu_ref[...]

    def per_seq(x, r):
        return jnp.concatenate(
            [jnp.broadcast_to(x[bi * L + r:bi * L + r + 1, :], (L, x.shape[1])) for bi in range(RB)],
            axis=0)

    r = z[:, 0:W]
    k = z[:, W:2 * W]
    v = z[:, 2 * W:3 * W]
    zw = z[:, ZW_OFF:ZW_OFF + 128]
    za = z[:, ZA_OFF:ZA_OFF + 128]
    zg = z[:, ZG_OFF:ZG_OFF + 256]

    w_raw = w0_ref[...] + _dot(_stack_lhs(jnp.tanh(zw)), w2_ref[...])
    sp = jnp.maximum(-w_raw, 0.0) + jnp.log(1.0 + jnp.exp(-jnp.abs(w_raw)))
    lw = -jnp.exp(-sp - 0.5)
    a = jax.nn.sigmoid(a0_ref[...] + _dot(_stack_lhs(za), a2_ref[...]))
    g_ref[...] = _dot(jax.nn.sigmoid(zg).astype(bf16), g2_ref[...])

    ti = lax.broadcasted_iota(jnp.int32, (RL, RL), 0)
    si = lax.broadcasted_iota(jnp.int32, (RL, RL), 1)
    tri = jnp.logical_and(ti >= si, ti // L == si // L).astype(bf16)
    lw_h, lw_m, lw_l = _split3(lw)
    c = _dot(jnp.concatenate([tri, tri, tri], axis=1), jnp.concatenate([lw_h, lw_m, lw_l], axis=0))
    cex = c - lw
    cm = per_seq(c, L // 2 - 1)
    cl = per_seq(c, L - 1)
    e_m = jnp.exp(-cm)
    e_c = jnp.exp(c)
    e_ex = jnp.exp(cex)
    e_inv = jnp.exp(cm - c)
    e_tail = jnp.exp(cl - c)

    kkv = k * kk_ref[...]
    kmod = k * (1.0 + (a - 1.0) * ka_ref[...])
    at0_ref[...] = e_ex
    rt0 = r * e_c
    rt0_ref[...] = rt0
    rt_ref[...] = rt0 * e_m
    kt_ref[...] = kmod * e_inv
    kh_ref[...] = kmod * e_tail
    bt_ref[...] = a * e_inv
    bh_ref[...] = a * e_tail
    at_ref[...] = e_ex * e_m
    v_ref[...] = v
    for bi in range(RB):
        dec_ref[bi:bi + 1, :] = jnp.exp(c[bi * L + L - 1:bi * L + L, :])
    rkb_ref[...] = r * kmod * rk_ref[...]
    kkr_ref[...] = kkv

    PW = 2 * HEAD_N
    pairs = range(RWKV_HEADS // 2)
    lane = lax.broadcasted_iota(jnp.int32, (L, PW), 1)
    trow = lax.broadcasted_iota(jnp.int32, (L, PW), 0)
    first = lane < HEAD_N
    scol = jnp.where(first, lane, lane - HEAD_N)
    strict_p = trow > scol
    incl_p = trow >= scol
    eye_p = (trow == scol).astype(f32)
    rb =lax.broadcasted_iota(jnp.int32, (PW, PW), 0)
    lb = lax.broadcasted_iota(jnp.int32, (PW, PW), 1)
    bd_mask = (rb < HEAD_N) == (lb < HEAD_N)

    def bd(x):
        zero = jnp.zeros_like(x)
        return jnp.concatenate([jnp.where(first, x, zero), jnp.where(first, zero, x)], axis=0)

    def segsum(x):
        s0 = jnp.sum(jnp.where(first, x, 0.0), axis=-1, keepdims=True)
        s1 = jnp.sum(jnp.where(first, 0.0, x), axis=-1, keepdims=True)
        return jnp.where(first, s0, s1)

    units = [(bi, p) for bi in range(RB) for p in pairs]
    ps = [(slice(bi * L, (bi + 1) * L), slice(p * PW, (p + 1) * PW)) for bi, p in units]
    kkr = [kkr_ref[s] for s in ps]
    ssq = [segsum(x * x) for x in kkr]
    kkh = [x / jnp.maximum(jnp.sqrt(q), 1e-12) for x, q in zip(kkr, ssq)]
    at0 = [(-kh_ * at0_ref[s]).astype(bf16) for kh_, s in zip(kkh, ps)]
    at = [(-kh_ * at_ref[s]).astype(bf16) for kh_, s in zip(kkh, ps)]
    bt = [(kh_ * bt_ref[s]).astype(bf16) for kh_, s in zip(kkh, ps)]
    bh = [(kh_ * bh_ref[s]).astype(bf16) for kh_, s in zip(kkh, ps)]
    vb = [v_ref[s].astype(bf16) for s in ps]

    lhs = [jnp.concatenate([a_, rt_ref[s].astype(bf16)], axis=0) for a_, s in zip(at, ps)]
    sb = [_dot(l_, bd(b_), _NT) for l_, b_ in zip(lhs, bt)]
    sk = [_dot(l_, bd(kt_ref[s].astype(bf16)), _NT) for l_, s in zip(lhs, ps)]
    a_ab = [jnp.where(strict_p, x[:L], 0.0) for x in sb]
    a_rb = [jnp.where(incl_p, x[L:], 0.0).astype(bf16) for x in sb]
    a_ak = [jnp.where(strict_p, x[:L], 0.0).astype(bf16) for x in sk]
    a_rk = [jnp.where(incl_p, x[L:], 0.0).astype(bf16) for x in sk]

    tinv = [eye_p + x for x in a_ab]
    xp = a_ab
    n = 2
    while n < L:
        xb = [x.astype(bf16) for x in xp]
        xp = [_dot(x, bd(x)) for x in xb]
        tinv = [t + _dot(t.astype(bf16), bd(x.astype(bf16))) for t, x in zip(tinv, xp)]
        n *= 2
    tb = [t.astype(bf16) for t in tinv]

    akv = [_dot(a_, bd(v_)).astype(bf16) for a_, v_ in zip(a_ak, vb)]
    w12 = [_dot(t, jnp.concatenate([bd(a_), bd(k_)], axis=1)) for t, a_, k_ in zip(tb, at0, akv)]

    n_pairs = len(pairs)
    s0 = [state_ref[bi * n_pairs + p] for bi, p in units]
    s0b = [s.astype(bf16) for s in s0]
    uy = [_dot(jnp.concatenate([w[:, :PW].astype(bf16), rt0_ref[s].astype(bf16)], axis=0), sb_, _NT)
          for w, s, sb_ in zip(w12, ps, s0b)]
    u = [x[:L] + w[:, PW:] for x, w in zip(uy, w12)]
    ub = [x.astype(bf16) for x in u]
    y = [x[L:] + _dot(jnp.concatenate([rb_, rk_], axis=1), jnp.concatenate([bd(u_), bd(v_)], axis=0))
         for x, rb_, rk_, u_, v_ in zip(uy, a_rb, a_rk, ub, vb)]
    for i, (bi, p) in enumerate(units):
        upd = _dot(jnp.concatenate([ub[i], vb[i]], axis=0),
                   jnp.concatenate([bh[i], kh_ref[ps[i]].astype(bf16)], axis=0), _TN)
        state_ref[bi * n_pairs + p] = (s0[i] * dec_ref[bi:bi + 1, ps[i][1]]
                                       + jnp.where(bd_mask, upd, 0.0))

    inv_n = 1.0 / HEAD_N
    mean = [segsum(x) * inv_n for x in y]
    yc = [x - m for x, m in zip(y, mean)]
    var = [segsum(x * x) * inv_n for x in yc]
    bonus = [segsum(rkb_ref[s]) for s in ps]
    for i, (bi, p) in enumerate(units):
        s = ps[i]
        yn = yc[i] * lax.rsqrt(var[i] + RWKV_GN_EPS) * lnw_ref[:, s[1]] + lnb_ref[:, s[1]]
        o_ref[bi, :, s[1]] = ((yn + bonus[i] * v_ref[s]) * g_ref[s]).astype(o_ref.dtype)


def _rwkv(proj3, mu, w0, w2d, a0, a2, g2, k_k, k_a, r_k, ln_w, ln_b):
    B, S, NP = proj3.shape
    L, W = RWKV_CHUNK, RWKV_WIDTH
    row = lambda n: pl.BlockSpec((1, n), lambda b, c: (0, 0))
    full = lambda a: pl.BlockSpec(a.shape, lambda b, c: (0, 0))
    RB = _RWKV_SEQS if B % _RWKV_SEQS == 0 else 1
    return pl.pallas_call(
        _rwkv_kernel,
        out_shape=jax.ShapeDtypeStruct((B, S, W), bf16),
        grid=(B // RB, S // L),
        in_specs=[pl.BlockSpec((RB, L, SHIFT_PAD), lambda b, c: (b, c, 0)),
                  row(SHIFT_PAD), row(W), full(w2d), row(W), full(a2), full(g2),
                  row(W), row(W), row(W), row(W), row(W)],
        out_specs=pl.BlockSpec((RB, L, W), lambda b, c: (b, c, 0)),
        scratch_shapes=[pltpu.VMEM((RB * RWKV_HEADS // 2, 2 * HEAD_N, 2 * HEAD_N), f32),
                        pltpu.VMEM((RB, SHIFT_PAD), f32), pltpu.VMEM((RB, W), f32),
                        pltpu.VMEM((_RWKV_NWIDE, RB * L, W), f32)],
        compiler_params=pltpu.CompilerParams(
            dimension_semantics=("arbitrary", "arbitrary"), vmem_limit_bytes=VMEM_LIMIT),
        name="rwkv",
    )(proj3, mu, w0, w2d, a0, a2, g2, k_k, k_a, r_k, ln_w, ln_b)


def _rope(x, cosf, sinf, lane):
    rot = jnp.where(lane < ROPE_DIM // 2, pltpu.roll(x, ATTN_E - ROPE_DIM // 2, axis=1),
                    pltpu.roll(x, ROPE_DIM // 2, axis=1))
    return x * cosf + rot * sinf


def _attn_kernel(q_ref, kc_ref, kp_ref, vc_ref, vp_ref, cc_ref, sc_ref, cp_ref, sp_ref,
                 o_ref, l_ref, *stage, dil, hps, nblk, wave):
    n = pl.program_id(1)
    blk = ATTN_BLK
    span = blk * dil
    qi = lax.broadcasted_iota(jnp.int32, (blk, blk), 0)
    kj = lax.broadcasted_iota(jnp.int32, (blk, blk), 1)
    lane = lax.broadcasted_iota(jnp.int32, (blk, ATTN_E), 1)
    cur_ok = kj <= qi
    prev_ok = kj >= qi
    first_ok = jnp.logical_and(prev_ok, n > 0)
    scale = ATTN_E ** -0.5

    ins = (q_ref, kc_ref, kp_ref, vc_ref, vp_ref, cc_ref, sc_ref, cp_ref, sp_ref)
    outs = (o_ref, l_ref)
    Q, KC, KP, VC, VP, CC, SC, CP, SP = range(len(ins))
    sub = dil // _STRIDE_SPLIT if stage else dil

    def rows(b, r, d):
        return pl.ds(b * span + r, blk, stride=d) if d > 1 else pl.ds(b * span, blk)

    def cols(hh):
        return slice(hh * ATTN_E, (hh + 1) * ATTN_E)

    if stage:
        in_stage, out_stage = stage
        quarter = span // _STRIDE_SPLIT
        for xi, ref in enumerate(ins):
            for r4 in range(_STRIDE_SPLIT):
                in_stage[xi, r4] = ref[0, pl.ds(r4, quarter, stride=_STRIDE_SPLIT), :]

        def rd(xi, b, r, hh):
            return in_stage[xi, r % _STRIDE_SPLIT, rows(0, r // _STRIDE_SPLIT, sub), :]

        def wr(oi, b, r, hh, val):
            out_stage[oi, r % _STRIDE_SPLIT, rows(0, r // _STRIDE_SPLIT, sub), :] = val
    else:
        def rd(xi, b, r, hh):
            return ins[xi][0, rows(b, r, dil), cols(hh) if xi < CC else slice(None)]

        def wr(oi, b, r, hh, val):
            outs[oi][0, rows(b, r, dil), cols(hh)] = val

    all_units = [(b, r, hh) for b in range(nblk) for r in range(dil) for hh in range(hps)]
    kc, vc = {}, {}
    for w0 in range(0, len(all_units), wave):
        units = all_units[w0:w0 + wave]
        tab = {u[:2]: (rd(CC, *u), rd(SC, *u)) for u in units}
        q = {u: _rope(rd(Q, *u), *tab[u[:2]], lane).astype(bf16) for u in units}
        for u in units:
            kc[u] = _rope(rd(KC, *u), *tab[u[:2]], lane).astype(bf16)
            vc[u] = rd(VC, *u).astype(bf16)
        kp, vp = {}, {}
        for (b, r, hh) in units:
            if b == 0:
                ptab = (rd(CP, 0, r, hh), rd(SP, 0, r, hh))
                kp[(b, r, hh)] = _rope(rd(KP, 0, r, hh), *ptab, lane).astype(bf16)
                vp[(b, r, hh)] = rd(VP, 0, r, hh).astype(bf16)
            else:
                kp[(b, r, hh)] = kc[(b - 1, r, hh)]
                vp[(b, r, hh)] = vc[(b - 1, r, hh)]
        s_c = {u: jnp.where(cur_ok, _dot(q[u], kc[u], _NT) * scale, -jnp.inf) for u in units}
        s_p = {u: jnp.where(first_ok if u[0] == 0 else prev_ok, _dot(q[u], kp[u], _NT) * scale, -jnp.inf)
               for u in units}
        m = {u: jnp.maximum(jnp.max(s_c[u], axis=-1, keepdims=True),
                            jnp.max(s_p[u], axis=-1, keepdims=True)) for u in units}
        p_c = {u: jnp.exp(s_c[u] - m[u]) for u in units}
        p_p = {u: jnp.exp(s_p[u] - m[u]) for u in units}
        l = {u: jnp.sum(p_c[u], axis=-1, keepdims=True) + jnp.sum(p_p[u], axis=-1, keepdims=True)
             for u in units}
        acc = {u: _dot(p_c[u].astype(bf16), vc[u]) + _dot(p_p[u].astype(bf16), vp[u]) for u in units}
        for u in units:
            wr(0, *u, acc[u] / l[u])
            wr(1, *u, jnp.broadcast_to(m[u] + jnp.log(l[u]), (blk, ATTN_E)))
    if stage:
        for oi, ref in enumerate(outs):
            for r4 in range(_STRIDE_SPLIT):
                ref[0, pl.ds(r4, quarter, stride=_STRIDE_SPLIT), :] = out_stage[oi, r4]


def _attn_group(proj3, cosf, sinf, gi, dil):
    B, S, NP = proj3.shape
    span = ATTN_BLK * dil
    hps = HEADS_PER_GROUP if dil == 1 else 1
    nblk = _ATTN_UNITS // (dil * hps)
    R = span * nblk
    cw = hps * ATTN_E
    per = ATTN_GW // cw
    qo = (ATTN_OFF + gi * ATTN_GW) // cw
    ko = qo + ATTN_WIDTH // cw
    vo = ko + ATTN_WIDTH // cw

    def cur(off):
        return pl.BlockSpec((1, R, cw), lambda b, n, h: (b, n, off + h))

    def prev(off):
        return pl.BlockSpec((1, span, cw), lambda b, n, h: (b, jnp.maximum(n * nblk - 1, 0), off + h))

    tc = pl.BlockSpec((1, R, ATTN_E), lambda b, n, h: (b, n, 0))
    tp = pl.BlockSpec((1, span, ATTN_E), lambda b, n, h: (b, jnp.maximum(n * nblk - 1, 0), 0))
    out = pl.BlockSpec((1, R, cw), lambda b, n, h: (b, n, h))
    staged = dil > _STRIDE_SPLIT and nblk == 1 and hps == 1
    scratch = ([pltpu.VMEM((9, _STRIDE_SPLIT, R // _STRIDE_SPLIT, ATTN_E), f32),
                pltpu.VMEM((2, _STRIDE_SPLIT, R // _STRIDE_SPLIT, ATTN_E), f32)] if staged else [])
    o, l = pl.pallas_call(
        functools.partial(_attn_kernel, dil=dil, hps=hps, nblk=nblk,
                          wave=_ATTN_UNITS // 2 if dil > HEADS_PER_GROUP else _ATTN_UNITS),
        out_shape=[jax.ShapeDtypeStruct((B, S, ATTN_GW), f32)] * 2,
        grid=(B, S // R, per),
        in_specs=[cur(qo), cur(ko), prev(ko), cur(vo), prev(vo), tc, tc, tp, tp],
        out_specs=[out, out],
        scratch_shapes=scratch,
        compiler_params=pltpu.CompilerParams(
            dimension_semantics=("arbitrary", "arbitrary", "arbitrary"),
            vmem_limit_bytes=VMEM_LIMIT),
        name=f"attn_g{gi}",
    )(proj3, proj3, proj3, proj3, proj3, cosf, sinf, cosf, sinf)
    return o.reshape(B * S, ATTN_GW), l.reshape(B * S, ATTN_GW)


def _merge_kernel(rw_ref, o0_ref, o1_ref, o2_ref, l0_ref, l1_ref, l2_ref, ga_ref, gb_ref,
                  wor_ref, woa_ref, mg_ref):
    l0, l1, l2 = l0_ref[...], l1_ref[...], l2_ref[...]
    m = jnp.maximum(jnp.maximum(l0, l1), l2)
    e0, e1, e2 = jnp.exp(l0 - m), jnp.exp(l1 - m), jnp.exp(l2 - m)
    o = (e0 * o0_ref[...] + e1 * o1_ref[...] + e2 * o2_ref[...]) / (e0 + e1 + e2)
    y_b = _dot(o.astype(bf16), woa_ref[...])
    y_a = _dot(rw_ref[...], wor_ref[...])
    merged = jax.nn.sigmoid(ga_ref[...]) * y_a + jax.nn.sigmoid(gb_ref[...]) * y_b
    mg_ref[...] = merged.astype(bf16)


def _out_kernel(x_ref, mg_ref, wout_ref, gn_ref, wr_ref, br_ref, x1_ref, h2_ref, lg_ref):
    x1 = x_ref[...] + _dot(mg_ref[...], wout_ref[...])
    x1_ref[...] = x1
    ms = jnp.mean(x1 * x1, axis=-1, keepdims=True)
    h2 = x1 * lax.rsqrt(ms + NORM_EPS) * gn_ref[...]
    h2_ref[...] = h2.astype(bf16)
    hi, lo = _split2(h2)
    n_e = lg_ref.shape[1]
    both = _dot(hi, wr_ref[...])
    lg_ref[...] = both[:, :n_e] + both[:, n_e:] + _dot(lo, wr_ref[:, :n_e]) + br_ref[...]


def _post(x2, rw, outs, lses, proj, wor, woa, wout, gn, wr, br, tm=512):
    T, D = x2.shape
    row = lambda w: pl.BlockSpec((tm, w), lambda i: (i, 0))
    const = lambda a: pl.BlockSpec(a.shape, lambda i: (0, 0), pipeline_mode=pl.Buffered(1))
    ga_blk = GATE_OFF // D
    params = pltpu.CompilerParams(dimension_semantics=("arbitrary",), vmem_limit_bytes=VMEM_LIMIT)
    merged = pl.pallas_call(
        _merge_kernel,
        out_shape=jax.ShapeDtypeStruct((T, D), bf16),
        grid=(T // tm,),
        in_specs=[row(RWKV_WIDTH)] + [row(ATTN_GW)] * 6
                 + [pl.BlockSpec((tm, D), lambda i: (i, ga_blk)),
                    pl.BlockSpec((tm, D), lambda i: (i, ga_blk + 1)),
                    const(wor), const(woa)],
        out_specs=row(D),
        compiler_params=params,
        name="merge",
    )(rw, *outs, *lses, proj, proj, wor, woa)
    return pl.pallas_call(
        _out_kernel,
        out_shape=[jax.ShapeDtypeStruct((T, D), f32), jax.ShapeDtypeStruct((T, D), bf16),
                   jax.ShapeDtypeStruct((T, 128), f32)],
        grid=(T // tm,),
        in_specs=[row(D), row(D), const(wout), const(gn), const(wr), const(br)],
        out_specs=[row(D), row(D), row(128)],
        compiler_params=params,
        name="out",
    )(x2, merged, wout, gn, wr, br)


def _moe_up_kernel(st, sc, se, sf, nv, xs_ref, w1g_ref, w1l_ref, b1g_ref, b1l_ref, *rest):
    o_ref, wg_ref, wl_ref = rest[-3:]
    s = pl.program_id(0)

    @pl.when(s < nv[0])
    def _():
        @pl.when(sf[s] == 1)
        def _():
            wg_ref[...] = w1g_ref[...].astype(bf16)
            wl_ref[...] = w1l_ref[...].astype(bf16)

        xs = xs_ref[...]
        hg = _dot(xs, wg_ref[...]) + b1g_ref[...]
        hl = _dot(xs, wl_ref[...]) + b1l_ref[...]
        x_glu = jnp.minimum(hg, SWIGLU_LIMIT)
        x_lin = jnp.clip(hl, -SWIGLU_LIMIT, SWIGLU_LIMIT)
        act = x_glu * jax.nn.sigmoid(SWIGLU_ALPHA * x_glu) * (x_lin + 1.0)
        o_ref[...] = act.astype(o_ref.dtype)


def _moe_down_kernel(st, sc, se, sf, nv, a_ref, w2_ref, b2_ref, o_ref, w_ref):
    s = pl.program_id(0)

    @pl.when(s < nv[0])
    def _():
        @pl.when(sf[s] == 1)
        def _():
            w_ref[...] = w2_ref[...].astype(bf16)

        o_ref[...] = (_dot(a_ref[...], w_ref[...]) + b2_ref[...]).astype(o_ref.dtype)


def _moe_schedule(tile_e, n_valid, first_tile, n_tiles_e, nc, lo, hi):
    i32 = jnp.int32
    E = first_tile.shape[0]
    experts = jnp.arange(E, dtype=i32)
    nv = jnp.clip(n_valid[0], lo, hi) - lo
    s = jnp.arange((hi - lo) * nc, dtype=i32)
    s = jnp.clip(s, 0, jnp.maximum(nc * nv - 1, 0))
    e = tile_e[lo:hi][s // nc]
    pick = lambda table: jnp.sum(jnp.where(e[:, None] == experts[None, :], table[None, :], 0), axis=1)
    ft = jnp.clip(first_tile, lo, hi)
    ne = jnp.clip(first_tile + n_tiles_e, lo, hi) - ft
    ft, ne = pick(ft), jnp.maximum(pick(ne), 1)
    local = s - nc * (ft - lo)
    col = local // ne
    row = ft + local % ne
    changed = jnp.logical_or(e != jnp.roll(e, 1), col != jnp.roll(col, 1)).at[0].set(True)
    return (row.astype(i32), col.astype(i32), e.astype(i32), changed.astype(i32),
            (nc * nv).astype(i32).reshape(1))


def _moe(tile_e, n_valid, first_tile, n_tiles_e, h2, tok, w1, b1, w2, b2, tm, tf=1024, tn=2048):
    P = tok.shape[0]
    D = h2.shape[1]
    E, _, F2 = w1.shape
    F = F2 // 2
    nj = F // tf
    nn = D // tn
    nt = P // tm
    params = pltpu.CompilerParams(dimension_semantics=("arbitrary",), vmem_limit_bytes=VMEM_LIMIT)

    bounds = sorted({(f * nt) // _MOE_CHUNK_TENTHS[-1] for f in _MOE_CHUNK_TENTHS})
    act = None
    for ck, (lo, hi) in enumerate(zip(bounds[:-1], bounds[1:])):
        ct = hi - lo
        sched = _moe_schedule(tile_e, n_valid, first_tile, n_tiles_e, nj, lo, hi)
        xs_k = h2[tok[lo * tm:hi * tm]]
        carry = [] if act is None else [act]
        act = pl.pallas_call(
            _moe_up_kernel,
            out_shape=jax.ShapeDtypeStruct((P, F), bf16),
            grid_spec=pltpu.PrefetchScalarGridSpec(
                num_scalar_prefetch=5,
                grid=(ct * nj,),
                in_specs=[
                    pl.BlockSpec((tm, D), lambda s, st, sc, se, sf, nv, lo=lo: (st[s] - lo, 0)),
                    pl.BlockSpec((None, D, tf), lambda s, st, sc, se, sf, nv: (se[s], 0, sc[s])),
                    pl.BlockSpec((None, D, tf), lambda s, st, sc, se, sf, nv: (se[s], 0, nj + sc[s])),
                    pl.BlockSpec((None, 1, tf), lambda s, st, sc, se, sf, nv: (se[s], 0, sc[s])),
                    pl.BlockSpec((None, 1, tf), lambda s, st, sc, se, sf, nv: (se[s], 0, nj + sc[s])),
                ] + [pl.BlockSpec(memory_space=pl.ANY)] * len(carry),
                out_specs=pl.BlockSpec((tm, tf), lambda s, st, sc, se, sf, nv: (st[s], sc[s])),
                scratch_shapes=[pltpu.VMEM((D, tf), bf16), pltpu.VMEM((D, tf), bf16)],
            ),
            input_output_aliases={10: 0} if carry else {},
            compiler_params=params,
            name=f"moe_up{ck}",
        )(*sched, xs_k, w1, w1, b1, b1, *carry)

    sched = _moe_schedule(tile_e, n_valid, first_tile, n_tiles_e, nn, 0, nt)
    return pl.pallas_call(
        _moe_down_kernel,
        out_shape=jax.ShapeDtypeStruct((P, D), bf16),
        grid_spec=pltpu.PrefetchScalarGridSpec(
            num_scalar_prefetch=5,
            grid=(nt * nn,),
            in_specs=[
                pl.BlockSpec((tm, F), lambda s, st, sc, se, sf, nv: (st[s], 0)),
                pl.BlockSpec((None, F, tn), lambda s, st, sc, se, sf, nv: (se[s], 0, sc[s])),
                pl.BlockSpec((None, 1, tn), lambda s, st, sc, se, sf, nv: (se[s], 0, sc[s])),
            ],
            out_specs=pl.BlockSpec((tm, tn), lambda s, st, sc, se, sf, nv: (st[s], sc[s])),
            scratch_shapes=[pltpu.VMEM((F, tn), bf16)],
        ),
        compiler_params=params,
        name="moe_down",
    )(*sched, act, w2, b2)


def _final_kernel(x1_ref, y_ref, gate_ref, g_ref, *rest, apply_norm):
    o_ref = rest[-1]
    gates = gate_ref[...]
    y = gates[:, 0:1] * y_ref[0].astype(f32)
    for kk in range(1, TOP_K):
        y = y + gates[:, kk:kk + 1] * y_ref[kk].astype(f32)
    x2 = x1_ref[...] + y
    if apply_norm:
        ms = jnp.mean(x2 * x2, axis=-1, keepdims=True)
        x2 = x2 * lax.rsqrt(ms + NORM_EPS) * g_ref[...]
    o_ref[...] = x2


def _final(x1, ys, pos, gates, g, apply_norm, tm=256):
    T, D = x1.shape
    n_chunks = _FINAL_CHUNKS if T % (_FINAL_CHUNKS * tm) == 0 else 1
    tc = T // n_chunks
    nb = tc // tm
    pos = pos.reshape(TOP_K, T)
    out = None
    for ck in range(n_chunks):
        y4 = ys[pos[:, ck * tc:(ck + 1) * tc].reshape(-1)].reshape(TOP_K, tc, D)
        carry = [] if out is None else [out]
        out = pl.pallas_call(
            functools.partial(_final_kernel, apply_norm=apply_norm),
            out_shape=jax.ShapeDtypeStruct((T, D), f32),
            grid=(nb,),
            in_specs=[pl.BlockSpec((tm, D), lambda i, o=ck * nb: (o + i, 0)),
                      pl.BlockSpec((TOP_K, tm, D), lambda i: (0, i, 0)),
                      pl.BlockSpec((tm, TOP_K), lambda i, o=ck * nb: (o + i, 0)),
                      pl.BlockSpec((1, D), lambda i: (0, 0))]
                     + [pl.BlockSpec(memory_space=pl.ANY)] * len(carry),
            out_specs=pl.BlockSpec((tm, D), lambda i, o=ck * nb: (o + i, 0)),
            input_output_aliases={4: 0} if carry else {},
            compiler_params=pltpu.CompilerParams(
                dimension_semantics=("arbitrary",), vmem_limit_bytes=VMEM_LIMIT),
            name=f"final{ck}",
        )(x1, y4, gates, g, *carry)
    return out


def _pad_cols(a, n):
    return jnp.pad(a, ((0, 0), (0, n - a.shape[1])))


def _pad_rows(a, n):
    return jnp.pad(a, ((0, n - a.shape[0]), (0, 0)))


def _regroup_cols(a):
    W = RWKV_WIDTH
    o = 3 * W
    return jnp.concatenate([
        a[:, :o],
        _pad_cols(a[:, o:o + DECAY_LORA], 128),
        _pad_cols(a[:, o + DECAY_LORA:o + DECAY_LORA + ICLR_LORA], 128),
        _pad_cols(a[:, o + DECAY_LORA + ICLR_LORA:o + DECAY_LORA + ICLR_LORA + GATE_LORA], 256),
        a[:, o + DECAY_LORA + ICLR_LORA + GATE_LORA:],
    ], axis=1)


def _rope_tables(positions):
    half = ROPE_DIM // 2
    inv_freq = ROPE_THETA ** (-jnp.arange(half, dtype=f32) / half)
    ang = positions.astype(f32)[..., None] * inv_freq
    cos, sin = jnp.cos(ang), jnp.sin(ang)
    B, S = positions.shape
    cosf = jnp.concatenate([cos, cos, jnp.ones((B, S, ATTN_E - ROPE_DIM), f32)], axis=-1)
    sinf = jnp.concatenate([-sin, sin, jnp.zeros((B, S, ATTN_E - ROPE_DIM), f32)], axis=-1)
    return cosf, sinf


def _route(logits, tm, n_tiles):
    T, E = logits.shape
    n = T * TOP_K
    i32 = jnp.int32
    top_val, top_idx = lax.top_k(logits, TOP_K)
    gates = jax.nn.softmax(top_val, axis=-1)
    flat_e = top_idx.reshape(-1).astype(i32)
    iota = jnp.arange(n, dtype=i32)
    experts = jnp.arange(E, dtype=i32)

    def lookup(table, idx):
        return jnp.sum(jnp.where(idx[:, None] == experts[None, :], table[None, :], 0), axis=1)

    _, order = lax.sort((flat_e, iota), num_keys=1, is_stable=True)
    _, rank = lax.sort((order, iota), num_keys=1)
    sizes = jnp.sum((flat_e[:, None] == experts[None, :]).astype(i32), axis=0)
    padded = ((sizes + tm - 1) // tm) * tm
    pad_end = jnp.cumsum(padded)
    pad_start = pad_end - padded
    start = jnp.cumsum(sizes) - sizes
    pos = lookup(pad_start - start, flat_e) + rank
    tile_start = jnp.arange(n_tiles, dtype=i32) * tm
    tile_e = jnp.minimum(jnp.sum((pad_end[None, :] <= tile_start[:, None]).astype(i32), axis=1), E - 1)
    n_valid = (pad_end[-1] // tm).astype(i32).reshape(1)
    row_e = jnp.repeat(tile_e, tm)
    rows = jnp.arange(n_tiles * tm, dtype=i32)
    off = rows - lookup(pad_start, row_e)
    src = jnp.clip(lookup(start, row_e) + off, 0, n - 1)
    tok = jnp.where(off < lookup(sizes, row_e), order[src] // TOP_K, rows % T)
    pos_slot_major = pos.reshape(T, TOP_K).T.reshape(-1)
    return tok, gates, pos_slot_major, tile_e, n_valid, pad_start // tm, padded // tm


def kernel(x, positions, norm_mix, w_in, shift_mu, w0, w2_decay, a0, a2_iclr, g2_gate, k_k, k_a,
           r_k, ln_x_w, ln_x_b, w_o_rwkv, w_o_attn, w_out, norm_ffn, w_router, b_router,
           w1, b1, w2, b2, norm_final):
    B, S, D = x.shape
    T = B * S
    E = w_router.shape[-1]
    depth = norm_mix.shape[0]
    xt = x.reshape(T, D)
    cosf, sinf = _rope_tables(positions)
    moe_tm = 512
    n_tiles = (T * TOP_K) // moe_tm + E
    for layer in range(depth):
        wp = _regroup_cols(w_in[layer]).astype(bf16)
        mu = _regroup_cols(shift_mu[layer][None, :])
        proj = _proj(xt, norm_mix[layer][None, :], wp)
        proj3 = proj.reshape(B, S, -1)
        rw = _rwkv(proj3, mu, w0[layer][None, :], _stack_rhs(_pad_rows(w2_decay[layer], 128)),
                   a0[layer][None, :], _stack_rhs(_pad_rows(a2_iclr[layer], 128)),
                   _pad_rows(g2_gate[layer], 256).astype(bf16), k_k[layer][None, :],
                   k_a[layer][None, :],
                   r_k[layer].reshape(1, -1), ln_x_w[layer][None, :], ln_x_b[layer][None, :])
        outs, lses = [], []
        for gi, (window, dil) in enumerate(ATTN_GROUPS):
            o_g, l_g = _attn_group(proj3, cosf, sinf, gi, dil)
            outs.append(o_g)
            lses.append(l_g)
        x1, h2, logits = _post(
            xt, rw.reshape(T, -1), outs, lses, proj,
            w_o_rwkv[layer].astype(bf16), w_o_attn[layer].astype(bf16), w_out[layer].astype(bf16),
            norm_ffn[layer][None, :], jnp.concatenate(_split2(_pad_cols(w_router[layer], 128)), axis=1),
            _pad_cols(b_router[layer][None, :], 128))
        tok, gates, pos, tile_e, n_valid, first_tile, n_tiles_e = _route(logits[:, :E], moe_tm, n_tiles)
        ys = _moe(tile_e, n_valid, first_tile, n_tiles_e, h2, tok, w1[layer], b1[layer][:, None, :],
                  w2[layer], b2[layer][:, None, :], moe_tm)
        xt = _final(x1, ys, pos, gates, norm_final[None, :], apply_norm=layer + 1 == depth)
    return xt.reshape(B, S, D)
```

```python
import functools

import jax
import jax.numpy as jnp
from jax import lax
from jax.experimental import pallas as pl
from jax.experimental.pallas import tpu as pltpu

f32 = jnp.float32
bf16 = jnp.bfloat16

NORM_EPS = 1e-5
RWKV_GN_EPS = 64e-5
ROPE_THETA = 500000.0
SWIGLU_LIMIT = 7.0
SWIGLU_ALPHA = 1.702

RWKV_HEADS = 16
HEAD_N = 64
RWKV_WIDTH = RWKV_HEADS * HEAD_N
DECAY_LORA = 64
ICLR_LORA = 64
GATE_LORA = 160
ATTN_GROUPS = ((128, 1), (512, 4), (2048, 16))
HEADS_PER_GROUP = 4
ATTN_E = 128
ATTN_GW = HEADS_PER_GROUP * ATTN_E
ATTN_WIDTH = len(ATTN_GROUPS) * ATTN_GW
ROPE_DIM = ATTN_E // 4
TOP_K = 4

LORA_PAD = 512
ZW_OFF, ZA_OFF, ZG_OFF = 3 * RWKV_WIDTH, 3 * RWKV_WIDTH + 128, 3 * RWKV_WIDTH + 256
SHIFT_PAD = 3 * RWKV_WIDTH + LORA_PAD
ATTN_OFF = SHIFT_PAD
GATE_OFF = ATTN_OFF + 3 * ATTN_WIDTH

RWKV_CHUNK = 64
_MOE_CHUNK_TENTHS = (0, 1, 4, 7, 10)
_FINAL_CHUNKS = 1
_RWKV_SEQS = 2
_RWKV_NWIDE = 12
ATTN_BLK = 128
_STRIDE_SPLIT = 4
_ATTN_UNITS = 16
VMEM_LIMIT = 56 * 1024 * 1024


def _dot(a, b, dims=(((1,), (0,)), ((), ())), precision=None):
    return lax.dot_general(a, b, dims, precision=precision, preferred_element_type=f32)


_NT = (((1,), (1,)), ((), ()))
_TN = (((0,), (0,)), ((), ()))


def _split2(a):
    hi = a.astype(bf16)
    return hi, (a - hi.astype(f32)).astype(bf16)


def _split3(a):
    hi = a.astype(bf16)
    rem = a - hi.astype(f32)
    mid = rem.astype(bf16)
    return hi, mid, (rem - mid.astype(f32)).astype(bf16)


def _stack_lhs(a):
    hi, lo = _split2(a)
    return jnp.concatenate([hi, lo, hi], axis=1)


def _stack_rhs(w):
    hi, lo = _split2(w)
    return jnp.concatenate([hi, hi, lo], axis=0)


def _proj_kernel(x_ref, g_ref, w_ref, o_ref, h_ref):
    @pl.when(pl.program_id(1) == 0)
    def _():
        x = x_ref[...]
        ms = jnp.mean(x * x, axis=-1, keepdims=True)
        h_ref[...] = (x * lax.rsqrt(ms + NORM_EPS) * g_ref[...]).astype(bf16)

    o_ref[...] = _dot(h_ref[...], w_ref[...])


def _proj(x2, g, wp, tm=1024, tn=2048):
    T, D = x2.shape
    NP = wp.shape[1]
    return pl.pallas_call(
        _proj_kernel,
        out_shape=jax.ShapeDtypeStruct((T, NP), f32),
        grid=(T // tm, NP // tn),
        in_specs=[pl.BlockSpec((tm, D), lambda i, j: (i, 0)),
                  pl.BlockSpec((1, D), lambda i, j: (0, 0)),
                  pl.BlockSpec((D, tn), lambda i, j: (0, j))],
        out_specs=pl.BlockSpec((tm, tn), lambda i, j: (i, j)),
        scratch_shapes=[pltpu.VMEM((tm, D), bf16)],
        compiler_params=pltpu.CompilerParams(
            dimension_semantics=("arbitrary", "arbitrary"), vmem_limit_bytes=VMEM_LIMIT),
        name="proj",
    )(x2, g, wp)


def _rwkv_kernel(p_ref, mu_ref, w0_ref, w2_ref, a0_ref, a2_ref, g2_ref, kk_ref, ka_ref,
                 rk_ref, lnw_ref, lnb_ref, o_ref, state_ref, prev_ref, dec_ref, wide_ref):
    (at0_ref, at_ref, rt0_ref, rt_ref, bt_ref, kt_ref, bh_ref, kh_ref, v_ref, rkb_ref, g_ref,
     kkr_ref) = [wide_ref.at[i] for i in range(_RWKV_NWIDE)]
    L = RWKV_CHUNK
    W = RWKV_WIDTH
    c_idx = pl.program_id(1)

    @pl.when(c_idx == 0)
    def _():
        state_ref[...] = jnp.zeros_like(state_ref)
        prev_ref[...] = jnp.zeros_like(prev_ref)

    RB = p_ref.shape[0]
    RL = RB * L
    p = p_ref[...].reshape(RL, p_ref.shape[2])
    row = lax.broadcasted_iota(jnp.int32, (RL, 1), 0)
    shifted = pltpu.roll(p, 1, axis=0)
    for bi in range(RB):
        shifted = jnp.where(row == bi * L, prev_ref[bi:bi + 1, :], shifted)
        prev_ref[bi:bi + 1, :] = p[bi * L + L - 1:bi * L + L, :]
    z = p + (shifted - p) * mu_ref[...]

    def per_seq(x, r):
        return jnp.concatenate(
            [jnp.broadcast_to(x[bi * L + r:bi * L + r + 1, :], (L, x.shape[1])) for bi in range(RB)],
            axis=0)

    r = z[:, 0:W]
    k = z[:, W:2 * W]
    v = z[:, 2 * W:3 * W]
    zw = z[:, ZW_OFF:ZW_OFF + 128]
    za = z[:, ZA_OFF:ZA_OFF + 128]
    zg = z[:, ZG_OFF:ZG_OFF + 256]

    w_raw = w0_ref[...] + _dot(_stack_lhs(jnp.tanh(zw)), w2_ref[...])
    sp = jnp.maximum(-w_raw, 0.0) + jnp.log(1.0 + jnp.exp(-jnp.abs(w_raw)))
    lw = -jnp.exp(-sp - 0.5)
    a = jax.nn.sigmoid(a0_ref[...] + _dot(_stack_lhs(za), a2_ref[...]))
    g_ref[...] = _dot(jax.nn.sigmoid(zg).astype(bf16), g2_ref[...])

    ti = lax.broadcasted_iota(jnp.int32, (RL, RL), 0)
    si = lax.broadcasted_iota(jnp.int32, (RL, RL), 1)
    tri = jnp.logical_and(ti >= si, ti // L == si // L).astype(bf16)
    lw_h, lw_m, lw_l = _split3(lw)
    c = _dot(jnp.concatenate([tri, tri, tri], axis=1), jnp.concatenate([lw_h, lw_m, lw_l], axis=0))
    cex = c - lw
    cm = per_seq(c, L // 2 - 1)
    cl = per_seq(c, L - 1)
    e_m = jnp.exp(-cm)
    e_c = jnp.exp(c)
    e_ex = jnp.exp(cex)
    e_inv = jnp.exp(cm - c)
    e_tail = jnp.exp(cl - c)

    kkv = k * kk_ref[...]
    kmod = k * (1.0 + (a - 1.0) * ka_ref[...])
    at0_ref[...] = e_ex
    rt0 = r * e_c
    rt0_ref[...] = rt0
    rt_ref[...] = rt0 * e_m
    kt_ref[...] = kmod * e_inv
    kh_ref[...] = kmod * e_tail
    bt_ref[...] = a * e_inv
    bh_ref[...] = a * e_tail
    at_ref[...] = e_ex * e_m
    v_ref[...] = v
    for bi in range(RB):
        dec_ref[bi:bi + 1, :] = jnp.exp(c[bi * L + L - 1:bi * L + L, :])
    rkb_ref[...] = r * kmod * rk_ref[...]
    kkr_ref[...] = kkv

    PW = 2 * HEAD_N
    pairs = range(RWKV_HEADS // 2)
    lane = lax.broadcasted_iota(jnp.int32, (L, PW), 1)
    trow = lax.broadcasted_iota(jnp.int32, (L, PW), 0)
    first = lane < HEAD_N
    scol = jnp.where(first, lane, lane - HEAD_N)
    strict_p = trow > scol
    incl_p = trow >= scol
    eye_p = (trow == scol).astype(f32)
    rb =lax.broadcasted_iota(jnp.int32, (PW, PW), 0)
    lb = lax.broadcasted_iota(jnp.int32, (PW, PW), 1)
    bd_mask = (rb < HEAD_N) == (lb < HEAD_N)

    def bd(x):
        zero = jnp.zeros_like(x)
        return jnp.concatenate([jnp.where(first, x, zero), jnp.where(first, zero, x)], axis=0)

    def segsum(x):
        s0 = jnp.sum(jnp.where(first, x, 0.0), axis=-1, keepdims=True)
        s1 = jnp.sum(jnp.where(first, 0.0, x), axis=-1, keepdims=True)
        return jnp.where(first, s0, s1)

    units = [(bi, p) for bi in range(RB) for p in pairs]
    ps = [(slice(bi * L, (bi + 1) * L), slice(p * PW, (p + 1) * PW)) for bi, p in units]
    kkr = [kkr_ref[s] for s in ps]
    ssq = [segsum(x * x) for x in kkr]
    kkh = [x / jnp.maximum(jnp.sqrt(q), 1e-12) for x, q in zip(kkr, ssq)]
    at0 = [(-kh_ * at0_ref[s]).astype(bf16) for kh_, s in zip(kkh, ps)]
    at = [(-kh_ * at_ref[s]).astype(bf16) for kh_, s in zip(kkh, ps)]
    bt = [(kh_ * bt_ref[s]).astype(bf16) for kh_, s in zip(kkh, ps)]
    bh = [(kh_ * bh_ref[s]).astype(bf16) for kh_, s in zip(kkh, ps)]
    vb = [v_ref[s].astype(bf16) for s in ps]

    lhs = [jnp.concatenate([a_, rt_ref[s].astype(bf16)], axis=0) for a_, s in zip(at, ps)]
    sb = [_dot(l_, bd(b_), _NT) for l_, b_ in zip(lhs, bt)]
    sk = [_dot(l_, bd(kt_ref[s].astype(bf16)), _NT) for l_, s in zip(lhs, ps)]
    a_ab = [jnp.where(strict_p, x[:L], 0.0) for x in sb]
    a_rb = [jnp.where(incl_p, x[L:], 0.0).astype(bf16) for x in sb]
    a_ak = [jnp.where(strict_p, x[:L], 0.0).astype(bf16) for x in sk]
    a_rk = [jnp.where(incl_p, x[L:], 0.0).astype(bf16) for x in sk]

    tinv = [eye_p + x for x in a_ab]
    xp = a_ab
    n = 2
    while n < L:
        xb = [x.astype(bf16) for x in xp]
        xp = [_dot(x, bd(x)) for x in xb]
        tinv = [t + _dot(t.astype(bf16), bd(x.astype(bf16))) for t, x in zip(tinv, xp)]
        n *= 2
    tb = [t.astype(bf16) for t in tinv]

    akv = [_dot(a_, bd(v_)).astype(bf16) for a_, v_ in zip(a_ak, vb)]
    w12 = [_dot(t, jnp.concatenate([bd(a_), bd(k_)], axis=1)) for t, a_, k_ in zip(tb, at0, akv)]

    n_pairs = len(pairs)
    s0 = [state_ref[bi * n_pairs + p] for bi, p in units]
    s0b = [s.astype(bf16) for s in s0]
    uy = [_dot(jnp.concatenate([w[:, :PW].astype(bf16), rt0_ref[s].astype(bf16)], axis=0), sb_, _NT)
          for w, s, sb_ in zip(w12, ps, s0b)]
    u = [x[:L] + w[:, PW:] for x, w in zip(uy, w12)]
    ub = [x.astype(bf16) for x in u]
    y = [x[L:] + _dot(jnp.concatenate([rb_, rk_], axis=1), jnp.concatenate([bd(u_), bd(v_)], axis=0))
         for x, rb_, rk_, u_, v_ in zip(uy, a_rb, a_rk, ub, vb)]
    for i, (bi, p) in enumerate(units):
        upd = _dot(jnp.concatenate([ub[i], vb[i]], axis=0),
                   jnp.concatenate([bh[i], kh_ref[ps[i]].astype(bf16)], axis=0), _TN)
        state_ref[bi * n_pairs + p] = (s0[i] * dec_ref[bi:bi + 1, ps[i][1]]
                                       + jnp.where(bd_mask, upd, 0.0))

    inv_n = 1.0 / HEAD_N
    mean = [segsum(x) * inv_n for x in y]
    yc = [x - m for x, m in zip(y, mean)]
    var = [segsum(x * x) * inv_n for x in yc]
    bonus = [segsum(rkb_ref[s]) for s in ps]
    for i, (bi, p) in enumerate(units):
        s = ps[i]
        yn = yc[i] * lax.rsqrt(var[i] + RWKV_GN_EPS) * lnw_ref[:, s[1]] + lnb_ref[:, s[1]]
        o_ref[bi, :, s[1]] = ((yn + bonus[i] * v_ref[s]) * g_ref[s]).astype(o_ref.dtype)


def _rwkv(proj3, mu, w0, w2d, a0, a2, g2, k_k, k_a, r_k, ln_w, ln_b):
    B, S, NP = proj3.shape
    L, W = RWKV_CHUNK, RWKV_WIDTH
    row = lambda n: pl.BlockSpec((1, n), lambda b, c: (0, 0))
    full = lambda a: pl.BlockSpec(a.shape, lambda b, c: (0, 0))
    RB = _RWKV_SEQS if B % _RWKV_SEQS == 0 else 1
    return pl.pallas_call(
        _rwkv_kernel,
        out_shape=jax.ShapeDtypeStruct((B, S, W), bf16),
        grid=(B // RB, S // L),
        in_specs=[pl.BlockSpec((RB, L, SHIFT_PAD), lambda b, c: (b, c, 0)),
                  row(SHIFT_PAD), row(W), full(w2d), row(W), full(a2), full(g2),
                  row(W), row(W), row(W), row(W), row(W)],
        out_specs=pl.BlockSpec((RB, L, W), lambda b, c: (b, c, 0)),
        scratch_shapes=[pltpu.VMEM((RB * RWKV_HEADS // 2, 2 * HEAD_N, 2 * HEAD_N), f32),
                        pltpu.VMEM((RB, SHIFT_PAD), f32), pltpu.VMEM((RB, W), f32),
                        pltpu.VMEM((_RWKV_NWIDE, RB * L, W), f32)],
        compiler_params=pltpu.CompilerParams(
            dimension_semantics=("arbitrary", "arbitrary"), vmem_limit_bytes=VMEM_LIMIT),
        name="rwkv",
    )(proj3, mu, w0, w2d, a0, a2, g2, k_k, k_a, r_k, ln_w, ln_b)


def _rope(x, cosf, sinf, lane):
    rot = jnp.where(lane < ROPE_DIM // 2, pltpu.roll(x, ATTN_E - ROPE_DIM // 2, axis=1),
                    pltpu.roll(x, ROPE_DIM // 2, axis=1))
    return x * cosf + rot * sinf


def _attn_kernel(q_ref, kc_ref, kp_ref, vc_ref, vp_ref, cc_ref, sc_ref, cp_ref, sp_ref,
                 o_ref, l_ref, *stage, dil, hps, nblk, wave):
    n = pl.program_id(1)
    blk = ATTN_BLK
    span = blk * dil
    qi = lax.broadcasted_iota(jnp.int32, (blk, blk), 0)
    kj = lax.broadcasted_iota(jnp.int32, (blk, blk), 1)
    lane = lax.broadcasted_iota(jnp.int32, (blk, ATTN_E), 1)
    cur_ok = kj <= qi
    prev_ok = kj >= qi
    first_ok = jnp.logical_and(prev_ok, n > 0)
    scale = ATTN_E ** -0.5

    ins = (q_ref, kc_ref, kp_ref, vc_ref, vp_ref, cc_ref, sc_ref, cp_ref, sp_ref)
    outs = (o_ref, l_ref)
    Q, KC, KP, VC, VP, CC, SC, CP, SP = range(len(ins))
    sub = dil // _STRIDE_SPLIT if stage else dil

    def rows(b, r, d):
        return pl.ds(b * span + r, blk, stride=d) if d > 1 else pl.ds(b * span, blk)

    def cols(hh):
        return slice(hh * ATTN_E, (hh + 1) * ATTN_E)

    if stage:
        in_stage, out_stage = stage
        quarter = span // _STRIDE_SPLIT
        for xi, ref in enumerate(ins):
            for r4 in range(_STRIDE_SPLIT):
                in_stage[xi, r4] = ref[0, pl.ds(r4, quarter, stride=_STRIDE_SPLIT), :]

        def rd(xi, b, r, hh):
            return in_stage[xi, r % _STRIDE_SPLIT, rows(0, r // _STRIDE_SPLIT, sub), :]

        def wr(oi, b, r, hh, val):
            out_stage[oi, r % _STRIDE_SPLIT, rows(0, r // _STRIDE_SPLIT, sub), :] = val
    else:
        def rd(xi, b, r, hh):
            return ins[xi][0, rows(b, r, dil), cols(hh) if xi < CC else slice(None)]

        def wr(oi, b, r, hh, val):
            outs[oi][0, rows(b, r, dil), cols(hh)] = val

    all_units = [(b, r, hh) for b in range(nblk) for r in range(dil) for hh in range(hps)]
    kc, vc = {}, {}
    for w0 in range(0, len(all_units), wave):
        units = all_units[w0:w0 + wave]
        tab = {u[:2]: (rd(CC, *u), rd(SC, *u)) for u in units}
        q = {u: _rope(rd(Q, *u), *tab[u[:2]], lane).astype(bf16) for u in units}
        for u in units:
            kc[u] = _rope(rd(KC, *u), *tab[u[:2]], lane).astype(bf16)
            vc[u] = rd(VC, *u).astype(bf16)
        kp, vp = {}, {}
        for (b, r, hh) in units:
            if b == 0:
                ptab = (rd(CP, 0, r, hh), rd(SP, 0, r, hh))
                kp[(b, r, hh)] = _rope(rd(KP, 0, r, hh), *ptab, lane).astype(bf16)
                vp[(b, r, hh)] = rd(VP, 0, r, hh).astype(bf16)
            else:
                kp[(b, r, hh)] = kc[(b - 1, r, hh)]
                vp[(b, r, hh)] = vc[(b - 1, r, hh)]
        s_c = {u: jnp.where(cur_ok, _dot(q[u], kc[u], _NT) * scale, -jnp.inf) for u in units}
        s_p = {u: jnp.where(first_ok if u[0] == 0 else prev_ok, _dot(q[u], kp[u], _NT) * scale, -jnp.inf)
               for u in units}
        m = {u: jnp.maximum(jnp.max(s_c[u], axis=-1, keepdims=True),
                            jnp.max(s_p[u], axis=-1, keepdims=True)) for u in units}
        p_c = {u: jnp.exp(s_c[u] - m[u]) for u in units}
        p_p = {u: jnp.exp(s_p[u] - m[u]) for u in units}
        l = {u: jnp.sum(p_c[u], axis=-1, keepdims=True) + jnp.sum(p_p[u], axis=-1, keepdims=True)
             for u in units}
        acc = {u: _dot(p_c[u].astype(bf16), vc[u]) + _dot(p_p[u].astype(bf16), vp[u]) for u in units}
        for u in units:
            wr(0, *u, acc[u] / l[u])
            wr(1, *u, jnp.broadcast_to(m[u] + jnp.log(l[u]), (blk, ATTN_E)))
    if stage:
        for oi, ref in enumerate(outs):
            for r4 in range(_STRIDE_SPLIT):
                ref[0, pl.ds(r4, quarter, stride=_STRIDE_SPLIT), :] = out_stage[oi, r4]


def _attn_group(proj3, cosf, sinf, gi, dil):
    B, S, NP = proj3.shape
    span = ATTN_BLK * dil
    hps = HEADS_PER_GROUP if dil == 1 else 1
    nblk = _ATTN_UNITS // (dil * hps)
    R = span * nblk
    cw = hps * ATTN_E
    per = ATTN_GW // cw
    qo = (ATTN_OFF + gi * ATTN_GW) // cw
    ko = qo + ATTN_WIDTH // cw
    vo = ko + ATTN_WIDTH // cw

    def cur(off):
        return pl.BlockSpec((1, R, cw), lambda b, n, h: (b, n, off + h))

    def prev(off):
        return pl.BlockSpec((1, span, cw), lambda b, n, h: (b, jnp.maximum(n * nblk - 1, 0), off + h))

    tc = pl.BlockSpec((1, R, ATTN_E), lambda b, n, h: (b, n, 0))
    tp = pl.BlockSpec((1, span, ATTN_E), lambda b, n, h: (b, jnp.maximum(n * nblk - 1, 0), 0))
    out = pl.BlockSpec((1, R, cw), lambda b, n, h: (b, n, h))
    staged = dil > _STRIDE_SPLIT and nblk == 1 and hps == 1
    scratch = ([pltpu.VMEM((9, _STRIDE_SPLIT, R // _STRIDE_SPLIT, ATTN_E), f32),
                pltpu.VMEM((2, _STRIDE_SPLIT, R // _STRIDE_SPLIT, ATTN_E), f32)] if staged else [])
    o, l = pl.pallas_call(
        functools.partial(_attn_kernel, dil=dil, hps=hps, nblk=nblk,
                          wave=_ATTN_UNITS // 2 if dil > HEADS_PER_GROUP else _ATTN_UNITS),
        out_shape=[jax.ShapeDtypeStruct((B, S, ATTN_GW), f32)] * 2,
        grid=(B, S // R, per),
        in_specs=[cur(qo), cur(ko), prev(ko), cur(vo), prev(vo), tc, tc, tp, tp],
        out_specs=[out, out],
        scratch_shapes=scratch,
        compiler_params=pltpu.CompilerParams(
            dimension_semantics=("arbitrary", "arbitrary", "arbitrary"),
            vmem_limit_bytes=VMEM_LIMIT),
        name=f"attn_g{gi}",
    )(proj3, proj3, proj3, proj3, proj3, cosf, sinf, cosf, sinf)
    return o.reshape(B * S, ATTN_GW), l.reshape(B * S, ATTN_GW)


def _merge_kernel(rw_ref, o0_ref, o1_ref, o2_ref, l0_ref, l1_ref, l2_ref, ga_ref, gb_ref,
                  wor_ref, woa_ref, mg_ref):
    l0, l1, l2 = l0_ref[...], l1_ref[...], l2_ref[...]
    m = jnp.maximum(jnp.maximum(l0, l1), l2)
    e0, e1, e2 = jnp.exp(l0 - m), jnp.exp(l1 - m), jnp.exp(l2 - m)
    o = (e0 * o0_ref[...] + e1 * o1_ref[...] + e2 * o2_ref[...]) / (e0 + e1 + e2)
    y_b = _dot(o.astype(bf16), woa_ref[...])
    y_a = _dot(rw_ref[...], wor_ref[...])
    merged = jax.nn.sigmoid(ga_ref[...]) * y_a + jax.nn.sigmoid(gb_ref[...]) * y_b
    mg_ref[...] = merged.astype(bf16)


def _out_kernel(x_ref, mg_ref, wout_ref, gn_ref, wr_ref, br_ref, x1_ref, h2_ref, lg_ref):
    x1 = x_ref[...] + _dot(mg_ref[...], wout_ref[...])
    x1_ref[...] = x1
    ms = jnp.mean(x1 * x1, axis=-1, keepdims=True)
    h2 = x1 * lax.rsqrt(ms + NORM_EPS) * gn_ref[...]
    h2_ref[...] = h2.astype(bf16)
    hi, lo = _split2(h2)
    n_e = lg_ref.shape[1]
    both = _dot(hi, wr_ref[...])
    lg_ref[...] = both[:, :n_e] + both[:, n_e:] + _dot(lo, wr_ref[:, :n_e]) + br_ref[...]


def _post(x2, rw, outs, lses, proj, wor, woa, wout, gn, wr, br, tm=512):
    T, D = x2.shape
    row = lambda w: pl.BlockSpec((tm, w), lambda i: (i, 0))
    const = lambda a: pl.BlockSpec(a.shape, lambda i: (0, 0), pipeline_mode=pl.Buffered(1))
    ga_blk = GATE_OFF // D
    params = pltpu.CompilerParams(dimension_semantics=("arbitrary",), vmem_limit_bytes=VMEM_LIMIT)
    merged = pl.pallas_call(
        _merge_kernel,
        out_shape=jax.ShapeDtypeStruct((T, D), bf16),
        grid=(T // tm,),
        in_specs=[row(RWKV_WIDTH)] + [row(ATTN_GW)] * 6
                 + [pl.BlockSpec((tm, D), lambda i: (i, ga_blk)),
                    pl.BlockSpec((tm, D), lambda i: (i, ga_blk + 1)),
                    const(wor), const(woa)],
        out_specs=row(D),
        compiler_params=params,
        name="merge",
    )(rw, *outs, *lses, proj, proj, wor, woa)
    return pl.pallas_call(
        _out_kernel,
        out_shape=[jax.ShapeDtypeStruct((T, D), f32), jax.ShapeDtypeStruct((T, D), bf16),
                   jax.ShapeDtypeStruct((T, 128), f32)],
        grid=(T // tm,),
        in_specs=[row(D), row(D), const(wout), const(gn), const(wr), const(br)],
        out_specs=[row(D), row(D), row(128)],
        compiler_params=params,
        name="out",
    )(x2, merged, wout, gn, wr, br)


def _moe_up_kernel(st, sc, se, sf, nv, xs_ref, w1g_ref, w1l_ref, b1g_ref, b1l_ref, *rest):
    o_ref, wg_ref, wl_ref = rest[-3:]
    s = pl.program_id(0)

    @pl.when(s < nv[0])
    def _():
        @pl.when(sf[s] == 1)
        def _():
            wg_ref[...] = w1g_ref[...].astype(bf16)
            wl_ref[...] = w1l_ref[...].astype(bf16)

        xs = xs_ref[...]
        hg = _dot(xs, wg_ref[...]) + b1g_ref[...]
        hl = _dot(xs, wl_ref[...]) + b1l_ref[...]
        x_glu = jnp.minimum(hg, SWIGLU_LIMIT)
        x_lin = jnp.clip(hl, -SWIGLU_LIMIT, SWIGLU_LIMIT)
        act = x_glu * jax.nn.sigmoid(SWIGLU_ALPHA * x_glu) * (x_lin + 1.0)
        o_ref[...] = act.astype(o_ref.dtype)


def _moe_down_kernel(st, sc, se, sf, nv, a_ref, w2_ref, b2_ref, o_ref, w_ref):
    s = pl.program_id(0)

    @pl.when(s < nv[0])
    def _():
        @pl.when(sf[s] == 1)
        def _():
            w_ref[...] = w2_ref[...].astype(bf16)

        o_ref[...] = (_dot(a_ref[...], w_ref[...]) + b2_ref[...]).astype(o_ref.dtype)


def _moe_schedule(tile_e, n_valid, first_tile, n_tiles_e, nc, lo, hi):
    i32 = jnp.int32
    E = first_tile.shape[0]
    experts = jnp.arange(E, dtype=i32)
    nv = jnp.clip(n_valid[0], lo, hi) - lo
    s = jnp.arange((hi - lo) * nc, dtype=i32)
    s = jnp.clip(s, 0, jnp.maximum(nc * nv - 1, 0))
    e = tile_e[lo:hi][s // nc]
    pick = lambda table: jnp.sum(jnp.where(e[:, None] == experts[None, :], table[None, :], 0), axis=1)
    ft = jnp.clip(first_tile, lo, hi)
    ne = jnp.clip(first_tile + n_tiles_e, lo, hi) - ft
    ft, ne = pick(ft), jnp.maximum(pick(ne), 1)
    local = s - nc * (ft - lo)
    col = local // ne
    row = ft + local % ne
    changed = jnp.logical_or(e != jnp.roll(e, 1), col != jnp.roll(col, 1)).at[0].set(True)
    return (row.astype(i32), col.astype(i32), e.astype(i32), changed.astype(i32),
            (nc * nv).astype(i32).reshape(1))


def _moe(tile_e, n_valid, first_tile, n_tiles_e, h2, tok, w1, b1, w2, b2, tm, tf=1024, tn=2048):
    P = tok.shape[0]
    D = h2.shape[1]
    E, _, F2 = w1.shape
    F = F2 // 2
    nj = F // tf
    nn = D // tn
    nt = P // tm
    params = pltpu.CompilerParams(dimension_semantics=("arbitrary",), vmem_limit_bytes=VMEM_LIMIT)

    bounds = sorted({(f * nt) // _MOE_CHUNK_TENTHS[-1] for f in _MOE_CHUNK_TENTHS})
    act = None
    for ck, (lo, hi) in enumerate(zip(bounds[:-1], bounds[1:])):
        ct = hi - lo
        sched = _moe_schedule(tile_e, n_valid, first_tile, n_tiles_e, nj, lo, hi)
        xs_k = h2[tok[lo * tm:hi * tm]]
        carry = [] if act is None else [act]
        act = pl.pallas_call(
            _moe_up_kernel,
            out_shape=jax.ShapeDtypeStruct((P, F), bf16),
            grid_spec=pltpu.PrefetchScalarGridSpec(
                num_scalar_prefetch=5,
                grid=(ct * nj,),
                in_specs=[
                    pl.BlockSpec((tm, D), lambda s, st, sc, se, sf, nv, lo=lo: (st[s] - lo, 0)),
                    pl.BlockSpec((None, D, tf), lambda s, st, sc, se, sf, nv: (se[s], 0, sc[s])),
                    pl.BlockSpec((None, D, tf), lambda s, st, sc, se, sf, nv: (se[s], 0, nj + sc[s])),
                    pl.BlockSpec((None, 1, tf), lambda s, st, sc, se, sf, nv: (se[s], 0, sc[s])),
                    pl.BlockSpec((None, 1, tf), lambda s, st, sc, se, sf, nv: (se[s], 0, nj + sc[s])),
                ] + [pl.BlockSpec(memory_space=pl.ANY)] * len(carry),
                out_specs=pl.BlockSpec((tm, tf), lambda s, st, sc, se, sf, nv: (st[s], sc[s])),
                scratch_shapes=[pltpu.VMEM((D, tf), bf16), pltpu.VMEM((D, tf), bf16)],
            ),
            input_output_aliases={10: 0} if carry else {},
            compiler_params=params,
            name=f"moe_up{ck}",
        )(*sched, xs_k, w1, w1, b1, b1, *carry)

    sched = _moe_schedule(tile_e, n_valid, first_tile, n_tiles_e, nn, 0, nt)
    return pl.pallas_call(
        _moe_down_kernel,
        out_shape=jax.ShapeDtypeStruct((P, D), bf16),
        grid_spec=pltpu.PrefetchScalarGridSpec(
            num_scalar_prefetch=5,
            grid=(nt * nn,),
            in_specs=[
                pl.BlockSpec((tm, F), lambda s, st, sc, se, sf, nv: (st[s], 0)),
                pl.BlockSpec((None, F, tn), lambda s, st, sc, se, sf, nv: (se[s], 0, sc[s])),
                pl.BlockSpec((None, 1, tn), lambda s, st, sc, se, sf, nv: (se[s], 0, sc[s])),
            ],
            out_specs=pl.BlockSpec((tm, tn), lambda s, st, sc, se, sf, nv: (st[s], sc[s])),
            scratch_shapes=[pltpu.VMEM((F, tn), bf16)],
        ),
        compiler_params=params,
        name="moe_down",
    )(*sched, act, w2, b2)


def _final_kernel(x1_ref, y_ref, gate_ref, g_ref, *rest, apply_norm):
    o_ref = rest[-1]
    gates = gate_ref[...]
    y = gates[:, 0:1] * y_ref[0].astype(f32)
    for kk in range(1, TOP_K):
        y = y + gates[:, kk:kk + 1] * y_ref[kk].astype(f32)
    x2 = x1_ref[...] + y
    if apply_norm:
        ms = jnp.mean(x2 * x2, axis=-1, keepdims=True)
        x2 = x2 * lax.rsqrt(ms + NORM_EPS) * g_ref[...]
    o_ref[...] = x2


def _final(x1, ys, pos, gates, g, apply_norm, tm=256):
    T, D = x1.shape
    n_chunks = _FINAL_CHUNKS if T % (_FINAL_CHUNKS * tm) == 0 else 1
    tc = T // n_chunks
    nb = tc // tm
    pos = pos.reshape(TOP_K, T)
    out = None
    for ck in range(n_chunks):
        y4 = ys[pos[:, ck * tc:(ck + 1) * tc].reshape(-1)].reshape(TOP_K, tc, D)
        carry = [] if out is None else [out]
        out = pl.pallas_call(
            functools.partial(_final_kernel, apply_norm=apply_norm),
            out_shape=jax.ShapeDtypeStruct((T, D), f32),
            grid=(nb,),
            in_specs=[pl.BlockSpec((tm, D), lambda i, o=ck * nb: (o + i, 0)),
                      pl.BlockSpec((TOP_K, tm, D), lambda i: (0, i, 0)),
                      pl.BlockSpec((tm, TOP_K), lambda i, o=ck * nb: (o + i, 0)),
                      pl.BlockSpec((1, D), lambda i: (0, 0))]
                     + [pl.BlockSpec(memory_space=pl.ANY)] * len(carry),
            out_specs=pl.BlockSpec((tm, D), lambda i, o=ck * nb: (o + i, 0)),
            input_output_aliases={4: 0} if carry else {},
            compiler_params=pltpu.CompilerParams(
                dimension_semantics=("arbitrary",), vmem_limit_bytes=VMEM_LIMIT),
            name=f"final{ck}",
        )(x1, y4, gates, g, *carry)
    return out


def _pad_cols(a, n):
    return jnp.pad(a, ((0, 0), (0, n - a.shape[1])))


def _pad_rows(a, n):
    return jnp.pad(a, ((0, n - a.shape[0]), (0, 0)))


def _regroup_cols(a):
    W = RWKV_WIDTH
    o = 3 * W
    return jnp.concatenate([
        a[:, :o],
        _pad_cols(a[:, o:o + DECAY_LORA], 128),
        _pad_cols(a[:, o + DECAY_LORA:o + DECAY_LORA + ICLR_LORA], 128),
        _pad_cols(a[:, o + DECAY_LORA + ICLR_LORA:o + DECAY_LORA + ICLR_LORA + GATE_LORA], 256),
        a[:, o + DECAY_LORA + ICLR_LORA + GATE_LORA:],
    ], axis=1)


def _rope_tables(positions):
    half = ROPE_DIM // 2
    inv_freq = ROPE_THETA ** (-jnp.arange(half, dtype=f32) / half)
    ang = positions.astype(f32)[..., None] * inv_freq
    cos, sin = jnp.cos(ang), jnp.sin(ang)
    B, S = positions.shape
    cosf = jnp.concatenate([cos, cos, jnp.ones((B, S, ATTN_E - ROPE_DIM), f32)], axis=-1)
    sinf = jnp.concatenate([-sin, sin, jnp.zeros((B, S, ATTN_E - ROPE_DIM), f32)], axis=-1)
    return cosf, sinf


def _route(logits, tm, n_tiles):
    T, E = logits.shape
    n = T * TOP_K
    i32 = jnp.int32
    top_val, top_idx = lax.top_k(logits, TOP_K)
    gates = jax.nn.softmax(top_val, axis=-1)
    flat_e = top_idx.reshape(-1).astype(i32)
    iota = jnp.arange(n, dtype=i32)
    experts = jnp.arange(E, dtype=i32)

    def lookup(table, idx):
        return jnp.sum(jnp.where(idx[:, None] == experts[None, :], table[None, :], 0), axis=1)

    _, order = lax.sort((flat_e, iota), num_keys=1, is_stable=True)
    _, rank = lax.sort((order, iota), num_keys=1)
    sizes = jnp.sum((flat_e[:, None] == experts[None, :]).astype(i32), axis=0)
    padded = ((sizes + tm - 1) // tm) * tm
    pad_end = jnp.cumsum(padded)
    pad_start = pad_end - padded
    start = jnp.cumsum(sizes) - sizes
    pos = lookup(pad_start - start, flat_e) + rank
    tile_start = jnp.arange(n_tiles, dtype=i32) * tm
    tile_e = jnp.minimum(jnp.sum((pad_end[None, :] <= tile_start[:, None]).astype(i32), axis=1), E - 1)
    n_valid = (pad_end[-1] // tm).astype(i32).reshape(1)
    row_e = jnp.repeat(tile_e, tm)
    rows = jnp.arange(n_tiles * tm, dtype=i32)
    off = rows - lookup(pad_start, row_e)
    src = jnp.clip(lookup(start, row_e) + off, 0, n - 1)
    tok = jnp.where(off < lookup(sizes, row_e), order[src] // TOP_K, rows % T)
    pos_slot_major = pos.reshape(T, TOP_K).T.reshape(-1)
    return tok, gates, pos_slot_major, tile_e, n_valid, pad_start // tm, padded // tm


def kernel(x, positions, norm_mix, w_in, shift_mu, w0, w2_decay, a0, a2_iclr, g2_gate, k_k, k_a,
           r_k, ln_x_w, ln_x_b, w_o_rwkv, w_o_attn, w_out, norm_ffn, w_router, b_router,
           w1, b1, w2, b2, norm_final):
    B, S, D = x.shape
    T = B * S
    E = w_router.shape[-1]
    depth = norm_mix.shape[0]
    xt = x.reshape(T, D)
    cosf, sinf = _rope_tables(positions)
    moe_tm = 512
    n_tiles = (T * TOP_K) // moe_tm + E
    for layer in range(depth):
        wp = _regroup_cols(w_in[layer]).astype(bf16)
        mu = _regroup_cols(shift_mu[layer][None, :])
        proj = _proj(xt, norm_mix[layer][None, :], wp)
        proj3 = proj.reshape(B, S, -1)
        rw = _rwkv(proj3, mu, w0[layer][None, :], _stack_rhs(_pad_rows(w2_decay[layer], 128)),
                   a0[layer][None, :], _stack_rhs(_pad_rows(a2_iclr[layer], 128)),
                   _pad_rows(g2_gate[layer], 256).astype(bf16), k_k[layer][None, :],
                   k_a[layer][None, :],
                   r_k[layer].reshape(1, -1), ln_x_w[layer][None, :], ln_x_b[layer][None, :])
        outs, lses = [], []
        for gi, (window, dil) in enumerate(ATTN_GROUPS):
            o_g, l_g = _attn_group(proj3, cosf, sinf, gi, dil)
            outs.append(o_g)
            lses.append(l_g)
        x1, h2, logits = _post(
            xt, rw.reshape(T, -1), outs, lses, proj,
            w_o_rwkv[layer].astype(bf16), w_o_attn[layer].astype(bf16), w_out[layer].astype(bf16),
            norm_ffn[layer][None, :], jnp.concatenate(_split2(_pad_cols(w_router[layer], 128)), axis=1),
            _pad_cols(b_router[layer][None, :], 128))
        tok, gates, pos, tile_e, n_valid, first_tile, n_tiles_e = _route(logits[:, :E], moe_tm, n_tiles)
        ys = _moe(tile_e, n_valid, first_tile, n_tiles_e, h2, tok, w1[layer], b1[layer][:, None, :],
                  w2[layer], b2[layer][:, None, :], moe_tm)
        xt = _final(x1, ys, pos, gates, norm_final[None, :], apply_norm=layer + 1 == depth)
    return xt.reshape(B, S, D)
```

```python
import functools

import jax
import jax.numpy as jnp
from jax import lax
from jax.experimental import pallas as pl
from jax.experimental.pallas import tpu as pltpu

f32 = jnp.float32
bf16 = jnp.bfloat16

NORM_EPS = 1e-5
RWKV_GN_EPS = 64e-5
ROPE_THETA = 500000.0
SWIGLU_LIMIT = 7.0
SWIGLU_ALPHA = 1.702

RWKV_HEADS = 16
HEAD_N = 64
RWKV_WIDTH = RWKV_HEADS * HEAD_N
DECAY_LORA = 64
ICLR_LORA = 64
GATE_LORA = 160
ATTN_GROUPS = ((128, 1), (512, 4), (2048, 16))
HEADS_PER_GROUP = 4
ATTN_E = 128
ATTN_GW = HEADS_PER_GROUP * ATTN_E
ATTN_WIDTH = len(ATTN_GROUPS) * ATTN_GW
ROPE_DIM = ATTN_E // 4
TOP_K = 4

LORA_PAD = 512
ZW_OFF, ZA_OFF, ZG_OFF = 3 * RWKV_WIDTH, 3 * RWKV_WIDTH + 128, 3 * RWKV_WIDTH + 256
SHIFT_PAD = 3 * RWKV_WIDTH + LORA_PAD
ATTN_OFF = SHIFT_PAD
GATE_OFF = ATTN_OFF + 3 * ATTN_WIDTH

RWKV_CHUNK = 64
_MOE_CHUNK_TENTHS = (0, 1, 5, 10)
_FINAL_CHUNKS = 1
_RWKV_SEQS = 2
_RWKV_NWIDE = 12
ATTN_BLK = 128
_STRIDE_SPLIT = 4
_ATTN_UNITS = 16
VMEM_LIMIT = 56 * 1024 * 1024


def _dot(a, b, dims=(((1,), (0,)), ((), ())), precision=None):
    return lax.dot_general(a, b, dims, precision=precision, preferred_element_type=f32)


_NT = (((1,), (1,)), ((), ()))
_TN = (((0,), (0,)), ((), ()))


def _split2(a):
    hi = a.astype(bf16)
    return hi, (a - hi.astype(f32)).astype(bf16)


def _split3(a):
    hi = a.astype(bf16)
    rem = a - hi.astype(f32)
    mid = rem.astype(bf16)
    return hi, mid, (rem - mid.astype(f32)).astype(bf16)


def _stack_lhs(a):
    hi, lo = _split2(a)
    return jnp.concatenate([hi, lo, hi], axis=1)


def _stack_rhs(w):
    hi, lo = _split2(w)
    return jnp.concatenate([hi, hi, lo], axis=0)


def _proj_kernel(x_ref, g_ref, w_ref, o_ref, h_ref):
    @pl.when(pl.program_id(1) == 0)
    def _():
        x = x_ref[...]
        ms = jnp.mean(x * x, axis=-1, keepdims=True)
        h_ref[...] = (x * lax.rsqrt(ms + NORM_EPS) * g_ref[...]).astype(bf16)

    o_ref[...] = _dot(h_ref[...], w_ref[...])


def _proj(x2, g, wp, tm=1024, tn=2048):
    T, D = x2.shape
    NP = wp.shape[1]
    return pl.pallas_call(
        _proj_kernel,
        out_shape=jax.ShapeDtypeStruct((T, NP), f32),
        grid=(T // tm, NP // tn),
        in_specs=[pl.BlockSpec((tm, D), lambda i, j: (i, 0)),
                  pl.BlockSpec((1, D), lambda i, j: (0, 0)),
                  pl.BlockSpec((D, tn), lambda i, j: (0, j))],
        out_specs=pl.BlockSpec((tm, tn), lambda i, j: (i, j)),
        scratch_shapes=[pltpu.VMEM((tm, D), bf16)],
        compiler_params=pltpu.CompilerParams(
            dimension_semantics=("arbitrary", "arbitrary"), vmem_limit_bytes=VMEM_LIMIT),
        name="proj",
    )(x2, g, wp)


def _rwkv_kernel(p_ref, mu_ref, w0_ref, w2_ref, a0_ref, a2_ref, g2_ref, kk_ref, ka_ref,
                 rk_ref, lnw_ref, lnb_ref, o_ref, state_ref, prev_ref, dec_ref, wide_ref):
    (at0_ref, at_ref, rt0_ref, rt_ref, bt_ref, kt_ref, bh_ref, kh_ref, v_ref, rkb_ref, g_ref,
     kkr_ref) = [wide_ref.at[i] for i in range(_RWKV_NWIDE)]
    L = RWKV_CHUNK
    W = RWKV_WIDTH
    c_idx = pl.program_id(1)

    @pl.when(c_idx == 0)
    def _():
        state_ref[...] = jnp.zeros_like(state_ref)
        prev_ref[...] = jnp.zeros_like(prev_ref)

    RB = p_ref.shape[0]
    RL = RB * L
    p = p_ref[...].reshape(RL, p_ref.shape[2])
    row = lax.broadcasted_iota(jnp.int32, (RL, 1), 0)
    shifted = pltpu.roll(p, 1, axis=0)
    for bi in range(RB):
        shifted = jnp.where(row == bi * L, prev_ref[bi:bi + 1, :], shifted)
        prev_ref[bi:bi + 1, :] = p[bi * L + L - 1:bi * L + L, :]
    z = p + (shifted - p) * mu_ref[...]

    def per_seq(x, r):
        return jnp.concatenate(
            [jnp.broadcast_to(x[bi * L + r:bi * L + r + 1, :], (L, x.shape[1])) for bi in range(RB)],
            axis=0)

    r = z[:, 0:W]
    k = z[:, W:2 * W]
    v = z[:, 2 * W:3 * W]
    zw = z[:, ZW_OFF:ZW_OFF + 128]
    za = z[:, ZA_OFF:ZA_OFF + 128]
    zg = z[:, ZG_OFF:ZG_OFF + 256]

    w_raw = w0_ref[...] + _dot(_stack_lhs(jnp.tanh(zw)), w2_ref[...])
    sp = jnp.maximum(-w_raw, 0.0) + jnp.log(1.0 + jnp.exp(-jnp.abs(w_raw)))
    lw = -jnp.exp(-sp - 0.5)
    a = jax.nn.sigmoid(a0_ref[...] + _dot(_stack_lhs(za), a2_ref[...]))
    g_ref[...] = _dot(jax.nn.sigmoid(zg).astype(bf16), g2_ref[...])

    ti = lax.broadcasted_iota(jnp.int32, (RL, RL), 0)
    si = lax.broadcasted_iota(jnp.int32, (RL, RL), 1)
    tri = jnp.logical_and(ti >= si, ti // L == si // L).astype(bf16)
    lw_h, lw_m, lw_l = _split3(lw)
    c = _dot(jnp.concatenate([tri, tri, tri], axis=1), jnp.concatenate([lw_h, lw_m, lw_l], axis=0))
    cex = c - lw
    cm = per_seq(c, L // 2 - 1)
    cl = per_seq(c, L - 1)
    e_m = jnp.exp(-cm)
    e_c = jnp.exp(c)
    e_ex = jnp.exp(cex)
    e_inv = jnp.exp(cm - c)
    e_tail = jnp.exp(cl - c)

    kkv = k * kk_ref[...]
    kmod = k * (1.0 + (a - 1.0) * ka_ref[...])
    at0_ref[...] = e_ex
    rt0 = r * e_c
    rt0_ref[...] = rt0
    rt_ref[...] = rt0 * e_m
    kt_ref[...] = kmod * e_inv
    kh_ref[...] = kmod * e_tail
    bt_ref[...] = a * e_inv
    bh_ref[...] = a * e_tail
    at_ref[...] = e_ex * e_m
    v_ref[...] = v
    for bi in range(RB):
        dec_ref[bi:bi + 1, :] = jnp.exp(c[bi * L + L - 1:bi * L + L, :])
    rkb_ref[...] = r * kmod * rk_ref[...]
    kkr_ref[...] = kkv

    PW = 2 * HEAD_N
    pairs = range(RWKV_HEADS // 2)
    lane = lax.broadcasted_iota(jnp.int32, (L, PW), 1)
    trow = lax.broadcasted_iota(jnp.int32, (L, PW), 0)
    first = lane < HEAD_N
    scol = jnp.where(first, lane, lane - HEAD_N)
    strict_p = trow > scol
    incl_p = trow >= scol
    eye_p = (trow == scol).astype(f32)
    rb =lax.broadcasted_iota(jnp.int32, (PW, PW), 0)
    lb = lax.broadcasted_iota(jnp.int32, (PW, PW), 1)
    bd_mask = (rb < HEAD_N) == (lb < HEAD_N)

    def bd(x):
        zero = jnp.zeros_like(x)
        return jnp.concatenate([jnp.where(first, x, zero), jnp.where(first, zero, x)], axis=0)

    def segsum(x):
        s0 = jnp.sum(jnp.where(first, x, 0.0), axis=-1, keepdims=True)
        s1 = jnp.sum(jnp.where(first, 0.0, x), axis=-1, keepdims=True)
        return jnp.where(first, s0, s1)

    units = [(bi, p) for bi in range(RB) for p in pairs]
    ps = [(slice(bi * L, (bi + 1) * L), slice(p * PW, (p + 1) * PW)) for bi, p in units]
    kkr = [kkr_ref[s] for s in ps]
    ssq = [segsum(x * x) for x in kkr]
    kkh = [x / jnp.maximum(jnp.sqrt(q), 1e-12) for x, q in zip(kkr, ssq)]
    at0 = [(-kh_ * at0_ref[s]).astype(bf16) for kh_, s in zip(kkh, ps)]
    at = [(-kh_ * at_ref[s]).astype(bf16) for kh_, s in zip(kkh, ps)]
    bt = [(kh_ * bt_ref[s]).astype(bf16) for kh_, s in zip(kkh, ps)]
    bh = [(kh_ * bh_ref[s]).astype(bf16) for kh_, s in zip(kkh, ps)]
    vb = [v_ref[s].astype(bf16) for s in ps]

    lhs = [jnp.concatenate([a_, rt_ref[s].astype(bf16)], axis=0) for a_, s in zip(at, ps)]
    sb = [_dot(l_, bd(b_), _NT) for l_, b_ in zip(lhs, bt)]
    sk = [_dot(l_, bd(kt_ref[s].astype(bf16)), _NT) for l_, s in zip(lhs, ps)]
    a_ab = [jnp.where(strict_p, x[:L], 0.0) for x in sb]
    a_rb = [jnp.where(incl_p, x[L:], 0.0).astype(bf16) for x in sb]
    a_ak = [jnp.where(strict_p, x[:L], 0.0).astype(bf16) for x in sk]
    a_rk = [jnp.where(incl_p, x[L:], 0.0).astype(bf16) for x in sk]

    tinv = [eye_p + x for x in a_ab]
    xp = a_ab
    n = 2
    while n < L:
        xb = [x.astype(bf16) for x in xp]
        xp = [_dot(x, bd(x)) for x in xb]
        tinv = [t + _dot(t.astype(bf16), bd(x.astype(bf16))) for t, x in zip(tinv, xp)]
        n *= 2
    tb = [t.astype(bf16) for t in tinv]

    akv = [_dot(a_, bd(v_)).astype(bf16) for a_, v_ in zip(a_ak, vb)]
    w12 = [_dot(t, jnp.concatenate([bd(a_), bd(k_)], axis=1)) for t, a_, k_ in zip(tb, at0, akv)]

    n_pairs = len(pairs)
    s0 = [state_ref[bi * n_pairs + p] for bi, p in units]
    s0b = [s.astype(bf16) for s in s0]
    uy = [_dot(jnp.concatenate([w[:, :PW].astype(bf16), rt0_ref[s].astype(bf16)], axis=0), sb_, _NT)
          for w, s, sb_ in zip(w12, ps, s0b)]
    u = [x[:L] + w[:, PW:] for x, w in zip(uy, w12)]
    ub = [x.astype(bf16) for x in u]
    y = [x[L:] + _dot(jnp.concatenate([rb_, rk_], axis=1), jnp.concatenate([bd(u_), bd(v_)], axis=0))
         for x, rb_, rk_, u_, v_ in zip(uy, a_rb, a_rk, ub, vb)]
    for i, (bi, p) in enumerate(units):
        upd = _dot(jnp.concatenate([ub[i], vb[i]], axis=0),
                   jnp.concatenate([bh[i], kh_ref[ps[i]].astype(bf16)], axis=0), _TN)
        state_ref[bi * n_pairs + p] = (s0[i] * dec_ref[bi:bi + 1, ps[i][1]]
                                       + jnp.where(bd_mask, upd, 0.0))

    inv_n = 1.0 / HEAD_N
    mean = [segsum(x) * inv_n for x in y]
    yc = [x - m for x, m in zip(y, mean)]
    var = [segsum(x * x) * inv_n for x in yc]
    bonus = [segsum(rkb_ref[s]) for s in ps]
    for i, (bi, p) in enumerate(units):
        s = ps[i]
        yn = yc[i] * lax.rsqrt(var[i] + RWKV_GN_EPS) * lnw_ref[:, s[1]] + lnb_ref[:, s[1]]
        o_ref[bi, :, s[1]] = ((yn + bonus[i] * v_ref[s]) * g_ref[s]).astype(o_ref.dtype)


def _rwkv(proj3, mu, w0, w2d, a0, a2, g2, k_k, k_a, r_k, ln_w, ln_b):
    B, S, NP = proj3.shape
    L, W = RWKV_CHUNK, RWKV_WIDTH
    row = lambda n: pl.BlockSpec((1, n), lambda b, c: (0, 0))
    full = lambda a: pl.BlockSpec(a.shape, lambda b, c: (0, 0))
    RB = _RWKV_SEQS if B % _RWKV_SEQS == 0 else 1
    return pl.pallas_call(
        _rwkv_kernel,
        out_shape=jax.ShapeDtypeStruct((B, S, W), bf16),
        grid=(B // RB, S // L),
        in_specs=[pl.BlockSpec((RB, L, SHIFT_PAD), lambda b, c: (b, c, 0)),
                  row(SHIFT_PAD), row(W), full(w2d), row(W), full(a2), full(g2),
                  row(W), row(W), row(W), row(W), row(W)],
        out_specs=pl.BlockSpec((RB, L, W), lambda b, c: (b, c, 0)),
        scratch_shapes=[pltpu.VMEM((RB * RWKV_HEADS // 2, 2 * HEAD_N, 2 * HEAD_N), f32),
                        pltpu.VMEM((RB, SHIFT_PAD), f32), pltpu.VMEM((RB, W), f32),
                        pltpu.VMEM((_RWKV_NWIDE, RB * L, W), f32)],
        compiler_params=pltpu.CompilerParams(
            dimension_semantics=("arbitrary", "arbitrary"), vmem_limit_bytes=VMEM_LIMIT),
        name="rwkv",
    )(proj3, mu, w0, w2d, a0, a2, g2, k_k, k_a, r_k, ln_w, ln_b)


def _rope(x, cosf, sinf, lane):
    rot = jnp.where(lane < ROPE_DIM // 2, pltpu.roll(x, ATTN_E - ROPE_DIM // 2, axis=1),
                    pltpu.roll(x, ROPE_DIM // 2, axis=1))
    return x * cosf + rot * sinf


def _attn_kernel(q_ref, kc_ref, kp_ref, vc_ref, vp_ref, cc_ref, sc_ref, cp_ref, sp_ref,
                 o_ref, l_ref, *stage, dil, hps, nblk, wave):
    n = pl.program_id(1)
    blk = ATTN_BLK
    span = blk * dil
    qi = lax.broadcasted_iota(jnp.int32, (blk, blk), 0)
    kj = lax.broadcasted_iota(jnp.int32, (blk, blk), 1)
    lane = lax.broadcasted_iota(jnp.int32, (blk, ATTN_E), 1)
    cur_ok = kj <= qi
    prev_ok = kj >= qi
    first_ok = jnp.logical_and(prev_ok, n > 0)
    scale = ATTN_E ** -0.5

    ins = (q_ref, kc_ref, kp_ref, vc_ref, vp_ref, cc_ref, sc_ref, cp_ref, sp_ref)
    outs = (o_ref, l_ref)
    Q, KC, KP, VC, VP, CC, SC, CP, SP = range(len(ins))
    sub = dil // _STRIDE_SPLIT if stage else dil

    def rows(b, r, d):
        return pl.ds(b * span + r, blk, stride=d) if d > 1 else pl.ds(b * span, blk)

    def cols(hh):
        return slice(hh * ATTN_E, (hh + 1) * ATTN_E)

    if stage:
        in_stage, out_stage = stage
        quarter = span // _STRIDE_SPLIT
        for xi, ref in enumerate(ins):
            for r4 in range(_STRIDE_SPLIT):
                in_stage[xi, r4] = ref[0, pl.ds(r4, quarter, stride=_STRIDE_SPLIT), :]

        def rd(xi, b, r, hh):
            return in_stage[xi, r % _STRIDE_SPLIT, rows(0, r // _STRIDE_SPLIT, sub), :]

        def wr(oi, b, r, hh, val):
            out_stage[oi, r % _STRIDE_SPLIT, rows(0, r // _STRIDE_SPLIT, sub), :] = val
    else:
        def rd(xi, b, r, hh):
            return ins[xi][0, rows(b, r, dil), cols(hh) if xi < CC else slice(None)]

        def wr(oi, b, r, hh, val):
            outs[oi][0, rows(b, r, dil), cols(hh)] = val

    all_units = [(b, r, hh) for b in range(nblk) for r in range(dil) for hh in range(hps)]
    kc, vc = {}, {}
    for w0 in range(0, len(all_units), wave):
        units = all_units[w0:w0 + wave]
        tab = {u[:2]: (rd(CC, *u), rd(SC, *u)) for u in units}
        q = {u: _rope(rd(Q, *u), *tab[u[:2]], lane).astype(bf16) for u in units}
        for u in units:
            kc[u] = _rope(rd(KC, *u), *tab[u[:2]], lane).astype(bf16)
            vc[u] = rd(VC, *u).astype(bf16)
        kp, vp = {}, {}
        for (b, r, hh) in units:
            if b == 0:
                ptab = (rd(CP, 0, r, hh), rd(SP, 0, r, hh))
                kp[(b, r, hh)] = _rope(rd(KP, 0, r, hh), *ptab, lane).astype(bf16)
                vp[(b, r, hh)] = rd(VP, 0, r, hh).astype(bf16)
            else:
                kp[(b, r, hh)] = kc[(b - 1, r, hh)]
                vp[(b, r, hh)] = vc[(b - 1, r, hh)]
        s_c = {u: jnp.where(cur_ok, _dot(q[u], kc[u], _NT) * scale, -jnp.inf) for u in units}
        s_p = {u: jnp.where(first_ok if u[0] == 0 else prev_ok, _dot(q[u], kp[u], _NT) * scale, -jnp.inf)
               for u in units}
        m = {u: jnp.maximum(jnp.max(s_c[u], axis=-1, keepdims=True),
                            jnp.max(s_p[u], axis=-1, keepdims=True)) for u in units}
        p_c = {u: jnp.exp(s_c[u] - m[u]) for u in units}
        p_p = {u: jnp.exp(s_p[u] - m[u]) for u in units}
        l = {u: jnp.sum(p_c[u], axis=-1, keepdims=True) + jnp.sum(p_p[u], axis=-1, keepdims=True)
             for u in units}
        acc = {u: _dot(p_c[u].astype(bf16), vc[u]) + _dot(p_p[u].astype(bf16), vp[u]) for u in units}
        for u in units:
            wr(0, *u, acc[u] / l[u])
            wr(1, *u, jnp.broadcast_to(m[u] + jnp.log(l[u]), (blk, ATTN_E)))
    if stage:
        for oi, ref in enumerate(outs):
            for r4 in range(_STRIDE_SPLIT):
                ref[0, pl.ds(r4, quarter, stride=_STRIDE_SPLIT), :] = out_stage[oi, r4]


def _attn_group(proj3, cosf, sinf, gi, dil):
    B, S, NP = proj3.shape
    span = ATTN_BLK * dil
    hps = HEADS_PER_GROUP if dil == 1 else 1
    nblk = _ATTN_UNITS // (dil * hps)
    R = span * nblk
    cw = hps * ATTN_E
    per = ATTN_GW // cw
    qo = (ATTN_OFF + gi * ATTN_GW) // cw
    ko = qo + ATTN_WIDTH // cw
    vo = ko + ATTN_WIDTH // cw

    def cur(off):
        return pl.BlockSpec((1, R, cw), lambda b, n, h: (b, n, off + h))

    def prev(off):
        return pl.BlockSpec((1, span, cw), lambda b, n, h: (b, jnp.maximum(n * nblk - 1, 0), off + h))

    tc = pl.BlockSpec((1, R, ATTN_E), lambda b, n, h: (b, n, 0))
    tp = pl.BlockSpec((1, span, ATTN_E), lambda b, n, h: (b, jnp.maximum(n * nblk - 1, 0), 0))
    out = pl.BlockSpec((1, R, cw), lambda b, n, h: (b, n, h))
    staged = dil > _STRIDE_SPLIT and nblk == 1 and hps == 1
    scratch = ([pltpu.VMEM((9, _STRIDE_SPLIT, R // _STRIDE_SPLIT, ATTN_E), f32),
                pltpu.VMEM((2, _STRIDE_SPLIT, R // _STRIDE_SPLIT, ATTN_E), f32)] if staged else [])
    o, l = pl.pallas_call(
        functools.partial(_attn_kernel, dil=dil, hps=hps, nblk=nblk,
                          wave=_ATTN_UNITS // 2 if dil > HEADS_PER_GROUP else _ATTN_UNITS),
        out_shape=[jax.ShapeDtypeStruct((B, S, ATTN_GW), f32)] * 2,
        grid=(B, S // R, per),
        in_specs=[cur(qo), cur(ko), prev(ko), cur(vo), prev(vo), tc, tc, tp, tp],
        out_specs=[out, out],
        scratch_shapes=scratch,
        compiler_params=pltpu.CompilerParams(
            dimension_semantics=("arbitrary", "arbitrary", "arbitrary"),
            vmem_limit_bytes=VMEM_LIMIT),
        name=f"attn_g{gi}",
    )(proj3, proj3, proj3, proj3, proj3, cosf, sinf, cosf, sinf)
    return o.reshape(B * S, ATTN_GW), l.reshape(B * S, ATTN_GW)


def _merge_kernel(rw_ref, o0_ref, o1_ref, o2_ref, l0_ref, l1_ref, l2_ref, ga_ref, gb_ref,
                  wor_ref, woa_ref, mg_ref):
    l0, l1, l2 = l0_ref[...], l1_ref[...], l2_ref[...]
    m = jnp.maximum(jnp.maximum(l0, l1), l2)
    e0, e1, e2 = jnp.exp(l0 - m), jnp.exp(l1 - m), jnp.exp(l2 - m)
    o = (e0 * o0_ref[...] + e1 * o1_ref[...] + e2 * o2_ref[...]) / (e0 + e1 + e2)
    y_b = _dot(o.astype(bf16), woa_ref[...])
    y_a = _dot(rw_ref[...], wor_ref[...])
    merged = jax.nn.sigmoid(ga_ref[...]) * y_a + jax.nn.sigmoid(gb_ref[...]) * y_b
    mg_ref[...] = merged.astype(bf16)


def _out_kernel(x_ref, mg_ref, wout_ref, gn_ref, wr_ref, br_ref, x1_ref, h2_ref, lg_ref):
    x1 = x_ref[...] + _dot(mg_ref[...], wout_ref[...])
    x1_ref[...] = x1
    ms = jnp.mean(x1 * x1, axis=-1, keepdims=True)
    h2 = x1 * lax.rsqrt(ms + NORM_EPS) * gn_ref[...]
    h2_ref[...] = h2.astype(bf16)
    hi, lo = _split2(h2)
    n_e = lg_ref.shape[1]
    both = _dot(hi, wr_ref[...])
    lg_ref[...] = both[:, :n_e] + both[:, n_e:] + _dot(lo, wr_ref[:, :n_e]) + br_ref[...]


def _post(x2, rw, outs, lses, proj, wor, woa, wout, gn, wr, br, tm=512):
    T, D = x2.shape
    row = lambda w: pl.BlockSpec((tm, w), lambda i: (i, 0))
    const = lambda a: pl.BlockSpec(a.shape, lambda i: (0, 0), pipeline_mode=pl.Buffered(1))
    ga_blk = GATE_OFF // D
    params = pltpu.CompilerParams(dimension_semantics=("arbitrary",), vmem_limit_bytes=VMEM_LIMIT)
    merged = pl.pallas_call(
        _merge_kernel,
        out_shape=jax.ShapeDtypeStruct((T, D), bf16),
        grid=(T // tm,),
        in_specs=[row(RWKV_WIDTH)] + [row(ATTN_GW)] * 6
                 + [pl.BlockSpec((tm, D), lambda i: (i, ga_blk)),
                    pl.BlockSpec((tm, D), lambda i: (i, ga_blk + 1)),
                    const(wor), const(woa)],
        out_specs=row(D),
        compiler_params=params,
        name="merge",
    )(rw, *outs, *lses, proj, proj, wor, woa)
    return pl.pallas_call(
        _out_kernel,
        out_shape=[jax.ShapeDtypeStruct((T, D), f32), jax.ShapeDtypeStruct((T, D), bf16),
                   jax.ShapeDtypeStruct((T, 128), f32)],
        grid=(T // tm,),
        in_specs=[row(D), row(D), const(wout), const(gn), const(wr), const(br)],
        out_specs=[row(D), row(D), row(128)],
        compiler_params=params,
        name="out",
    )(x2, merged, wout, gn, wr, br)


def _moe_up_kernel(st, sc, se, sf, nv, xs_ref, w1g_ref, w1l_ref, b1g_ref, b1l_ref, *rest):
    o_ref, wg_ref, wl_ref = rest[-3:]
    s = pl.program_id(0)

    @pl.when(s < nv[0])
    def _():
        @pl.when(sf[s] == 1)
        def _():
            wg_ref[...] = w1g_ref[...].astype(bf16)
            wl_ref[...] = w1l_ref[...].astype(bf16)

        xs = xs_ref[...]
        hg = _dot(xs, wg_ref[...]) + b1g_ref[...]
        hl = _dot(xs, wl_ref[...]) + b1l_ref[...]
        x_glu = jnp.minimum(hg, SWIGLU_LIMIT)
        x_lin = jnp.clip(hl, -SWIGLU_LIMIT, SWIGLU_LIMIT)
        act = x_glu * jax.nn.sigmoid(SWIGLU_ALPHA * x_glu) * (x_lin + 1.0)
        o_ref[...] = act.astype(o_ref.dtype)


def _moe_down_kernel(st, sc, se, sf, nv, a_ref, w2_ref, b2_ref, o_ref, w_ref):
    s = pl.program_id(0)

    @pl.when(s < nv[0])
    def _():
        @pl.when(sf[s] == 1)
        def _():
            w_ref[...] = w2_ref[...].astype(bf16)

        o_ref[...] = (_dot(a_ref[...], w_ref[...]) + b2_ref[...]).astype(o_ref.dtype)


def _moe_schedule(tile_e, n_valid, first_tile, n_tiles_e, nc, lo, hi):
    i32 = jnp.int32
    E = first_tile.shape[0]
    experts = jnp.arange(E, dtype=i32)
    nv = jnp.clip(n_valid[0], lo, hi) - lo
    s = jnp.arange((hi - lo) * nc, dtype=i32)
    s = jnp.clip(s, 0, jnp.maximum(nc * nv - 1, 0))
    e = tile_e[lo:hi][s // nc]
    pick = lambda table: jnp.sum(jnp.where(e[:, None] == experts[None, :], table[None, :], 0), axis=1)
    ft = jnp.clip(first_tile, lo, hi)
    ne = jnp.clip(first_tile + n_tiles_e, lo, hi) - ft
    ft, ne = pick(ft), jnp.maximum(pick(ne), 1)
    local = s - nc * (ft - lo)
    col = local // ne
    row = ft + local % ne
    changed = jnp.logical_or(e != jnp.roll(e, 1), col != jnp.roll(col, 1)).at[0].set(True)
    return (row.astype(i32), col.astype(i32), e.astype(i32), changed.astype(i32),
            (nc * nv).astype(i32).reshape(1))


def _moe(tile_e, n_valid, first_tile, n_tiles_e, h2, tok, w1, b1, w2, b2, tm, tf=1024, tn=2048):
    P = tok.shape[0]
    D = h2.shape[1]
    E, _, F2 = w1.shape
    F = F2 // 2
    nj = F // tf
    nn = D // tn
    nt = P // tm
    params = pltpu.CompilerParams(dimension_semantics=("arbitrary",), vmem_limit_bytes=VMEM_LIMIT)

    bounds = sorted({(f * nt) // _MOE_CHUNK_TENTHS[-1] for f in _MOE_CHUNK_TENTHS})
    act = None
    for ck, (lo, hi) in enumerate(zip(bounds[:-1], bounds[1:])):
        ct = hi - lo
        sched = _moe_schedule(tile_e, n_valid, first_tile, n_tiles_e, nj, lo, hi)
        xs_k = h2[tok[lo * tm:hi * tm]]
        carry = [] if act is None else [act]
        act = pl.pallas_call(
            _moe_up_kernel,
            out_shape=jax.ShapeDtypeStruct((P, F), bf16),
            grid_spec=pltpu.PrefetchScalarGridSpec(
                num_scalar_prefetch=5,
                grid=(ct * nj,),
                in_specs=[
                    pl.BlockSpec((tm, D), lambda s, st, sc, se, sf, nv, lo=lo: (st[s] - lo, 0)),
                    pl.BlockSpec((None, D, tf), lambda s, st, sc, se, sf, nv: (se[s], 0, sc[s])),
                    pl.BlockSpec((None, D, tf), lambda s, st, sc, se, sf, nv: (se[s], 0, nj + sc[s])),
                    pl.BlockSpec((None, 1, tf), lambda s, st, sc, se, sf, nv: (se[s], 0, sc[s])),
                    pl.BlockSpec((None, 1, tf), lambda s, st, sc, se, sf, nv: (se[s], 0, nj + sc[s])),
                ] + [pl.BlockSpec(memory_space=pl.ANY)] * len(carry),
                out_specs=pl.BlockSpec((tm, tf), lambda s, st, sc, se, sf, nv: (st[s], sc[s])),
                scratch_shapes=[pltpu.VMEM((D, tf), bf16), pltpu.VMEM((D, tf), bf16)],
            ),
            input_output_aliases={10: 0} if carry else {},
            compiler_params=params,
            name=f"moe_up{ck}",
        )(*sched, xs_k, w1, w1, b1, b1, *carry)

    sched = _moe_schedule(tile_e, n_valid, first_tile, n_tiles_e, nn, 0, nt)
    return pl.pallas_call(
        _moe_down_kernel,
        out_shape=jax.ShapeDtypeStruct((P, D), bf16),
        grid_spec=pltpu.PrefetchScalarGridSpec(
            num_scalar_prefetch=5,
            grid=(nt * nn,),
            in_specs=[
                pl.BlockSpec((tm, F), lambda s, st, sc, se, sf, nv: (st[s], 0)),
                pl.BlockSpec((None, F, tn), lambda s, st, sc, se, sf, nv: (se[s], 0, sc[s])),
                pl.BlockSpec((None, 1, tn), lambda s, st, sc, se, sf, nv: (se[s], 0, sc[s])),
            ],
            out_specs=pl.BlockSpec((tm, tn), lambda s, st, sc, se, sf, nv: (st[s], sc[s])),
            scratch_shapes=[pltpu.VMEM((F, tn), bf16)],
        ),
        compiler_params=params,
        name="moe_down",
    )(*sched, act, w2, b2)


def _final_kernel(x1_ref, y_ref, gate_ref, g_ref, *rest, apply_norm):
    o_ref = rest[-1]
    gates = gate_ref[...]
    y = gates[:, 0:1] * y_ref[0].astype(f32)
    for kk in range(1, TOP_K):
        y = y + gates[:, kk:kk + 1] * y_ref[kk].astype(f32)
    x2 = x1_ref[...] + y
    if apply_norm:
        ms = jnp.mean(x2 * x2, axis=-1, keepdims=True)
        x2 = x2 * lax.rsqrt(ms + NORM_EPS) * g_ref[...]
    o_ref[...] = x2


def _final(x1, ys, pos, gates, g, apply_norm, tm=256):
    T, D = x1.shape
    n_chunks = _FINAL_CHUNKS if T % (_FINAL_CHUNKS * tm) == 0 else 1
    tc = T // n_chunks
    nb = tc // tm
    pos = pos.reshape(TOP_K, T)
    out = None
    for ck in range(n_chunks):
        y4 = ys[pos[:, ck * tc:(ck + 1) * tc].reshape(-1)].reshape(TOP_K, tc, D)
        carry = [] if out is None else [out]
        out = pl.pallas_call(
            functools.partial(_final_kernel, apply_norm=apply_norm),
            out_shape=jax.ShapeDtypeStruct((T, D), f32),
            grid=(nb,),
            in_specs=[pl.BlockSpec((tm, D), lambda i, o=ck * nb: (o + i, 0)),
                      pl.BlockSpec((TOP_K, tm, D), lambda i: (0, i, 0)),
                      pl.BlockSpec((tm, TOP_K), lambda i, o=ck * nb: (o + i, 0)),
                      pl.BlockSpec((1, D), lambda i: (0, 0))]
                     + [pl.BlockSpec(memory_space=pl.ANY)] * len(carry),
            out_specs=pl.BlockSpec((tm, D), lambda i, o=ck * nb: (o + i, 0)),
            input_output_aliases={4: 0} if carry else {},
            compiler_params=pltpu.CompilerParams(
                dimension_semantics=("arbitrary",), vmem_limit_bytes=VMEM_LIMIT),
            name=f"final{ck}",
        )(x1, y4, gates, g, *carry)
    return out


def _pad_cols(a, n):
    return jnp.pad(a, ((0, 0), (0, n - a.shape[1])))


def _pad_rows(a, n):
    return jnp.pad(a, ((0, n - a.shape[0]), (0, 0)))


def _regroup_cols(a):
    W = RWKV_WIDTH
    o = 3 * W
    return jnp.concatenate([
        a[:, :o],
        _pad_cols(a[:, o:o + DECAY_LORA], 128),
        _pad_cols(a[:, o + DECAY_LORA:o + DECAY_LORA + ICLR_LORA], 128),
        _pad_cols(a[:, o + DECAY_LORA + ICLR_LORA:o + DECAY_LORA + ICLR_LORA + GATE_LORA], 256),
        a[:, o + DECAY_LORA + ICLR_LORA + GATE_LORA:],
    ], axis=1)


def _rope_tables(positions):
    half = ROPE_DIM // 2
    inv_freq = ROPE_THETA ** (-jnp.arange(half, dtype=f32) / half)
    ang = positions.astype(f32)[..., None] * inv_freq
    cos, sin = jnp.cos(ang), jnp.sin(ang)
    B, S = positions.shape
    cosf = jnp.concatenate([cos, cos, jnp.ones((B, S, ATTN_E - ROPE_DIM), f32)], axis=-1)
    sinf = jnp.concatenate([-sin, sin, jnp.zeros((B, S, ATTN_E - ROPE_DIM), f32)], axis=-1)
    return cosf, sinf


def _route(logits, tm, n_tiles):
    T, E = logits.shape
    n = T * TOP_K
    i32 = jnp.int32
    top_val, top_idx = lax.top_k(logits, TOP_K)
    gates = jax.nn.softmax(top_val, axis=-1)
    flat_e = top_idx.reshape(-1).astype(i32)
    iota = jnp.arange(n, dtype=i32)
    experts = jnp.arange(E, dtype=i32)

    def lookup(table, idx):
        return jnp.sum(jnp.where(idx[:, None] == experts[None, :], table[None, :], 0), axis=1)

    _, order = lax.sort((flat_e, iota), num_keys=1, is_stable=True)
    _, rank = lax.sort((order, iota), num_keys=1)
    sizes = jnp.sum((flat_e[:, None] == experts[None, :]).astype(i32), axis=0)
    padded = ((sizes + tm - 1) // tm) * tm
    pad_end = jnp.cumsum(padded)
    pad_start = pad_end - padded
    start = jnp.cumsum(sizes) - sizes
    pos = lookup(pad_start - start, flat_e) + rank
    tile_start = jnp.arange(n_tiles, dtype=i32) * tm
    tile_e = jnp.minimum(jnp.sum((pad_end[None, :] <= tile_start[:, None]).astype(i32), axis=1), E - 1)
    n_valid = (pad_end[-1] // tm).astype(i32).reshape(1)
    row_e = jnp.repeat(tile_e, tm)
    rows = jnp.arange(n_tiles * tm, dtype=i32)
    off = rows - lookup(pad_start, row_e)
    src = jnp.clip(lookup(start, row_e) + off, 0, n - 1)
    tok = jnp.where(off < lookup(sizes, row_e), order[src] // TOP_K, rows % T)
    pos_slot_major = pos.reshape(T, TOP_K).T.reshape(-1)
    return tok, gates, pos_slot_major, tile_e, n_valid, pad_start // tm, padded // tm


def kernel(x, positions, norm_mix, w_in, shift_mu, w0, w2_decay, a0, a2_iclr, g2_gate, k_k, k_a,
           r_k, ln_x_w, ln_x_b, w_o_rwkv, w_o_attn, w_out, norm_ffn, w_router, b_router,
           w1, b1, w2, b2, norm_final):
    B, S, D = x.shape
    T = B * S
    E = w_router.shape[-1]
    depth = norm_mix.shape[0]
    xt = x.reshape(T, D)
    cosf, sinf = _rope_tables(positions)
    moe_tm = 512
    n_tiles = (T * TOP_K) // moe_tm + E
    for layer in range(depth):
        wp = _regroup_cols(w_in[layer]).astype(bf16)
        mu = _regroup_cols(shift_mu[layer][None, :])
        proj = _proj(xt, norm_mix[layer][None, :], wp)
        proj3 = proj.reshape(B, S, -1)
        rw = _rwkv(proj3, mu, w0[layer][None, :], _stack_rhs(_pad_rows(w2_decay[layer], 128)),
                   a0[layer][None, :], _stack_rhs(_pad_rows(a2_iclr[layer], 128)),
                   _pad_rows(g2_gate[layer], 256).astype(bf16), k_k[layer][None, :],
                   k_a[layer][None, :],
                   r_k[layer].reshape(1, -1), ln_x_w[layer][None, :], ln_x_b[layer][None, :])
        outs, lses = [], []
        for gi, (window, dil) in enumerate(ATTN_GROUPS):
            o_g, l_g = _attn_group(proj3, cosf, sinf, gi, dil)
            outs.append(o_g)
            lses.append(l_g)
        x1, h2, logits = _post(
            xt, rw.reshape(T, -1), outs, lses, proj,
            w_o_rwkv[layer].astype(bf16), w_o_attn[layer].astype(bf16), w_out[layer].astype(bf16),
            norm_ffn[layer][None, :], jnp.concatenate(_split2(_pad_cols(w_router[layer], 128)), axis=1),
            _pad_cols(b_router[layer][None, :], 128))
        tok, gates, pos, tile_e, n_valid, first_tile, n_tiles_e = _route(logits[:, :E], moe_tm, n_tiles)
        ys = _moe(tile_e, n_valid, first_tile, n_tiles_e, h2, tok, w1[layer], b1[layer][:, None, :],
                  w2[layer], b2[layer][:, None, :], moe_tm)
        xt = _final(x1, ys, pos, gates, norm_final[None, :], apply_norm=layer + 1 == depth)
    return xt.reshape(B, S, D)
```

```python
import functools

import jax
import jax.numpy as jnp
from jax import lax
from jax.experimental import pallas as pl
from jax.experimental.pallas import tpu as pltpu

f32 = jnp.float32
bf16 = jnp.bfloat16

NORM_EPS = 1e-5
RWKV_GN_EPS = 64e-5
ROPE_THETA = 500000.0
SWIGLU_LIMIT = 7.0
SWIGLU_ALPHA = 1.702

RWKV_HEADS = 16
HEAD_N = 64
RWKV_WIDTH = RWKV_HEADS * HEAD_N
DECAY_LORA = 64
ICLR_LORA = 64
GATE_LORA = 160
ATTN_GROUPS = ((128, 1), (512, 4), (2048, 16))
HEADS_PER_GROUP = 4
ATTN_E = 128
ATTN_GW = HEADS_PER_GROUP * ATTN_E
ATTN_WIDTH = len(ATTN_GROUPS) * ATTN_GW
ROPE_DIM = ATTN_E // 4
TOP_K = 4

LORA_PAD = 512
ZW_OFF, ZA_OFF, ZG_OFF = 3 * RWKV_WIDTH, 3 * RWKV_WIDTH + 128, 3 * RWKV_WIDTH + 256
SHIFT_PAD = 3 * RWKV_WIDTH + LORA_PAD
ATTN_OFF = SHIFT_PAD
GATE_OFF = ATTN_OFF + 3 * ATTN_WIDTH

RWKV_CHUNK = 64
_MOE_CHUNK_TENTHS = (0, 1, 4, 7, 10)
_FINAL_CHUNKS = 1
_RWKV_SEQS = 2
_RWKV_NWIDE = 12
ATTN_BLK = 128
_STRIDE_SPLIT = 4
_ATTN_UNITS = 16
VMEM_LIMIT = 56 * 1024 * 1024


def _dot(a, b, dims=(((1,), (0,)), ((), ())), precision=None):
    return lax.dot_general(a, b, dims, precision=precision, preferred_element_type=f32)


_NT = (((1,), (1,)), ((), ()))
_TN = (((0,), (0,)), ((), ()))


def _split2(a):
    hi = a.astype(bf16)
    return hi, (a - hi.astype(f32)).astype(bf16)


def _split3(a):
    hi = a.astype(bf16)
    rem = a - hi.astype(f32)
    mid = rem.astype(bf16)
    return hi, mid, (rem - mid.astype(f32)).astype(bf16)


def _stack_lhs(a):
    hi, lo = _split2(a)
    return jnp.concatenate([hi, lo, hi], axis=1)


def _stack_rhs(w):
    hi, lo = _split2(w)
    return jnp.concatenate([hi, hi, lo], axis=0)


def _proj_kernel(x_ref, g_ref, w_ref, o_ref, h_ref):
    @pl.when(pl.program_id(1) == 0)
    def _():
        x = x_ref[...]
        ms = jnp.mean(x * x, axis=-1, keepdims=True)
        h_ref[...] = (x * lax.rsqrt(ms + NORM_EPS) * g_ref[...]).astype(bf16)

    o_ref[...] = _dot(h_ref[...], w_ref[...])


def _proj(x2, g, wp, tm=1024, tn=2048):
    T, D = x2.shape
    NP = wp.shape[1]
    return pl.pallas_call(
        _proj_kernel,
        out_shape=jax.ShapeDtypeStruct((T, NP), f32),
        grid=(T // tm, NP // tn),
        in_specs=[pl.BlockSpec((tm, D), lambda i, j: (i, 0)),
                  pl.BlockSpec((1, D), lambda i, j: (0, 0)),
                  pl.BlockSpec((D, tn), lambda i, j: (0, j))],
        out_specs=pl.BlockSpec((tm, tn), lambda i, j: (i, j)),
        scratch_shapes=[pltpu.VMEM((tm, D), bf16)],
        compiler_params=pltpu.CompilerParams(
            dimension_semantics=("arbitrary", "arbitrary"), vmem_limit_bytes=VMEM_LIMIT),
        name="proj",
    )(x2, g, wp)


def _rwkv_kernel(p_ref, mu_ref, w0_ref, w2_ref, a0_ref, a2_ref, g2_ref, kk_ref, ka_ref,
                 rk_ref, lnw_ref, lnb_ref, o_ref, state_ref, prev_ref, dec_ref, wide_ref):
    (at0_ref, at_ref, rt0_ref, rt_ref, bt_ref, kt_ref, bh_ref, kh_ref, v_ref, rkb_ref, g_ref,
     kkr_ref) = [wide_ref.at[i] for i in range(_RWKV_NWIDE)]
    L = RWKV_CHUNK
    W = RWKV_WIDTH
    c_idx = pl.program_id(1)

    @pl.when(c_idx == 0)
    def _():
        state_ref[...] = jnp.zeros_like(state_ref)
        prev_ref[...] = jnp.zeros_like(prev_ref)

    RB = p_ref.shape[0]
    RL = RB * L
    p = p_ref[...].reshape(RL, p_ref.shape[2])
    row = lax.broadcasted_iota(jnp.int32, (RL, 1), 0)
    shifted = pltpu.roll(p, 1, axis=0)
    for bi in range(RB):
        shifted = jnp.where(row == bi * L, prev_ref[bi:bi + 1, :], shifted)
        prev_ref[bi:bi + 1, :] = p[bi * L + L - 1:bi * L + L, :]
    z = p + (shifted - p) * mu_ref[...]

    def per_seq(x, r):
        return jnp.concatenate(
            [jnp.broadcast_to(x[bi * L + r:bi * L + r + 1, :], (L, x.shape[1])) for bi in range(RB)],
            axis=0)

    r = z[:, 0:W]
    k = z[:, W:2 * W]
    v = z[:, 2 * W:3 * W]
    zw = z[:, ZW_OFF:ZW_OFF + 128]
    za = z[:, ZA_OFF:ZA_OFF + 128]
    zg = z[:, ZG_OFF:ZG_OFF + 256]

    w_raw = w0_ref[...] + _dot(_stack_lhs(jnp.tanh(zw)), w2_ref[...])
    sp = jnp.maximum(-w_raw, 0.0) + jnp.log(1.0 + jnp.exp(-jnp.abs(w_raw)))
    lw = -jnp.exp(-sp - 0.5)
    a = jax.nn.sigmoid(a0_ref[...] + _dot(_stack_lhs(za), a2_ref[...]))
    g_ref[...] = _dot(jax.nn.sigmoid(zg).astype(bf16), g2_ref[...])

    ti = lax.broadcasted_iota(jnp.int32, (RL, RL), 0)
    si = lax.broadcasted_iota(jnp.int32, (RL, RL), 1)
    tri = jnp.logical_and(ti >= si, ti // L == si // L).astype(bf16)
    lw_h, lw_m, lw_l = _split3(lw)
    c = _dot(jnp.concatenate([tri, tri, tri], axis=1), jnp.concatenate([lw_h, lw_m, lw_l], axis=0))
    cex = c - lw
    cm = per_seq(c, L // 2 - 1)
    cl = per_seq(c, L - 1)
    e_m = jnp.exp(-cm)
    e_c = jnp.exp(c)
    e_ex = jnp.exp(cex)
    e_inv = jnp.exp(cm - c)
    e_tail = jnp.exp(cl - c)

    kkv = k * kk_ref[...]
    kmod = k * (1.0 + (a - 1.0) * ka_ref[...])
    at0_ref[...] = e_ex
    rt0 = r * e_c
    rt0_ref[...] = rt0
    rt_ref[...] = rt0 * e_m
    kt_ref[...] = kmod * e_inv
    kh_ref[...] = kmod * e_tail
    bt_ref[...] = a * e_inv
    bh_ref[...] = a * e_tail
    at_ref[...] = e_ex * e_m
    v_ref[...] = v
    for bi in range(RB):
        dec_ref[bi:bi + 1, :] = jnp.exp(c[bi * L + L - 1:bi * L + L, :])
    rkb_ref[...] = r * kmod * rk_ref[...]
    kkr_ref[...] = kkv

    PW = 2 * HEAD_N
    pairs = range(RWKV_HEADS // 2)
    lane = lax.broadcasted_iota(jnp.int32, (L, PW), 1)
    trow = lax.broadcasted_iota(jnp.int32, (L, PW), 0)
    first = lane < HEAD_N
    scol = jnp.where(first, lane, lane - HEAD_N)
    strict_p = trow > scol
    incl_p = trow >= scol
    eye_p = (trow == scol).astype(f32)
    rb =lax.broadcasted_iota(jnp.int32, (PW, PW), 0)
    lb = lax.broadcasted_iota(jnp.int32, (PW, PW), 1)
    bd_mask = (rb < HEAD_N) == (lb < HEAD_N)

    def bd(x):
        zero = jnp.zeros_like(x)
        return jnp.concatenate([jnp.where(first, x, zero), jnp.where(first, zero, x)], axis=0)

    def segsum(x):
        s0 = jnp.sum(jnp.where(first, x, 0.0), axis=-1, keepdims=True)
        s1 = jnp.sum(jnp.where(first, 0.0, x), axis=-1, keepdims=True)
        return jnp.where(first, s0, s1)

    units = [(bi, p) for bi in range(RB) for p in pairs]
    ps = [(slice(bi * L, (bi + 1) * L), slice(p * PW, (p + 1) * PW)) for bi, p in units]
    kkr = [kkr_ref[s] for s in ps]
    ssq = [segsum(x * x) for x in kkr]
    kkh = [x / jnp.maximum(jnp.sqrt(q), 1e-12) for x, q in zip(kkr, ssq)]
    at0 = [(-kh_ * at0_ref[s]).astype(bf16) for kh_, s in zip(kkh, ps)]
    at = [(-kh_ * at_ref[s]).astype(bf16) for kh_, s in zip(kkh, ps)]
    bt = [(kh_ * bt_ref[s]).astype(bf16) for kh_, s in zip(kkh, ps)]
    bh = [(kh_ * bh_ref[s]).astype(bf16) for kh_, s in zip(kkh, ps)]
    vb = [v_ref[s].astype(bf16) for s in ps]

    lhs = [jnp.concatenate([a_, rt_ref[s].astype(bf16)], axis=0) for a_, s in zip(at, ps)]
    sb = [_dot(l_, bd(b_), _NT) for l_, b_ in zip(lhs, bt)]
    sk = [_dot(l_, bd(kt_ref[s].astype(bf16)), _NT) for l_, s in zip(lhs, ps)]
    a_ab = [jnp.where(strict_p, x[:L], 0.0) for x in sb]
    a_rb = [jnp.where(incl_p, x[L:], 0.0).astype(bf16) for x in sb]
    a_ak = [jnp.where(strict_p, x[:L], 0.0).astype(bf16) for x in sk]
    a_rk = [jnp.where(incl_p, x[L:], 0.0).astype(bf16) for x in sk]

    tinv = [eye_p + x for x in a_ab]
    xp = a_ab
    n = 2
    while n < L:
        xb = [x.astype(bf16) for x in xp]
        xp = [_dot(x, bd(x)) for x in xb]
        tinv = [t + _dot(t.astype(bf16), bd(x.astype(bf16))) for t, x in zip(tinv, xp)]
        n *= 2
    tb = [t.astype(bf16) for t in tinv]

    akv = [_dot(a_, bd(v_)).astype(bf16) for a_, v_ in zip(a_ak, vb)]
    w12 = [_dot(t, jnp.concatenate([bd(a_), bd(k_)], axis=1)) for t, a_, k_ in zip(tb, at0, akv)]

    n_pairs = len(pairs)
    s0 = [state_ref[bi * n_pairs + p] for bi, p in units]
    s0b = [s.astype(bf16) for s in s0]
    uy = [_dot(jnp.concatenate([w[:, :PW].astype(bf16), rt0_ref[s].astype(bf16)], axis=0), sb_, _NT)
          for w, s, sb_ in zip(w12, ps, s0b)]
    u = [x[:L] + w[:, PW:] for x, w in zip(uy, w12)]
    ub = [x.astype(bf16) for x in u]
    y = [x[L:] + _dot(jnp.concatenate([rb_, rk_], axis=1), jnp.concatenate([bd(u_), bd(v_)], axis=0))
         for x, rb_, rk_, u_, v_ in zip(uy, a_rb, a_rk, ub, vb)]
    for i, (bi, p) in enumerate(units):
        upd = _dot(jnp.concatenate([ub[i], vb[i]], axis=0),
                   jnp.concatenate([bh[i], kh_ref[ps[i]].astype(bf16)], axis=0), _TN)
        state_ref[bi * n_pairs + p] = (s0[i] * dec_ref[bi:bi + 1, ps[i][1]]
                                       + jnp.where(bd_mask, upd, 0.0))

    inv_n = 1.0 / HEAD_N
    mean = [segsum(x) * inv_n for x in y]
    yc = [x - m for x, m in zip(y, mean)]
    var = [segsum(x * x) * inv_n for x in yc]
    bonus = [segsum(rkb_ref[s]) for s in ps]
    for i, (bi, p) in enumerate(units):
        s = ps[i]
        yn = yc[i] * lax.rsqrt(var[i] + RWKV_GN_EPS) * lnw_ref[:, s[1]] + lnb_ref[:, s[1]]
        o_ref[bi, :, s[1]] = ((yn + bonus[i] * v_ref[s]) * g_ref[s]).astype(o_ref.dtype)


def _rwkv(proj3, mu, w0, w2d, a0, a2, g2, k_k, k_a, r_k, ln_w, ln_b):
    B, S, NP = proj3.shape
    L, W = RWKV_CHUNK, RWKV_WIDTH
    row = lambda n: pl.BlockSpec((1, n), lambda b, c: (0, 0))
    full = lambda a: pl.BlockSpec(a.shape, lambda b, c: (0, 0))
    RB = _RWKV_SEQS if B % _RWKV_SEQS == 0 else 1
    return pl.pallas_call(
        _rwkv_kernel,
        out_shape=jax.ShapeDtypeStruct((B, S, W), bf16),
        grid=(B // RB, S // L),
        in_specs=[pl.BlockSpec((RB, L, SHIFT_PAD), lambda b, c: (b, c, 0)),
                  row(SHIFT_PAD), row(W), full(w2d), row(W), full(a2), full(g2),
                  row(W), row(W), row(W), row(W), row(W)],
        out_specs=pl.BlockSpec((RB, L, W), lambda b, c: (b, c, 0)),
        scratch_shapes=[pltpu.VMEM((RB * RWKV_HEADS // 2, 2 * HEAD_N, 2 * HEAD_N), f32),
                        pltpu.VMEM((RB, SHIFT_PAD), f32), pltpu.VMEM((RB, W), f32),
                        pltpu.VMEM((_RWKV_NWIDE, RB * L, W), f32)],
        compiler_params=pltpu.CompilerParams(
            dimension_semantics=("arbitrary", "arbitrary"), vmem_limit_bytes=VMEM_LIMIT),
        name="rwkv",
    )(proj3, mu, w0, w2d, a0, a2, g2, k_k, k_a, r_k, ln_w, ln_b)


def _rope(x, cosf, sinf, lane):
    rot = jnp.where(lane < ROPE_DIM // 2, pltpu.roll(x, ATTN_E - ROPE_DIM // 2, axis=1),
                    pltpu.roll(x, ROPE_DIM // 2, axis=1))
    return x * cosf + rot * sinf


def _attn_kernel(q_ref, kc_ref, kp_ref, vc_ref, vp_ref, cc_ref, sc_ref, cp_ref, sp_ref,
                 o_ref, l_ref, *stage, dil, hps, nblk, wave):
    n = pl.program_id(1)
    blk = ATTN_BLK
    span = blk * dil
    qi = lax.broadcasted_iota(jnp.int32, (blk, blk), 0)
    kj = lax.broadcasted_iota(jnp.int32, (blk, blk), 1)
    lane = lax.broadcasted_iota(jnp.int32, (blk, ATTN_E), 1)
    cur_ok = kj <= qi
    prev_ok = kj >= qi
    first_ok = jnp.logical_and(prev_ok, n > 0)
    scale = ATTN_E ** -0.5

    ins = (q_ref, kc_ref, kp_ref, vc_ref, vp_ref, cc_ref, sc_ref, cp_ref, sp_ref)
    outs = (o_ref, l_ref)
    Q, KC, KP, VC, VP, CC, SC, CP, SP = range(len(ins))
    sub = dil // _STRIDE_SPLIT if stage else dil

    def rows(b, r, d):
        return pl.ds(b * span + r, blk, stride=d) if d > 1 else pl.ds(b * span, blk)

    def cols(hh):
        return slice(hh * ATTN_E, (hh + 1) * ATTN_E)

    if stage:
        in_stage, out_stage = stage
        quarter = span // _STRIDE_SPLIT
        for xi, ref in enumerate(ins):
            for r4 in range(_STRIDE_SPLIT):
                in_stage[xi, r4] = ref[0, pl.ds(r4, quarter, stride=_STRIDE_SPLIT), :]

        def rd(xi, b, r, hh):
            return in_stage[xi, r % _STRIDE_SPLIT, rows(0, r // _STRIDE_SPLIT, sub), :]

        def wr(oi, b, r, hh, val):
            out_stage[oi, r % _STRIDE_SPLIT, rows(0, r // _STRIDE_SPLIT, sub), :] = val
    else:
        def rd(xi, b, r, hh):
            return ins[xi][0, rows(b, r, dil), cols(hh) if xi < CC else slice(None)]

        def wr(oi, b, r, hh, val):
            outs[oi][0, rows(b, r, dil), cols(hh)] = val

    all_units = [(b, r, hh) for b in range(nblk) for r in range(dil) for hh in range(hps)]
    kc, vc = {}, {}
    for w0 in range(0, len(all_units), wave):
        units = all_units[w0:w0 + wave]
        tab = {u[:2]: (rd(CC, *u), rd(SC, *u)) for u in units}
        q = {u: _rope(rd(Q, *u), *tab[u[:2]], lane).astype(bf16) for u in units}
        for u in units:
            kc[u] = _rope(rd(KC, *u), *tab[u[:2]], lane).astype(bf16)
            vc[u] = rd(VC, *u).astype(bf16)
        kp, vp = {}, {}
        for (b, r, hh) in units:
            if b == 0:
                ptab = (rd(CP, 0, r, hh), rd(SP, 0, r, hh))
                kp[(b, r, hh)] = _rope(rd(KP, 0, r, hh), *ptab, lane).astype(bf16)
                vp[(b, r, hh)] = rd(VP, 0, r, hh).astype(bf16)
            else:
                kp[(b, r, hh)] = kc[(b - 1, r, hh)]
                vp[(b, r, hh)] = vc[(b - 1, r, hh)]
        s_c = {u: jnp.where(cur_ok, _dot(q[u], kc[u], _NT) * scale, -jnp.inf) for u in units}
        s_p = {u: jnp.where(first_ok if u[0] == 0 else prev_ok, _dot(q[u], kp[u], _NT) * scale, -jnp.inf)
               for u in units}
        m = {u: jnp.maximum(jnp.max(s_c[u], axis=-1, keepdims=True),
                            jnp.max(s_p[u], axis=-1, keepdims=True)) for u in units}
        p_c = {u: jnp.exp(s_c[u] - m[u]) for u in units}
        p_p = {u: jnp.exp(s_p[u] - m[u]) for u in units}
        l = {u: jnp.sum(p_c[u], axis=-1, keepdims=True) + jnp.sum(p_p[u], axis=-1, keepdims=True)
             for u in units}
        acc = {u: _dot(p_c[u].astype(bf16), vc[u]) + _dot(p_p[u].astype(bf16), vp[u]) for u in units}
        for u in units:
            wr(0, *u, acc[u] / l[u])
            wr(1, *u, jnp.broadcast_to(m[u] + jnp.log(l[u]), (blk, ATTN_E)))
    if stage:
        for oi, ref in enumerate(outs):
            for r4 in range(_STRIDE_SPLIT):
                ref[0, pl.ds(r4, quarter, stride=_STRIDE_SPLIT), :] = out_stage[oi, r4]


def _attn_group(proj3, cosf, sinf, gi, dil):
    B, S, NP = proj3.shape
    span = ATTN_BLK * dil
    hps = HEADS_PER_GROUP if dil == 1 else 1
    nblk = _ATTN_UNITS // (dil * hps)
    R = span * nblk
    cw = hps * ATTN_E
    per = ATTN_GW // cw
    qo = (ATTN_OFF + gi * ATTN_GW) // cw
    ko = qo + ATTN_WIDTH // cw
    vo = ko + ATTN_WIDTH // cw

    def cur(off):
        return pl.BlockSpec((1, R, cw), lambda b, n, h: (b, n, off + h))

    def prev(off):
        return pl.BlockSpec((1, span, cw), lambda b, n, h: (b, jnp.maximum(n * nblk - 1, 0), off + h))

    tc = pl.BlockSpec((1, R, ATTN_E), lambda b, n, h: (b, n, 0))
    tp = pl.BlockSpec((1, span, ATTN_E), lambda b, n, h: (b, jnp.maximum(n * nblk - 1, 0), 0))
    out = pl.BlockSpec((1, R, cw), lambda b, n, h: (b, n, h))
    staged = dil > _STRIDE_SPLIT and nblk == 1 and hps == 1
    scratch = ([pltpu.VMEM((9, _STRIDE_SPLIT, R // _STRIDE_SPLIT, ATTN_E), f32),
                pltpu.VMEM((2, _STRIDE_SPLIT, R // _STRIDE_SPLIT, ATTN_E), f32)] if staged else [])
    o, l = pl.pallas_call(
        functools.partial(_attn_kernel, dil=dil, hps=hps, nblk=nblk,
                          wave=_ATTN_UNITS // 2 if dil > HEADS_PER_GROUP else _ATTN_UNITS),
        out_shape=[jax.ShapeDtypeStruct((B, S, ATTN_GW), f32)] * 2,
        grid=(B, S // R, per),
        in_specs=[cur(qo), cur(ko), prev(ko), cur(vo), prev(vo), tc, tc, tp, tp],
        out_specs=[out, out],
        scratch_shapes=scratch,
        compiler_params=pltpu.CompilerParams(
            dimension_semantics=("arbitrary", "arbitrary", "arbitrary"),
            vmem_limit_bytes=VMEM_LIMIT),
        name=f"attn_g{gi}",
    )(proj3, proj3, proj3, proj3, proj3, cosf, sinf, cosf, sinf)
    return o.reshape(B * S, ATTN_GW), l.reshape(B * S, ATTN_GW)


def _merge_kernel(rw_ref, o0_ref, o1_ref, o2_ref, l0_ref, l1_ref, l2_ref, ga_ref, gb_ref,
                  wor_ref, woa_ref, mg_ref):
    l0, l1, l2 = l0_ref[...], l1_ref[...], l2_ref[...]
    m = jnp.maximum(jnp.maximum(l0, l1), l2)
    e0, e1, e2 = jnp.exp(l0 - m), jnp.exp(l1 - m), jnp.exp(l2 - m)
    o = (e0 * o0_ref[...] + e1 * o1_ref[...] + e2 * o2_ref[...]) / (e0 + e1 + e2)
    y_b = _dot(o.astype(bf16), woa_ref[...])
    y_a = _dot(rw_ref[...], wor_ref[...])
    merged = jax.nn.sigmoid(ga_ref[...]) * y_a + jax.nn.sigmoid(gb_ref[...]) * y_b
    mg_ref[...] = merged.astype(bf16)


def _out_kernel(x_ref, mg_ref, wout_ref, gn_ref, wr_ref, br_ref, x1_ref, h2_ref, lg_ref, *,
                n_experts):
    x1 = x_ref[...] + _dot(mg_ref[...], wout_ref[...])
    x1_ref[...] = x1
    ms = jnp.mean(x1 * x1, axis=-1, keepdims=True)
    h2 = x1 * lax.rsqrt(ms + NORM_EPS) * gn_ref[...]
    h2_ref[...] = h2.astype(bf16)
    hi, lo = _split2(h2)
    n_e = lg_ref.shape[1]
    both = _dot(hi, wr_ref[...])
    logits = both[:, :n_e] + both[:, n_e:] + _dot(lo, wr_ref[:, :n_e]) + br_ref[...]
    lane = lax.broadcasted_iota(jnp.int32, logits.shape, 1).astype(f32)
    cur = jnp.where(lane < n_experts, logits, -jnp.inf)
    vals, ids = [], []
    for _ in range(TOP_K):
        m = jnp.max(cur, axis=-1, keepdims=True)
        idx = jnp.min(jnp.where(cur == m, lane, float(n_e)), axis=-1, keepdims=True)
        vals.append(m)
        ids.append(idx)
        cur = jnp.where(lane == idx, -jnp.inf, cur)
    ex = [jnp.exp(v - vals[0]) for v in vals]
    den = ex[0]
    for e_ in ex[1:]:
        den = den + e_
    slab = jnp.zeros_like(logits)
    for kk in range(TOP_K):
        slab = jnp.where(lane == kk, ex[kk] / den, slab)
        slab = jnp.where(lane == TOP_K + kk, ids[kk], slab)
    lg_ref[...] = slab


def _post(x2, rw, outs, lses, proj, wor, woa, wout, gn, wr, br, n_experts, tm=512):
    T, D = x2.shape
    row = lambda w: pl.BlockSpec((tm, w), lambda i: (i, 0))
    const = lambda a: pl.BlockSpec(a.shape, lambda i: (0, 0), pipeline_mode=pl.Buffered(1))
    ga_blk = GATE_OFF // D
    params = pltpu.CompilerParams(dimension_semantics=("arbitrary",), vmem_limit_bytes=VMEM_LIMIT)
    merged = pl.pallas_call(
        _merge_kernel,
        out_shape=jax.ShapeDtypeStruct((T, D), bf16),
        grid=(T // tm,),
        in_specs=[row(RWKV_WIDTH)] + [row(ATTN_GW)] * 6
                 + [pl.BlockSpec((tm, D), lambda i: (i, ga_blk)),
                    pl.BlockSpec((tm, D), lambda i: (i, ga_blk + 1)),
                    const(wor), const(woa)],
        out_specs=row(D),
        compiler_params=params,
        name="merge",
    )(rw, *outs, *lses, proj, proj, wor, woa)
    return pl.pallas_call(
        functools.partial(_out_kernel, n_experts=n_experts),
        out_shape=[jax.ShapeDtypeStruct((T, D), f32), jax.ShapeDtypeStruct((T, D), bf16),
                   jax.ShapeDtypeStruct((T, 128), f32)],
        grid=(T // tm,),
        in_specs=[row(D), row(D), const(wout), const(gn), const(wr), const(br)],
        out_specs=[row(D), row(D), row(128)],
        compiler_params=params,
        name="out",
    )(x2, merged, wout, gn, wr, br)


def _moe_up_kernel(st, sc, se, sf, nv, xs_ref, w1g_ref, w1l_ref, b1g_ref, b1l_ref, *rest):
    o_ref, wg_ref, wl_ref = rest[-3:]
    s = pl.program_id(0)

    @pl.when(s < nv[0])
    def _():
        @pl.when(sf[s] == 1)
        def _():
            wg_ref[...] = w1g_ref[...].astype(bf16)
            wl_ref[...] = w1l_ref[...].astype(bf16)

        xs = xs_ref[...]
        hg = _dot(xs, wg_ref[...]) + b1g_ref[...]
        hl = _dot(xs, wl_ref[...]) + b1l_ref[...]
        x_glu = jnp.minimum(hg, SWIGLU_LIMIT)
        x_lin = jnp.clip(hl, -SWIGLU_LIMIT, SWIGLU_LIMIT)
        act = x_glu * jax.nn.sigmoid(SWIGLU_ALPHA * x_glu) * (x_lin + 1.0)
        o_ref[...] = act.astype(o_ref.dtype)


def _moe_down_kernel(st, sc, se, sf, nv, a_ref, w2_ref, b2_ref, o_ref, w_ref):
    s = pl.program_id(0)

    @pl.when(s < nv[0])
    def _():
        @pl.when(sf[s] == 1)
        def _():
            w_ref[...] = w2_ref[...].astype(bf16)

        o_ref[...] = (_dot(a_ref[...], w_ref[...]) + b2_ref[...]).astype(o_ref.dtype)


def _moe_schedule(tile_e, n_valid, first_tile, n_tiles_e, nc, lo, hi):
    i32 = jnp.int32
    E = first_tile.shape[0]
    experts = jnp.arange(E, dtype=i32)
    nv = jnp.clip(n_valid[0], lo, hi) - lo
    s = jnp.arange((hi - lo) * nc, dtype=i32)
    s = jnp.clip(s, 0, jnp.maximum(nc * nv - 1, 0))
    e = tile_e[lo:hi][s // nc]
    pick = lambda table: jnp.sum(jnp.where(e[:, None] == experts[None, :], table[None, :], 0), axis=1)
    ft = jnp.clip(first_tile, lo, hi)
    ne = jnp.clip(first_tile + n_tiles_e, lo, hi) - ft
    ft, ne = pick(ft), jnp.maximum(pick(ne), 1)
    local = s - nc * (ft - lo)
    col = local // ne
    row = ft + local % ne
    changed = jnp.logical_or(e != jnp.roll(e, 1), col != jnp.roll(col, 1)).at[0].set(True)
    return (row.astype(i32), col.astype(i32), e.astype(i32), changed.astype(i32),
            (nc * nv).astype(i32).reshape(1))


def _moe(tile_e, n_valid, first_tile, n_tiles_e, h2, tok, w1, b1, w2, b2, tm, tf=1024, tn=2048):
    P = tok.shape[0]
    D = h2.shape[1]
    E, _, F2 = w1.shape
    F = F2 // 2
    nj = F // tf
    nn = D // tn
    nt = P // tm
    params = pltpu.CompilerParams(dimension_semantics=("arbitrary",), vmem_limit_bytes=VMEM_LIMIT)

    bounds = sorted({(f * nt) // _MOE_CHUNK_TENTHS[-1] for f in _MOE_CHUNK_TENTHS})
    act = None
    for ck, (lo, hi) in enumerate(zip(bounds[:-1], bounds[1:])):
        ct = hi - lo
        sched = _moe_schedule(tile_e, n_valid, first_tile, n_tiles_e, nj, lo, hi)
        xs_k = h2[tok[lo * tm:hi * tm]]
        carry = [] if act is None else [act]
        act = pl.pallas_call(
            _moe_up_kernel,
            out_shape=jax.ShapeDtypeStruct((P, F), bf16),
            grid_spec=pltpu.PrefetchScalarGridSpec(
                num_scalar_prefetch=5,
                grid=(ct * nj,),
                in_specs=[
                    pl.BlockSpec((tm, D), lambda s, st, sc, se, sf, nv, lo=lo: (st[s] - lo, 0)),
                    pl.BlockSpec((None, D, tf), lambda s, st, sc, se, sf, nv: (se[s], 0, sc[s])),
                    pl.BlockSpec((None, D, tf), lambda s, st, sc, se, sf, nv: (se[s], 0, nj + sc[s])),
                    pl.BlockSpec((None, 1, tf), lambda s, st, sc, se, sf, nv: (se[s], 0, sc[s])),
                    pl.BlockSpec((None, 1, tf), lambda s, st, sc, se, sf, nv: (se[s], 0, nj + sc[s])),
                ] + [pl.BlockSpec(memory_space=pl.ANY)] * len(carry),
                out_specs=pl.BlockSpec((tm, tf), lambda s, st, sc, se, sf, nv: (st[s], sc[s])),
                scratch_shapes=[pltpu.VMEM((D, tf), bf16), pltpu.VMEM((D, tf), bf16)],
            ),
            input_output_aliases={10: 0} if carry else {},
            compiler_params=params,
            name=f"moe_up{ck}",
        )(*sched, xs_k, w1, w1, b1, b1, *carry)

    sched = _moe_schedule(tile_e, n_valid, first_tile, n_tiles_e, nn, 0, nt)
    return pl.pallas_call(
        _moe_down_kernel,
        out_shape=jax.ShapeDtypeStruct((P, D), bf16),
        grid_spec=pltpu.PrefetchScalarGridSpec(
            num_scalar_prefetch=5,
            grid=(nt * nn,),
            in_specs=[
                pl.BlockSpec((tm, F), lambda s, st, sc, se, sf, nv: (st[s], 0)),
                pl.BlockSpec((None, F, tn), lambda s, st, sc, se, sf, nv: (se[s], 0, sc[s])),
                pl.BlockSpec((None, 1, tn), lambda s, st, sc, se, sf, nv: (se[s], 0, sc[s])),
            ],
            out_specs=pl.BlockSpec((tm, tn), lambda s, st, sc, se, sf, nv: (st[s], sc[s])),
            scratch_shapes=[pltpu.VMEM((F, tn), bf16)],
        ),
        compiler_params=params,
        name="moe_down",
    )(*sched, act, w2, b2)


def _final_kernel(x1_ref, y_ref, gate_ref, g_ref, *rest, apply_norm):
    o_ref = rest[-1]
    gates = gate_ref[...]
    y = gates[:, 0:1] * y_ref[0].astype(f32)
    for kk in range(1, TOP_K):
        y = y + gates[:, kk:kk + 1] * y_ref[kk].astype(f32)
    x2 = x1_ref[...] + y
    if apply_norm:
        ms = jnp.mean(x2 * x2, axis=-1, keepdims=True)
        x2 = x2 * lax.rsqrt(ms + NORM_EPS) * g_ref[...]
    o_ref[...] = x2


def _final(x1, ys, pos, gates, g, apply_norm, tm=256):
    T, D = x1.shape
    n_chunks = _FINAL_CHUNKS if T % (_FINAL_CHUNKS * tm) == 0 else 1
    tc = T // n_chunks
    nb = tc // tm
    pos = pos.reshape(TOP_K, T)
    out = None
    for ck in range(n_chunks):
        y4 = ys[pos[:, ck * tc:(ck + 1) * tc].reshape(-1)].reshape(TOP_K, tc, D)
        carry = [] if out is None else [out]
        out = pl.pallas_call(
            functools.partial(_final_kernel, apply_norm=apply_norm),
            out_shape=jax.ShapeDtypeStruct((T, D), f32),
            grid=(nb,),
            in_specs=[pl.BlockSpec((tm, D), lambda i, o=ck * nb: (o + i, 0)),
                      pl.BlockSpec((TOP_K, tm, D), lambda i: (0, i, 0)),
                      pl.BlockSpec((tm, TOP_K), lambda i, o=ck * nb: (o + i, 0)),
                      pl.BlockSpec((1, D), lambda i: (0, 0))]
                     + [pl.BlockSpec(memory_space=pl.ANY)] * len(carry),
            out_specs=pl.BlockSpec((tm, D), lambda i, o=ck * nb: (o + i, 0)),
            input_output_aliases={4: 0} if carry else {},
            compiler_params=pltpu.CompilerParams(
                dimension_semantics=("arbitrary",), vmem_limit_bytes=VMEM_LIMIT),
            name=f"final{ck}",
        )(x1, y4, gates, g, *carry)
    return out


def _pad_cols(a, n):
    return jnp.pad(a, ((0, 0), (0, n - a.shape[1])))


def _pad_rows(a, n):
    return jnp.pad(a, ((0, n - a.shape[0]), (0, 0)))


def _regroup_cols(a):
    W = RWKV_WIDTH
    o = 3 * W
    return jnp.concatenate([
        a[:, :o],
        _pad_cols(a[:, o:o + DECAY_LORA], 128),
        _pad_cols(a[:, o + DECAY_LORA:o + DECAY_LORA + ICLR_LORA], 128),
        _pad_cols(a[:, o + DECAY_LORA + ICLR_LORA:o + DECAY_LORA + ICLR_LORA + GATE_LORA], 256),
        a[:, o + DECAY_LORA + ICLR_LORA + GATE_LORA:],
    ], axis=1)


def _rope_tables(positions):
    half = ROPE_DIM // 2
    inv_freq = ROPE_THETA ** (-jnp.arange(half, dtype=f32) / half)
    ang = positions.astype(f32)[..., None] * inv_freq
    cos, sin = jnp.cos(ang), jnp.sin(ang)
    B, S = positions.shape
    cosf = jnp.concatenate([cos, cos, jnp.ones((B, S, ATTN_E - ROPE_DIM), f32)], axis=-1)
    sinf = jnp.concatenate([-sin, sin, jnp.zeros((B, S, ATTN_E - ROPE_DIM), f32)], axis=-1)
    return cosf, sinf


def _route(routing, E, tm, n_tiles):
    T = routing.shape[0]
    n = T * TOP_K
    i32 = jnp.int32
    gates = routing[:, :TOP_K]
    top_idx = routing[:, TOP_K:2 * TOP_K].astype(i32)
    flat_e = top_idx.reshape(-1).astype(i32)
    iota = jnp.arange(n, dtype=i32)
    experts = jnp.arange(E, dtype=i32)

    def lookup(table, idx):
        return jnp.sum(jnp.where(idx[:, None] == experts[None, :], table[None, :], 0), axis=1)

    _, order = lax.sort((flat_e, iota), num_keys=1, is_stable=True)
    _, rank = lax.sort((order, iota), num_keys=1)
    sizes = jnp.sum((flat_e[:, None] == experts[None, :]).astype(i32), axis=0)
    padded = ((sizes + tm - 1) // tm) * tm
    pad_end = jnp.cumsum(padded)
    pad_start = pad_end - padded
    start = jnp.cumsum(sizes) - sizes
    pos = lookup(pad_start - start, flat_e) + rank
    tile_start = jnp.arange(n_tiles, dtype=i32) * tm
    tile_e = jnp.minimum(jnp.sum((pad_end[None, :] <= tile_start[:, None]).astype(i32), axis=1), E - 1)
    n_valid = (pad_end[-1] // tm).astype(i32).reshape(1)
    row_e = jnp.repeat(tile_e, tm)
    rows = jnp.arange(n_tiles * tm, dtype=i32)
    off = rows - lookup(pad_start, row_e)
    src = jnp.clip(lookup(start, row_e) + off, 0, n - 1)
    tok = jnp.where(off < lookup(sizes, row_e), order[src] // TOP_K, rows % T)
    pos_slot_major = pos.reshape(T, TOP_K).T.reshape(-1)
    return tok, gates, pos_slot_major, tile_e, n_valid, pad_start // tm, padded // tm


def kernel(x, positions, norm_mix, w_in, shift_mu, w0, w2_decay, a0, a2_iclr, g2_gate, k_k, k_a,
           r_k, ln_x_w, ln_x_b, w_o_rwkv, w_o_attn, w_out, norm_ffn, w_router, b_router,
           w1, b1, w2, b2, norm_final):
    B, S, D = x.shape
    T = B * S
    E = w_router.shape[-1]
    depth = norm_mix.shape[0]
    xt = x.reshape(T, D)
    cosf, sinf = _rope_tables(positions)
    moe_tm = 512
    n_tiles = (T * TOP_K) // moe_tm + E
    for layer in range(depth):
        wp = _regroup_cols(w_in[layer]).astype(bf16)
        mu = _regroup_cols(shift_mu[layer][None, :])
        proj = _proj(xt, norm_mix[layer][None, :], wp)
        proj3 = proj.reshape(B, S, -1)
        rw = _rwkv(proj3, mu, w0[layer][None, :], _stack_rhs(_pad_rows(w2_decay[layer], 128)),
                   a0[layer][None, :], _stack_rhs(_pad_rows(a2_iclr[layer], 128)),
                   _pad_rows(g2_gate[layer], 256).astype(bf16), k_k[layer][None, :],
                   k_a[layer][None, :],
                   r_k[layer].reshape(1, -1), ln_x_w[layer][None, :], ln_x_b[layer][None, :])
        outs, lses = [], []
        for gi, (window, dil) in enumerate(ATTN_GROUPS):
            o_g, l_g = _attn_group(proj3, cosf, sinf, gi, dil)
            outs.append(o_g)
            lses.append(l_g)
        x1, h2, routing = _post(
            xt, rw.reshape(T, -1), outs, lses, proj,
            w_o_rwkv[layer].astype(bf16), w_o_attn[layer].astype(bf16), w_out[layer].astype(bf16),
            norm_ffn[layer][None, :], jnp.concatenate(_split2(_pad_cols(w_router[layer], 128)), axis=1),
            _pad_cols(b_router[layer][None, :], 128), E)
        tok, gates, pos, tile_e, n_valid, first_tile, n_tiles_e = _route(routing, E, moe_tm, n_tiles)
        ys = _moe(tile_e, n_valid, first_tile, n_tiles_e, h2, tok, w1[layer], b1[layer][:, None, :],
                  w2[layer], b2[layer][:, None, :], moe_tm)
        xt = _final(x1, ys, pos, gates, norm_final[None, :], apply_norm=layer + 1 == depth)
    return xt.reshape(B, S, D)
```

```python
import functools

import jax
import jax.numpy as jnp
from jax import lax
from jax.experimental import pallas as pl
from jax.experimental.pallas import tpu as pltpu

f32 = jnp.float32
bf16 = jnp.bfloat16

NORM_EPS = 1e-5
RWKV_GN_EPS = 64e-5
ROPE_THETA = 500000.0
SWIGLU_LIMIT = 7.0
SWIGLU_ALPHA = 1.702

RWKV_HEADS = 16
HEAD_N = 64
RWKV_WIDTH = RWKV_HEADS * HEAD_N
DECAY_LORA = 64
ICLR_LORA = 64
GATE_LORA = 160
ATTN_GROUPS = ((128, 1), (512, 4), (2048, 16))
HEADS_PER_GROUP = 4
ATTN_E = 128
ATTN_GW = HEADS_PER_GROUP * ATTN_E
ATTN_WIDTH = len(ATTN_GROUPS) * ATTN_GW
ROPE_DIM = ATTN_E // 4
TOP_K = 4

LORA_PAD = 512
ZW_OFF, ZA_OFF, ZG_OFF = 3 * RWKV_WIDTH, 3 * RWKV_WIDTH + 128, 3 * RWKV_WIDTH + 256
SHIFT_PAD = 3 * RWKV_WIDTH + LORA_PAD
ATTN_OFF = SHIFT_PAD
GATE_OFF = ATTN_OFF + 3 * ATTN_WIDTH

RWKV_CHUNK = 64
_MOE_CHUNK_TENTHS = (0, 1, 4, 7, 10)
_FINAL_CHUNKS = 1
_RWKV_SEQS = 4
_RWKV_NWIDE = 12
ATTN_BLK = 128
_STRIDE_SPLIT = 4
_ATTN_UNITS = 16
VMEM_LIMIT = 56 * 1024 * 1024


def _dot(a, b, dims=(((1,), (0,)), ((), ())), precision=None):
    return lax.dot_general(a, b, dims, precision=precision, preferred_element_type=f32)


_NT = (((1,), (1,)), ((), ()))
_TN = (((0,), (0,)), ((), ()))


def _split2(a):
    hi = a.astype(bf16)
    return hi, (a - hi.astype(f32)).astype(bf16)


def _split3(a):
    hi = a.astype(bf16)
    rem = a - hi.astype(f32)
    mid = rem.astype(bf16)
    return hi, mid, (rem - mid.astype(f32)).astype(bf16)


def _stack_lhs(a):
    hi, lo = _split2(a)
    return jnp.concatenate([hi, lo, hi], axis=1)


def _stack_rhs(w):
    hi, lo = _split2(w)
    return jnp.concatenate([hi, hi, lo], axis=0)


def _proj_kernel(x_ref, g_ref, w_ref, o_ref, h_ref):
    @pl.when(pl.program_id(1) == 0)
    def _():
        x = x_ref[...]
        ms = jnp.mean(x * x, axis=-1, keepdims=True)
        h_ref[...] = (x * lax.rsqrt(ms + NORM_EPS) * g_ref[...]).astype(bf16)

    o_ref[...] = _dot(h_ref[...], w_ref[...])


def _proj(x2, g, wp, tm=1024, tn=2048):
    T, D = x2.shape
    NP = wp.shape[1]
    return pl.pallas_call(
        _proj_kernel,
        out_shape=jax.ShapeDtypeStruct((T, NP), f32),
        grid=(T // tm, NP // tn),
        in_specs=[pl.BlockSpec((tm, D), lambda i, j: (i, 0)),
                  pl.BlockSpec((1, D), lambda i, j: (0, 0)),
                  pl.BlockSpec((D, tn), lambda i, j: (0, j))],
        out_specs=pl.BlockSpec((tm, tn), lambda i, j: (i, j)),
        scratch_shapes=[pltpu.VMEM((tm, D), bf16)],
        compiler_params=pltpu.CompilerParams(
            dimension_semantics=("arbitrary", "arbitrary"), vmem_limit_bytes=VMEM_LIMIT),
        name="proj",
    )(x2, g, wp)


def _rwkv_kernel(p_ref, mu_ref, w0_ref, w2_ref, a0_ref, a2_ref, g2_ref, kk_ref, ka_ref,
                 rk_ref, lnw_ref, lnb_ref, o_ref, state_ref, prev_ref, dec_ref, wide_ref):
    (at0_ref, at_ref, rt0_ref, rt_ref, bt_ref, kt_ref, bh_ref, kh_ref, v_ref, rkb_ref, g_ref,
     kkr_ref) = [wide_ref.at[i] for i in range(_RWKV_NWIDE)]
    L = RWKV_CHUNK
    W = RWKV_WIDTH
    c_idx = pl.program_id(1)

    @pl.when(c_idx == 0)
    def _():
        state_ref[...] = jnp.zeros_like(state_ref)
        prev_ref[...] = jnp.zeros_like(prev_ref)

    RB = p_ref.shape[0]
    RL = RB * L
    p = p_ref[...].reshape(RL, p_ref.shape[2])
    row = lax.broadcasted_iota(jnp.int32, (RL, 1), 0)
    shifted = pltpu.roll(p, 1, axis=0)
    for bi in range(RB):
        shifted = jnp.where(row == bi * L, prev_ref[bi:bi + 1, :], shifted)
        prev_ref[bi:bi + 1, :] = p[bi * L + L - 1:bi * L + L, :]
    z = p + (shifted - p) * mu_ref[...]

    def per_seq(x, r):
        return jnp.concatenate(
            [jnp.broadcast_to(x[bi * L + r:bi * L + r + 1, :], (L, x.shape[1])) for bi in range(RB)],
            axis=0)

    r = z[:, 0:W]
    k = z[:, W:2 * W]
    v = z[:, 2 * W:3 * W]
    zw = z[:, ZW_OFF:ZW_OFF + 128]
    za = z[:, ZA_OFF:ZA_OFF + 128]
    zg = z[:, ZG_OFF:ZG_OFF + 256]

    w_raw = w0_ref[...] + _dot(_stack_lhs(jnp.tanh(zw)), w2_ref[...])
    sp = jnp.maximum(-w_raw, 0.0) + jnp.log(1.0 + jnp.exp(-jnp.abs(w_raw)))
    lw = -jnp.exp(-sp - 0.5)
    a = jax.nn.sigmoid(a0_ref[...] + _dot(_stack_lhs(za), a2_ref[...]))
    g_ref[...] = _dot(jax.nn.sigmoid(zg).astype(bf16), g2_ref[...])

    ti = lax.broadcasted_iota(jnp.int32, (RL, RL), 0)
    si = lax.broadcasted_iota(jnp.int32, (RL, RL), 1)
    tri = jnp.logical_and(ti >= si, ti // L == si // L).astype(bf16)
    lw_h, lw_m, lw_l = _split3(lw)
    c = _dot(jnp.concatenate([tri, tri, tri], axis=1), jnp.concatenate([lw_h, lw_m, lw_l], axis=0))
    cex = c - lw
    cm = per_seq(c, L // 2 - 1)
    cl = per_seq(c, L - 1)
    e_m = jnp.exp(-cm)
    e_c = jnp.exp(c)
    e_ex = jnp.exp(cex)
    e_inv = jnp.exp(cm - c)
    e_tail = jnp.exp(cl - c)

    kkv = k * kk_ref[...]
    kmod = k * (1.0 + (a - 1.0) * ka_ref[...])
    at0_ref[...] = e_ex
    rt0 = r * e_c
    rt0_ref[...] = rt0
    rt_ref[...] = rt0 * e_m
    kt_ref[...] = kmod * e_inv
    kh_ref[...] = kmod * e_tail
    bt_ref[...] = a * e_inv
    bh_ref[...] = a * e_tail
    at_ref[...] = e_ex * e_m
    v_ref[...] = v
    for bi in range(RB):
        dec_ref[bi:bi + 1, :] = jnp.exp(c[bi * L + L - 1:bi * L + L, :])
    rkb_ref[...] = r * kmod * rk_ref[...]
    kkr_ref[...] = kkv

    PW = 2 * HEAD_N
    pairs = range(RWKV_HEADS // 2)
    lane = lax.broadcasted_iota(jnp.int32, (L, PW), 1)
    trow = lax.broadcasted_iota(jnp.int32, (L, PW), 0)
    first = lane < HEAD_N
    scol = jnp.where(first, lane, lane - HEAD_N)
    strict_p = trow > scol
    incl_p = trow >= scol
    eye_p = (trow == scol).astype(f32)
    rb =lax.broadcasted_iota(jnp.int32, (PW, PW), 0)
    lb = lax.broadcasted_iota(jnp.int32, (PW, PW), 1)
    bd_mask = (rb < HEAD_N) == (lb < HEAD_N)

    def bd(x):
        zero = jnp.zeros_like(x)
        return jnp.concatenate([jnp.where(first, x, zero), jnp.where(first, zero, x)], axis=0)

    def segsum(x):
        s0 = jnp.sum(jnp.where(first, x, 0.0), axis=-1, keepdims=True)
        s1 = jnp.sum(jnp.where(first, 0.0, x), axis=-1, keepdims=True)
        return jnp.where(first, s0, s1)

    units = [(bi, p) for bi in range(RB) for p in pairs]
    ps = [(slice(bi * L, (bi + 1) * L), slice(p * PW, (p + 1) * PW)) for bi, p in units]
    kkr = [kkr_ref[s] for s in ps]
    ssq = [segsum(x * x) for x in kkr]
    kkh = [x / jnp.maximum(jnp.sqrt(q), 1e-12) for x, q in zip(kkr, ssq)]
    at0 = [(-kh_ * at0_ref[s]).astype(bf16) for kh_, s in zip(kkh, ps)]
    at = [(-kh_ * at_ref[s]).astype(bf16) for kh_, s in zip(kkh, ps)]
    bt = [(kh_ * bt_ref[s]).astype(bf16) for kh_, s in zip(kkh, ps)]
    bh = [(kh_ * bh_ref[s]).astype(bf16) for kh_, s in zip(kkh, ps)]
    vb = [v_ref[s].astype(bf16) for s in ps]

    lhs = [jnp.concatenate([a_, rt_ref[s].astype(bf16)], axis=0) for a_, s in zip(at, ps)]
    sb = [_dot(l_, bd(b_), _NT) for l_, b_ in zip(lhs, bt)]
    sk = [_dot(l_, bd(kt_ref[s].astype(bf16)), _NT) for l_, s in zip(lhs, ps)]
    a_ab = [jnp.where(strict_p, x[:L], 0.0) for x in sb]
    a_rb = [jnp.where(incl_p, x[L:], 0.0).astype(bf16) for x in sb]
    a_ak = [jnp.where(strict_p, x[:L], 0.0).astype(bf16) for x in sk]
    a_rk = [jnp.where(incl_p, x[L:], 0.0).astype(bf16) for x in sk]

    tinv = [eye_p + x for x in a_ab]
    xp = a_ab
    n = 2
    while n < L:
        xb = [x.astype(bf16) for x in xp]
        xp = [_dot(x, bd(x)) for x in xb]
        tinv = [t + _dot(t.astype(bf16), bd(x.astype(bf16))) for t, x in zip(tinv, xp)]
        n *= 2
    tb = [t.astype(bf16) for t in tinv]

    akv = [_dot(a_, bd(v_)).astype(bf16) for a_, v_ in zip(a_ak, vb)]
    w12 = [_dot(t, jnp.concatenate([bd(a_), bd(k_)], axis=1)) for t, a_, k_ in zip(tb, at0, akv)]

    n_pairs = len(pairs)
    s0 = [state_ref[bi * n_pairs + p] for bi, p in units]
    s0b = [s.astype(bf16) for s in s0]
    uy = [_dot(jnp.concatenate([w[:, :PW].astype(bf16), rt0_ref[s].astype(bf16)], axis=0), sb_, _NT)
          for w, s, sb_ in zip(w12, ps, s0b)]
    u = [x[:L] + w[:, PW:] for x, w in zip(uy, w12)]
    ub = [x.astype(bf16) for x in u]
    y = [x[L:] + _dot(jnp.concatenate([rb_, rk_], axis=1), jnp.concatenate([bd(u_), bd(v_)], axis=0))
         for x, rb_, rk_, u_, v_ in zip(uy, a_rb, a_rk, ub, vb)]
    for i, (bi, p) in enumerate(units):
        upd = _dot(jnp.concatenate([ub[i], vb[i]], axis=0),
                   jnp.concatenate([bh[i], kh_ref[ps[i]].astype(bf16)], axis=0), _TN)
        state_ref[bi * n_pairs + p] = (s0[i] * dec_ref[bi:bi + 1, ps[i][1]]
                                       + jnp.where(bd_mask, upd, 0.0))

    inv_n = 1.0 / HEAD_N
    mean = [segsum(x) * inv_n for x in y]
    yc = [x - m for x, m in zip(y, mean)]
    var = [segsum(x * x) * inv_n for x in yc]
    bonus = [segsum(rkb_ref[s]) for s in ps]
    for i, (bi, p) in enumerate(units):
        s = ps[i]
        yn = yc[i] * lax.rsqrt(var[i] + RWKV_GN_EPS) * lnw_ref[:, s[1]] + lnb_ref[:, s[1]]
        o_ref[bi, :, s[1]] = ((yn + bonus[i] * v_ref[s]) * g_ref[s]).astype(o_ref.dtype)


def _rwkv(proj3, mu, w0, w2d, a0, a2, g2, k_k, k_a, r_k, ln_w, ln_b):
    B, S, NP = proj3.shape
    L, W = RWKV_CHUNK, RWKV_WIDTH
    row = lambda n: pl.BlockSpec((1, n), lambda b, c: (0, 0))
    full = lambda a: pl.BlockSpec(a.shape, lambda b, c: (0, 0))
    RB = _RWKV_SEQS if B % _RWKV_SEQS == 0 else 1
    return pl.pallas_call(
        _rwkv_kernel,
        out_shape=jax.ShapeDtypeStruct((B, S, W), bf16),
        grid=(B // RB, S // L),
        in_specs=[pl.BlockSpec((RB, L, SHIFT_PAD), lambda b, c: (b, c, 0)),
                  row(SHIFT_PAD), row(W), full(w2d), row(W), full(a2), full(g2),
                  row(W), row(W), row(W), row(W), row(W)],
        out_specs=pl.BlockSpec((RB, L, W), lambda b, c: (b, c, 0)),
        scratch_shapes=[pltpu.VMEM((RB * RWKV_HEADS // 2, 2 * HEAD_N, 2 * HEAD_N), f32),
                        pltpu.VMEM((RB, SHIFT_PAD), f32), pltpu.VMEM((RB, W), f32),
                        pltpu.VMEM((_RWKV_NWIDE, RB * L, W), f32)],
        compiler_params=pltpu.CompilerParams(
            dimension_semantics=("arbitrary", "arbitrary"), vmem_limit_bytes=VMEM_LIMIT),
        name="rwkv",
    )(proj3, mu, w0, w2d, a0, a2, g2, k_k, k_a, r_k, ln_w, ln_b)


def _rope(x, cosf, sinf, lane):
    rot = jnp.where(lane < ROPE_DIM // 2, pltpu.roll(x, ATTN_E - ROPE_DIM // 2, axis=1),
                    pltpu.roll(x, ROPE_DIM // 2, axis=1))
    return x * cosf + rot * sinf


def _attn_kernel(q_ref, kc_ref, kp_ref, vc_ref, vp_ref, cc_ref, sc_ref, cp_ref, sp_ref,
                 o_ref, l_ref, *stage, dil, hps, nblk, wave):
    n = pl.program_id(1)
    blk = ATTN_BLK
    span = blk * dil
    qi = lax.broadcasted_iota(jnp.int32, (blk, blk), 0)
    kj = lax.broadcasted_iota(jnp.int32, (blk, blk), 1)
    lane = lax.broadcasted_iota(jnp.int32, (blk, ATTN_E), 1)
    cur_ok = kj <= qi
    prev_ok = kj >= qi
    first_ok = jnp.logical_and(prev_ok, n > 0)
    scale = ATTN_E ** -0.5

    ins = (q_ref, kc_ref, kp_ref, vc_ref, vp_ref, cc_ref, sc_ref, cp_ref, sp_ref)
    outs = (o_ref, l_ref)
    Q, KC, KP, VC, VP, CC, SC, CP, SP = range(len(ins))
    sub = dil // _STRIDE_SPLIT if stage else dil

    def rows(b, r, d):
        return pl.ds(b * span + r, blk, stride=d) if d > 1 else pl.ds(b * span, blk)

    def cols(hh):
        return slice(hh * ATTN_E, (hh + 1) * ATTN_E)

    if stage:
        in_stage, out_stage = stage
        quarter = span // _STRIDE_SPLIT
        for xi, ref in enumerate(ins):
            for r4 in range(_STRIDE_SPLIT):
                in_stage[xi, r4] = ref[0, pl.ds(r4, quarter, stride=_STRIDE_SPLIT), :]

        def rd(xi, b, r, hh):
            return in_stage[xi, r % _STRIDE_SPLIT, rows(0, r // _STRIDE_SPLIT, sub), :]

        def wr(oi, b, r, hh, val):
            out_stage[oi, r % _STRIDE_SPLIT, rows(0, r // _STRIDE_SPLIT, sub), :] = val
    else:
        def rd(xi, b, r, hh):
            return ins[xi][0, rows(b, r, dil), cols(hh) if xi < CC else slice(None)]

        def wr(oi, b, r, hh, val):
            outs[oi][0, rows(b, r, dil), cols(hh)] = val

    all_units = [(b, r, hh) for b in range(nblk) for r in range(dil) for hh in range(hps)]
    kc, vc = {}, {}
    for w0 in range(0, len(all_units), wave):
        units = all_units[w0:w0 + wave]
        tab = {u[:2]: (rd(CC, *u), rd(SC, *u)) for u in units}
        q = {u: _rope(rd(Q, *u), *tab[u[:2]], lane).astype(bf16) for u in units}
        for u in units:
            kc[u] = _rope(rd(KC, *u), *tab[u[:2]], lane).astype(bf16)
            vc[u] = rd(VC, *u).astype(bf16)
        kp, vp = {}, {}
        for (b, r, hh) in units:
            if b == 0:
                ptab = (rd(CP, 0, r, hh), rd(SP, 0, r, hh))
                kp[(b, r, hh)] = _rope(rd(KP, 0, r, hh), *ptab, lane).astype(bf16)
                vp[(b, r, hh)] = rd(VP, 0, r, hh).astype(bf16)
            else:
                kp[(b, r, hh)] = kc[(b - 1, r, hh)]
                vp[(b, r, hh)] = vc[(b - 1, r, hh)]
        s_c = {u: jnp.where(cur_ok, _dot(q[u], kc[u], _NT) * scale, -jnp.inf) for u in units}
        s_p = {u: jnp.where(first_ok if u[0] == 0 else prev_ok, _dot(q[u], kp[u], _NT) * scale, -jnp.inf)
               for u in units}
        m = {u: jnp.maximum(jnp.max(s_c[u], axis=-1, keepdims=True),
                            jnp.max(s_p[u], axis=-1, keepdims=True)) for u in units}
        p_c = {u: jnp.exp(s_c[u] - m[u]) for u in units}
        p_p = {u: jnp.exp(s_p[u] - m[u]) for u in units}
        l = {u: jnp.sum(p_c[u], axis=-1, keepdims=True) + jnp.sum(p_p[u], axis=-1, keepdims=True)
             for u in units}
        acc = {u: _dot(p_c[u].astype(bf16), vc[u]) + _dot(p_p[u].astype(bf16), vp[u]) for u in units}
        for u in units:
            wr(0, *u, acc[u] / l[u])
            wr(1, *u, jnp.broadcast_to(m[u] + jnp.log(l[u]), (blk, ATTN_E)))
    if stage:
        for oi, ref in enumerate(outs):
            for r4 in range(_STRIDE_SPLIT):
                ref[0, pl.ds(r4, quarter, stride=_STRIDE_SPLIT), :] = out_stage[oi, r4]


def _attn_group(proj3, cosf, sinf, gi, dil):
    B, S, NP = proj3.shape
    span = ATTN_BLK * dil
    hps = HEADS_PER_GROUP if dil == 1 else 1
    nblk = _ATTN_UNITS // (dil * hps)
    R = span * nblk
    cw = hps * ATTN_E
    per = ATTN_GW // cw
    qo = (ATTN_OFF + gi * ATTN_GW) // cw
    ko = qo + ATTN_WIDTH // cw
    vo = ko + ATTN_WIDTH // cw

    def cur(off):
        return pl.BlockSpec((1, R, cw), lambda b, n, h: (b, n, off + h))

    def prev(off):
        return pl.BlockSpec((1, span, cw), lambda b, n, h: (b, jnp.maximum(n * nblk - 1, 0), off + h))

    tc = pl.BlockSpec((1, R, ATTN_E), lambda b, n, h: (b, n, 0))
    tp = pl.BlockSpec((1, span, ATTN_E), lambda b, n, h: (b, jnp.maximum(n * nblk - 1, 0), 0))
    out = pl.BlockSpec((1, R, cw), lambda b, n, h: (b, n, h))
    staged = dil > _STRIDE_SPLIT and nblk == 1 and hps == 1
    scratch = ([pltpu.VMEM((9, _STRIDE_SPLIT, R // _STRIDE_SPLIT, ATTN_E), f32),
                pltpu.VMEM((2, _STRIDE_SPLIT, R // _STRIDE_SPLIT, ATTN_E), f32)] if staged else [])
    o, l = pl.pallas_call(
        functools.partial(_attn_kernel, dil=dil, hps=hps, nblk=nblk,
                          wave=_ATTN_UNITS // 2 if dil > HEADS_PER_GROUP else _ATTN_UNITS),
        out_shape=[jax.ShapeDtypeStruct((B, S, ATTN_GW), f32)] * 2,
        grid=(B, S // R, per),
        in_specs=[cur(qo), cur(ko), prev(ko), cur(vo), prev(vo), tc, tc, tp, tp],
        out_specs=[out, out],
        scratch_shapes=scratch,
        compiler_params=pltpu.CompilerParams(
            dimension_semantics=("arbitrary", "arbitrary", "arbitrary"),
            vmem_limit_bytes=VMEM_LIMIT),
        name=f"attn_g{gi}",
    )(proj3, proj3, proj3, proj3, proj3, cosf, sinf, cosf, sinf)
    return o.reshape(B * S, ATTN_GW), l.reshape(B * S, ATTN_GW)


def _merge_kernel(rw_ref, o0_ref, o1_ref, o2_ref, l0_ref, l1_ref, l2_ref, ga_ref, gb_ref,
                  wor_ref, woa_ref, mg_ref):
    l0, l1, l2 = l0_ref[...], l1_ref[...], l2_ref[...]
    m = jnp.maximum(jnp.maximum(l0, l1), l2)
    e0, e1, e2 = jnp.exp(l0 - m), jnp.exp(l1 - m), jnp.exp(l2 - m)
    o = (e0 * o0_ref[...] + e1 * o1_ref[...] + e2 * o2_ref[...]) / (e0 + e1 + e2)
    y_b = _dot(o.astype(bf16), woa_ref[...])
    y_a = _dot(rw_ref[...], wor_ref[...])
    sig_a = 0.5 * (1.0 + jnp.tanh(0.5 * ga_ref[...]))
    sig_b = 0.5 * (1.0 + jnp.tanh(0.5 * gb_ref[...]))
    merged = sig_a * y_a + sig_b * y_b
    mg_ref[...] = merged.astype(bf16)


def _out_kernel(x_ref, mg_ref, wout_ref, gn_ref, wr_ref, br_ref, x1_ref, h2_ref, lg_ref):
    x1 = x_ref[...] + _dot(mg_ref[...], wout_ref[...])
    x1_ref[...] = x1
    ms = jnp.mean(x1 * x1, axis=-1, keepdims=True)
    h2 = x1 * lax.rsqrt(ms + NORM_EPS) * gn_ref[...]
    h2_ref[...] = h2.astype(bf16)
    hi, lo = _split2(h2)
    n_e = lg_ref.shape[1]
    both = _dot(hi, wr_ref[...])
    lg_ref[...] = both[:, :n_e] + both[:, n_e:] + _dot(lo, wr_ref[:, :n_e]) + br_ref[...]


def _post(x2, rw, outs, lses, proj, wor, woa, wout, gn, wr, br, tm=512):
    T, D = x2.shape
    row = lambda w: pl.BlockSpec((tm, w), lambda i: (i, 0))
    const = lambda a: pl.BlockSpec(a.shape, lambda i: (0, 0), pipeline_mode=pl.Buffered(1))
    ga_blk = GATE_OFF // D
    params = pltpu.CompilerParams(dimension_semantics=("arbitrary",), vmem_limit_bytes=VMEM_LIMIT)
    merged = pl.pallas_call(
        _merge_kernel,
        out_shape=jax.ShapeDtypeStruct((T, D), bf16),
        grid=(T // tm,),
        in_specs=[row(RWKV_WIDTH)] + [row(ATTN_GW)] * 6
                 + [pl.BlockSpec((tm, D), lambda i: (i, ga_blk)),
                    pl.BlockSpec((tm, D), lambda i: (i, ga_blk + 1)),
                    const(wor), const(woa)],
        out_specs=row(D),
        compiler_params=params,
        name="merge",
    )(rw, *outs, *lses, proj, proj, wor, woa)
    return pl.pallas_call(
        _out_kernel,
        out_shape=[jax.ShapeDtypeStruct((T, D), f32), jax.ShapeDtypeStruct((T, D), bf16),
                   jax.ShapeDtypeStruct((T, 128), f32)],
        grid=(T // tm,),
        in_specs=[row(D), row(D), const(wout), const(gn), const(wr), const(br)],
        out_specs=[row(D), row(D), row(128)],
        compiler_params=params,
        name="out",
    )(x2, merged, wout, gn, wr, br)


def _moe_up_kernel(st, sc, se, sf, nv, xs_ref, w1g_ref, w1l_ref, b1g_ref, b1l_ref, *rest):
    o_ref, wg_ref, wl_ref = rest[-3:]
    s = pl.program_id(0)

    @pl.when(s < nv[0])
    def _():
        @pl.when(sf[s] == 1)
        def _():
            wg_ref[...] = w1g_ref[...].astype(bf16)
            wl_ref[...] = w1l_ref[...].astype(bf16)

        xs = xs_ref[...]
        hg = _dot(xs, wg_ref[...]) + b1g_ref[...]
        hl = _dot(xs, wl_ref[...]) + b1l_ref[...]
        x_glu = jnp.minimum(hg, SWIGLU_LIMIT)
        x_lin = jnp.clip(hl, -SWIGLU_LIMIT, SWIGLU_LIMIT)
        act = x_glu * jax.nn.sigmoid(SWIGLU_ALPHA * x_glu) * (x_lin + 1.0)
        o_ref[...] = act.astype(o_ref.dtype)


def _moe_down_kernel(st, sc, se, sf, nv, a_ref, w2_ref, b2_ref, o_ref, w_ref):
    s = pl.program_id(0)

    @pl.when(s < nv[0])
    def _():
        @pl.when(sf[s] == 1)
        def _():
            w_ref[...] = w2_ref[...].astype(bf16)

        o_ref[...] = (_dot(a_ref[...], w_ref[...]) + b2_ref[...]).astype(o_ref.dtype)


def _moe_schedule(tile_e, n_valid, first_tile, n_tiles_e, nc, lo, hi):
    i32 = jnp.int32
    E = first_tile.shape[0]
    experts = jnp.arange(E, dtype=i32)
    nv = jnp.clip(n_valid[0], lo, hi) - lo
    s = jnp.arange((hi - lo) * nc, dtype=i32)
    s = jnp.clip(s, 0, jnp.maximum(nc * nv - 1, 0))
    e = tile_e[lo:hi][s // nc]
    pick = lambda table: jnp.sum(jnp.where(e[:, None] == experts[None, :], table[None, :], 0), axis=1)
    ft = jnp.clip(first_tile, lo, hi)
    ne = jnp.clip(first_tile + n_tiles_e, lo, hi) - ft
    ft, ne = pick(ft), jnp.maximum(pick(ne), 1)
    local = s - nc * (ft - lo)
    col = local // ne
    row = ft + local % ne
    changed = jnp.logical_or(e != jnp.roll(e, 1), col != jnp.roll(col, 1)).at[0].set(True)
    return (row.astype(i32), col.astype(i32), e.astype(i32), changed.astype(i32),
            (nc * nv).astype(i32).reshape(1))


def _moe(tile_e, n_valid, first_tile, n_tiles_e, h2, tok, w1, b1, w2, b2, tm, tf=1024, tn=2048):
    P = tok.shape[0]
    D = h2.shape[1]
    E, _, F2 = w1.shape
    F = F2 // 2
    nj = F // tf
    nn = D // tn
    nt = P // tm
    params = pltpu.CompilerParams(dimension_semantics=("arbitrary",), vmem_limit_bytes=VMEM_LIMIT)

    bounds = sorted({(f * nt) // _MOE_CHUNK_TENTHS[-1] for f in _MOE_CHUNK_TENTHS})
    act = None
    for ck, (lo, hi) in enumerate(zip(bounds[:-1], bounds[1:])):
        ct = hi - lo
        sched = _moe_schedule(tile_e, n_valid, first_tile, n_tiles_e, nj, lo, hi)
        xs_k = h2[tok[lo * tm:hi * tm]]
        carry = [] if act is None else [act]
        act = pl.pallas_call(
            _moe_up_kernel,
            out_shape=jax.ShapeDtypeStruct((P, F), bf16),
            grid_spec=pltpu.PrefetchScalarGridSpec(
                num_scalar_prefetch=5,
                grid=(ct * nj,),
                in_specs=[
                    pl.BlockSpec((tm, D), lambda s, st, sc, se, sf, nv, lo=lo: (st[s] - lo, 0)),
                    pl.BlockSpec((None, D, tf), lambda s, st, sc, se, sf, nv: (se[s], 0, sc[s])),
                    pl.BlockSpec((None, D, tf), lambda s, st, sc, se, sf, nv: (se[s], 0, nj + sc[s])),
                    pl.BlockSpec((None, 1, tf), lambda s, st, sc, se, sf, nv: (se[s], 0, sc[s])),
                    pl.BlockSpec((None, 1, tf), lambda s, st, sc, se, sf, nv: (se[s], 0, nj + sc[s])),
                ] + [pl.BlockSpec(memory_space=pl.ANY)] * len(carry),
                out_specs=pl.BlockSpec((tm, tf), lambda s, st, sc, se, sf, nv: (st[s], sc[s])),
                scratch_shapes=[pltpu.VMEM((D, tf), bf16), pltpu.VMEM((D, tf), bf16)],
            ),
            input_output_aliases={10: 0} if carry else {},
            compiler_params=params,
            name=f"moe_up{ck}",
        )(*sched, xs_k, w1, w1, b1, b1, *carry)

    sched = _moe_schedule(tile_e, n_valid, first_tile, n_tiles_e, nn, 0, nt)
    return pl.pallas_call(
        _moe_down_kernel,
        out_shape=jax.ShapeDtypeStruct((P, D), bf16),
        grid_spec=pltpu.PrefetchScalarGridSpec(
            num_scalar_prefetch=5,
            grid=(nt * nn,),
            in_specs=[
                pl.BlockSpec((tm, F), lambda s, st, sc, se, sf, nv: (st[s], 0)),
                pl.BlockSpec((None, F, tn), lambda s, st, sc, se, sf, nv: (se[s], 0, sc[s])),
                pl.BlockSpec((None, 1, tn), lambda s, st, sc, se, sf, nv: (se[s], 0, sc[s])),
            ],
            out_specs=pl.BlockSpec((tm, tn), lambda s, st, sc, se, sf, nv: (st[s], sc[s])),
            scratch_shapes=[pltpu.VMEM((F, tn), bf16)],
        ),
        compiler_params=params,
        name="moe_down",
    )(*sched, act, w2, b2)


def _final_kernel(x1_ref, y_ref, gate_ref, g_ref, *rest, apply_norm):
    o_ref = rest[-1]
    gates = gate_ref[...]
    y = gates[:, 0:1] * y_ref[0].astype(f32)
    for kk in range(1, TOP_K):
        y = y + gates[:, kk:kk + 1] * y_ref[kk].astype(f32)
    x2 = x1_ref[...] + y
    if apply_norm:
        ms = jnp.mean(x2 * x2, axis=-1, keepdims=True)
        x2 = x2 * lax.rsqrt(ms + NORM_EPS) * g_ref[...]
    o_ref[...] = x2


def _final(x1, ys, pos, gates, g, apply_norm, tm=256):
    T, D = x1.shape
    n_chunks = _FINAL_CHUNKS if T % (_FINAL_CHUNKS * tm) == 0 else 1
    tc = T // n_chunks
    nb = tc // tm
    pos = pos.reshape(TOP_K, T)
    out = None
    for ck in range(n_chunks):
        y4 = ys[pos[:, ck * tc:(ck + 1) * tc].reshape(-1)].reshape(TOP_K, tc, D)
        carry = [] if out is None else [out]
        out = pl.pallas_call(
            functools.partial(_final_kernel, apply_norm=apply_norm),
            out_shape=jax.ShapeDtypeStruct((T, D), f32),
            grid=(nb,),
            in_specs=[pl.BlockSpec((tm, D), lambda i, o=ck * nb: (o + i, 0)),
                      pl.BlockSpec((TOP_K, tm, D), lambda i: (0, i, 0)),
                      pl.BlockSpec((tm, TOP_K), lambda i, o=ck * nb: (o + i, 0)),
                      pl.BlockSpec((1, D), lambda i: (0, 0))]
                     + [pl.BlockSpec(memory_space=pl.ANY)] * len(carry),
            out_specs=pl.BlockSpec((tm, D), lambda i, o=ck * nb: (o + i, 0)),
            input_output_aliases={4: 0} if carry else {},
            compiler_params=pltpu.CompilerParams(
                dimension_semantics=("arbitrary",), vmem_limit_bytes=VMEM_LIMIT),
            name=f"final{ck}",
        )(x1, y4, gates, g, *carry)
    return out


def _pad_cols(a, n):
    return jnp.pad(a, ((0, 0), (0, n - a.shape[1])))


def _pad_rows(a, n):
    return jnp.pad(a, ((0, n - a.shape[0]), (0, 0)))


def _regroup_cols(a):
    W = RWKV_WIDTH
    o = 3 * W
    return jnp.concatenate([
        a[:, :o],
        _pad_cols(a[:, o:o + DECAY_LORA], 128),
        _pad_cols(a[:, o + DECAY_LORA:o + DECAY_LORA + ICLR_LORA], 128),
        _pad_cols(a[:, o + DECAY_LORA + ICLR_LORA:o + DECAY_LORA + ICLR_LORA + GATE_LORA], 256),
        a[:, o + DECAY_LORA + ICLR_LORA + GATE_LORA:],
    ], axis=1)


def _rope_tables(positions):
    half = ROPE_DIM // 2
    inv_freq = ROPE_THETA ** (-jnp.arange(half, dtype=f32) / half)
    ang = positions.astype(f32)[..., None] * inv_freq
    cos, sin = jnp.cos(ang), jnp.sin(ang)
    B, S = positions.shape
    cosf = jnp.concatenate([cos, cos, jnp.ones((B, S, ATTN_E - ROPE_DIM), f32)], axis=-1)
    sinf = jnp.concatenate([-sin, sin, jnp.zeros((B, S, ATTN_E - ROPE_DIM), f32)], axis=-1)
    return cosf, sinf


def _route(logits, tm, n_tiles):
    T, E = logits.shape
    n = T * TOP_K
    i32 = jnp.int32
    top_val, top_idx = lax.top_k(logits, TOP_K)
    gates = jax.nn.softmax(top_val, axis=-1)
    flat_e = top_idx.reshape(-1).astype(i32)
    iota = jnp.arange(n, dtype=i32)
    experts = jnp.arange(E, dtype=i32)

    def lookup(table, idx):
        return jnp.sum(jnp.where(idx[:, None] == experts[None, :], table[None, :], 0), axis=1)

    _, order = lax.sort((flat_e, iota), num_keys=1, is_stable=True)
    _, rank = lax.sort((order, iota), num_keys=1)
    sizes = jnp.sum((flat_e[:, None] == experts[None, :]).astype(i32), axis=0)
    padded = ((sizes + tm - 1) // tm) * tm
    pad_end = jnp.cumsum(padded)
    pad_start = pad_end - padded
    start = jnp.cumsum(sizes) - sizes
    pos = lookup(pad_start - start, flat_e) + rank
    tile_start = jnp.arange(n_tiles, dtype=i32) * tm
    tile_e = jnp.minimum(jnp.sum((pad_end[None, :] <= tile_start[:, None]).astype(i32), axis=1), E - 1)
    n_valid = (pad_end[-1] // tm).astype(i32).reshape(1)
    row_e = jnp.repeat(tile_e, tm)
    rows = jnp.arange(n_tiles * tm, dtype=i32)
    off = rows - lookup(pad_start, row_e)
    src = jnp.clip(lookup(start, row_e) + off, 0, n - 1)
    tok = jnp.where(off < lookup(sizes, row_e), order[src] // TOP_K, rows % T)
    pos_slot_major = pos.reshape(T, TOP_K).T.reshape(-1)
    return tok, gates, pos_slot_major, tile_e, n_valid, pad_start // tm, padded // tm


def kernel(x, positions, norm_mix, w_in, shift_mu, w0, w2_decay, a0, a2_iclr, g2_gate, k_k, k_a,
           r_k, ln_x_w, ln_x_b, w_o_rwkv, w_o_attn, w_out, norm_ffn, w_router, b_router,
           w1, b1, w2, b2, norm_final):
    B, S, D = x.shape
    T = B * S
    E = w_router.shape[-1]
    depth = norm_mix.shape[0]
    xt = x.reshape(T, D)
    cosf, sinf = _rope_tables(positions)
    moe_tm = 512
    n_tiles = (T * TOP_K) // moe_tm + E
    for layer in range(depth):
        wp = _regroup_cols(w_in[layer]).astype(bf16)
        mu = _regroup_cols(shift_mu[layer][None, :])
        proj = _proj(xt, norm_mix[layer][None, :], wp)
        proj3 = proj.reshape(B, S, -1)
        rw = _rwkv(proj3, mu, w0[layer][None, :], _stack_rhs(_pad_rows(w2_decay[layer], 128)),
                   a0[layer][None, :], _stack_rhs(_pad_rows(a2_iclr[layer], 128)),
                   _pad_rows(g2_gate[layer], 256).astype(bf16), k_k[layer][None, :],
                   k_a[layer][None, :],
                   r_k[layer].reshape(1, -1), ln_x_w[layer][None, :], ln_x_b[layer][None, :])
        outs, lses = [], []
        for gi, (window, dil) in enumerate(ATTN_GROUPS):
            o_g, l_g = _attn_group(proj3, cosf, sinf, gi, dil)
            outs.append(o_g)
            lses.append(l_g)
        x1, h2, logits = _post(
            xt, rw.reshape(T, -1), outs, lses, proj,
            w_o_rwkv[layer].astype(bf16), w_o_attn[layer].astype(bf16), w_out[layer].astype(bf16),
            norm_ffn[layer][None, :], jnp.concatenate(_split2(_pad_cols(w_router[layer], 128)), axis=1),
            _pad_cols(b_router[layer][None, :], 128))
        tok, gates, pos, tile_e, n_valid, first_tile, n_tiles_e = _route(logits[:, :E], moe_tm, n_tiles)
        ys = _moe(tile_e, n_valid, first_tile, n_tiles_e, h2, tok, w1[layer], b1[layer][:, None, :],
                  w2[layer], b2[layer][:, None, :], moe_tm)
        xt = _final(x1, ys, pos, gates, norm_final[None, :], apply_norm=layer + 1 == depth)
    return xt.reshape(B, S, D)
```
